```python
import math
import jax, jax.numpy as jnp
from jax import lax
import numpy as np

D_MODEL = 1024
BATCH = 8
SEQ = 2048
DEPTH = 2
DEC_BATCH = 128
DEC_SEQ = 8
PAST_LEN = 16384
PAGE_SIZE = 128

N_MIXERS = 2
N_GM_LAYERS = (DEPTH + 1) // 2
N_SSM_LAYERS = DEPTH // 2
GM_CHUNK = 128
D_GM = 2 * D_MODEL
GM_GROUPS = 8
GM_GROUP_W = D_GM // GM_GROUPS
D_INNER = 2 * D_MODEL
SSM_HEAD_DIM = 64
SSM_HEADS = D_INNER // SSM_HEAD_DIM
SSM_GROUPS = 8
HEADS_PER_GROUP = SSM_HEADS // SSM_GROUPS
D_STATE = 128
CONV_W = 4
CONV_DIM = D_INNER + 2 * SSM_GROUPS * D_STATE
D_IN_PROJ = D_INNER + CONV_DIM + SSM_HEADS
SSM_CHUNK = 128
D_FF = 4 * D_MODEL
N_MOD = 6
EPS = 1e-6

kernel_name = 'hybrid_chunkgmlp_mamba2_adaln_step'


def rms_norm(x, g):
    xf = x.astype(jnp.float32)
    y = xf * lax.rsqrt(jnp.mean(xf * xf, axis=-1, keepdims=True) + EPS)
    return (y * g.astype(jnp.float32)).astype(x.dtype)


def layer_norm(x, g, b):
    xf = x.astype(jnp.float32)
    mu = jnp.mean(xf, axis=-1, keepdims=True)
    xc = xf - mu
    y = xc * lax.rsqrt(jnp.mean(xc * xc, axis=-1, keepdims=True) + EPS)
    return (y * g.astype(jnp.float32) + b.astype(jnp.float32)).astype(x.dtype)


def modulate(h, shift, scale):
    return h * (1 + scale[:, None, :]) + shift[:, None, :]


def pad_len(a, lp):
    l = a.shape[1]
    if lp == l:
        return a
    return jnp.pad(a, [(0, 0), (0, lp - l)] + [(0, 0)] * (a.ndim - 2))


def chunk_gmlp_mixer(h, w_in, ln_g, ln_b, w_s, b_s, w_out):
    bsz, l, _ = h.shape
    z = jax.nn.gelu(h @ w_in)
    u = z[..., :D_GM]
    v = layer_norm(z[..., D_GM:], ln_g, ln_b)
    q = min(l, GM_CHUNK)
    n_chunks = -(-l // q)
    lp = n_chunks * q
    vc = pad_len(v, lp).reshape(bsz, n_chunks, q, GM_GROUPS, GM_GROUP_W)
    causal = jnp.tril(jnp.ones((q, q), dtype=bool))
    ws = jnp.where(causal[None], w_s[:, :q, :q], 0)
    s = jnp.einsum('gts,bcsgw->bctgw', ws, vc) + b_s[:, :q].T[:, :, None]
    s = s.reshape(bsz, lp, D_GM)[:, :l]
    y = (u * s) @ w_out
    start = ((l - 1) // GM_CHUNK) * GM_CHUNK
    return y, v[:, start:]


def ssd_scan(x, dt, a, bm, cm, state0):
    bsz, l = x.shape[:2]
    q = min(l, SSM_CHUNK)
    nc = -(-l // q)
    lp = nc * q
    f32 = jnp.float32
    xdt = pad_len(x.astype(f32) * dt[..., None], lp).reshape(bsz, nc, q, SSM_GROUPS, HEADS_PER_GROUP, SSM_HEAD_DIM)
    da = pad_len(dt * a, lp).reshape(bsz, nc, q, SSM_GROUPS, HEADS_PER_GROUP)
    bm = pad_len(bm.astype(f32), lp).reshape(bsz, nc, q, SSM_GROUPS, D_STATE)
    cm = pad_len(cm.astype(f32), lp).reshape(bsz, nc, q, SSM_GROUPS, D_STATE)
    acum = jnp.cumsum(da, axis=2)
    causal = jnp.tril(jnp.ones((q, q), dtype=bool))[:, :, None, None]
    seg = acum[:, :, :, None] - acum[:, :, None, :]
    decay = jnp.exp(jnp.where(causal, seg, -jnp.inf))
    cb = jnp.einsum('bctgn,bcsgn->bctsg', cm, bm)
    y_diag = jnp.einsum('bctsgr,bcsgrp->bctgrp', cb[..., None] * decay, xdt)
    decay_to_end = jnp.exp(acum[:, :, -1:] - acum)
    chunk_states = jnp.einsum('bcsgn,bcsgrp->bcgrpn', bm, xdt * decay_to_end[..., None])
    chunk_decay = jnp.exp(acum[:, :, -1])

    def step(carry, inp):
        cs, cd = inp
        return carry * cd[..., None, None] + cs, carry

    s0 = state0.astype(f32).reshape(bsz, SSM_GROUPS, HEADS_PER_GROUP, SSM_HEAD_DIM, D_STATE)
    final, entering = lax.scan(step, s0, (jnp.moveaxis(chunk_states, 1, 0), jnp.moveaxis(chunk_decay, 1, 0)))
    entering = jnp.moveaxis(entering, 0, 1)
    y_off = jnp.einsum('bctgn,bcgrpn->bctgrp', cm, entering) * jnp.exp(acum)[..., None]
    y = (y_diag + y_off).reshape(bsz, lp, SSM_HEADS, SSM_HEAD_DIM)[:, :l]
    return y, final.reshape(bsz, SSM_HEADS, SSM_HEAD_DIM, D_STATE)


def mamba2_mixer(h, conv_state, ssm_state, w_in, conv_w, conv_b, dt_bias, a_log, d_skip, norm_g, w_out):
    bsz, l, _ = h.shape
    zxbcdt = h @ w_in
    z = zxbcdt[..., :D_INNER]
    xbc = zxbcdt[..., D_INNER:D_INNER + CONV_DIM]
    dt_raw = zxbcdt[..., D_INNER + CONV_DIM:]
    xp = jnp.concatenate([conv_state.astype(xbc.dtype), xbc], axis=1)
    conv = conv_b + xp[:, 0:l] * conv_w[0]
    for k in range(1, CONV_W):
        conv = conv + xp[:, k:k + l] * conv_w[k]
    xbc_c = jax.nn.silu(conv)
    new_conv = xp[:, l:]
    xs = xbc_c[..., :D_INNER].reshape(bsz, l, SSM_HEADS, SSM_HEAD_DIM)
    bm = xbc_c[..., D_INNER:D_INNER + SSM_GROUPS * D_STATE].reshape(bsz, l, SSM_GROUPS, D_STATE)
    cm = xbc_c[..., D_INNER + SSM_GROUPS * D_STATE:].reshape(bsz, l, SSM_GROUPS, D_STATE)
    dt = jax.nn.softplus(dt_raw.astype(jnp.float32) + dt_bias.astype(jnp.float32))
    a = -jnp.exp(a_log.astype(jnp.float32))
    y, new_ssm = ssd_scan(xs, dt, a, bm, cm, ssm_state)
    y = y + xs.astype(jnp.float32) * d_skip.astype(jnp.float32)[:, None]
    y = y.reshape(bsz, l, D_INNER).astype(h.dtype) * jax.nn.silu(z)
    yg = rms_norm(y.reshape(bsz, l, SSM_GROUPS, D_INNER // SSM_GROUPS), norm_g.reshape(SSM_GROUPS, -1))
    out = yg.reshape(bsz, l, D_INNER) @ w_out
    return out, new_conv, new_ssm.astype(ssm_state.dtype)


def sq_relu_mlp(h, w1, w2):
    return jnp.square(jax.nn.relu(h @ w1)) @ w2


def trunk(x, c, ssm_states, conv_states, ada_w, ada_b, norm1_g, norm2_g,
          gm_w_in, gm_ln_g, gm_ln_b, gm_w_s, gm_b_s, gm_w_out,
          ssm_w_in, ssm_conv_w, ssm_conv_b, ssm_dt_bias, ssm_a_log, ssm_d, ssm_norm_g, ssm_w_out,
          mlp_w1, mlp_w2, final_g):
    new_v, new_conv, new_ssm = [], [], []
    for i in range(DEPTH):
        mod = (jax.nn.silu(c) @ ada_w[i] + ada_b[i]).reshape(c.shape[0], N_MOD, D_MODEL)
        h = modulate(rms_norm(x, norm1_g[i]), mod[:, 0], mod[:, 1])
        j = i // N_MIXERS
        if i % N_MIXERS == 0:
            out, v = chunk_gmlp_mixer(h, gm_w_in[j], gm_ln_g[j], gm_ln_b[j], gm_w_s[j], gm_b_s[j], gm_w_out[j])
            new_v.append(v)
        else:
            out, cs, ss = mamba2_mixer(h, conv_states[j], ssm_states[j], ssm_w_in[j], ssm_conv_w[j], ssm_conv_b[j],
                                       ssm_dt_bias[j], ssm_a_log[j], ssm_d[j], ssm_norm_g[j], ssm_w_out[j])
            new_conv.append(cs)
            new_ssm.append(ss)
        x = x + mod[:, 2][:, None, :] * out
        h = modulate(rms_norm(x, norm2_g[i]), mod[:, 3], mod[:, 4])
        x = x + mod[:, 5][:, None, :] * sq_relu_mlp(h, mlp_w1[i], mlp_w2[i])
    y = rms_norm(x, final_g)
    return y, jnp.stack(new_v), jnp.stack(new_ssm), jnp.stack(new_conv)


def setup_inputs(seed: int = 0) -> dict:
    key = jax.random.key(seed)
    ks = iter(jax.random.split(key, 40))
    f32 = jnp.float32

    def nrm(shape, scale):
        return jax.random.normal(next(ks), shape, f32) * scale

    NG, NS = N_GM_LAYERS, N_SSM_LAYERS
    dt0 = jnp.exp(jax.random.uniform(next(ks), (NS, SSM_HEADS), f32, math.log(1e-3), math.log(1e-1)))
    dt_bias = dt0 + jnp.log(-jnp.expm1(-dt0))
    a_log = jnp.log(jax.random.uniform(next(ks), (NS, SSM_HEADS), f32, 1.0, 16.0))
    return {
        'x_prompt': nrm((BATCH, SEQ, D_MODEL), 1.0),
        'x_sample': nrm((DEC_BATCH, DEC_SEQ, D_MODEL), 1.0),
        'c_prompt': nrm((BATCH, D_MODEL), 1.0),
        'c_sample': nrm((DEC_BATCH, D_MODEL), 1.0),
        'state_ssm': nrm((NS, DEC_BATCH, SSM_HEADS, SSM_HEAD_DIM, D_STATE), 0.1),
        'state_conv': nrm((NS, DEC_BATCH, CONV_W - 1, CONV_DIM), 1.0),
        'ada_w': nrm((DEPTH, D_MODEL, N_MOD * D_MODEL), D_MODEL ** -0.5),
        'ada_b': nrm((DEPTH, N_MOD * D_MODEL), 0.01),
        'norm1_g': 1.0 + nrm((DEPTH, D_MODEL), 0.02),
        'norm2_g': 1.0 + nrm((DEPTH, D_MODEL), 0.02),
        'gm_w_in': nrm((NG, D_MODEL, 2 * D_GM), D_MODEL ** -0.5),
        'gm_ln_g': 1.0 + nrm((NG, D_GM), 0.02),
        'gm_ln_b': nrm((NG, D_GM), 0.02),
        'gm_w_s': nrm((NG, GM_GROUPS, GM_CHUNK, GM_CHUNK), GM_CHUNK ** -0.5),
        'gm_b_s': 1.0 + nrm((NG, GM_GROUPS, GM_CHUNK), 0.1),
        'gm_w_out': nrm((NG, D_GM, D_MODEL), D_GM ** -0.5),
        'ssm_w_in': nrm((NS, D_MODEL, D_IN_PROJ), D_MODEL ** -0.5),
        'ssm_conv_w': nrm((NS, CONV_W, CONV_DIM), CONV_W ** -0.5),
        'ssm_conv_b': nrm((NS, CONV_DIM), 0.02),
        'ssm_dt_bias': dt_bias,
        'ssm_a_log': a_log,
        'ssm_d': 1.0 + nrm((NS, SSM_HEADS), 0.1),
        'ssm_norm_g': 1.0 + nrm((NS, D_INNER), 0.02),
        'ssm_w_out': nrm((NS, D_INNER, D_MODEL), D_INNER ** -0.5),
        'mlp_w1': nrm((DEPTH, D_MODEL, D_FF), D_MODEL ** -0.5),
        'mlp_w2': nrm((DEPTH, D_FF, D_MODEL), D_FF ** -0.5),
        'final_g': 1.0 + nrm((D_MODEL,), 0.02),
    }


def reference(x_prompt, x_sample, c_prompt, c_sample, state_ssm, state_conv, ada_w, ada_b, norm1_g, norm2_g,
              gm_w_in, gm_ln_g, gm_ln_b, gm_w_s, gm_b_s, gm_w_out,
              ssm_w_in, ssm_conv_w, ssm_conv_b, ssm_dt_bias, ssm_a_log, ssm_d, ssm_norm_g, ssm_w_out,
              mlp_w1, mlp_w2, final_g):
    weights = (ada_w, ada_b, norm1_g, norm2_g, gm_w_in, gm_ln_g, gm_ln_b, gm_w_s, gm_b_s, gm_w_out,
               ssm_w_in, ssm_conv_w, ssm_conv_b, ssm_dt_bias, ssm_a_log, ssm_d, ssm_norm_g, ssm_w_out,
               mlp_w1, mlp_w2, final_g)
    bp = x_prompt.shape[0]
    ssm0 = jnp.zeros((N_SSM_LAYERS, bp, SSM_HEADS, SSM_HEAD_DIM, D_STATE), x_prompt.dtype)
    conv0 = jnp.zeros((N_SSM_LAYERS, bp, CONV_W - 1, CONV_DIM), x_prompt.dtype)
    y_prompt, gm_v_prompt, ssm_state_prompt, conv_state_prompt = trunk(x_prompt, c_prompt, ssm0, conv0, *weights)
    y_sample, gm_v_sample, ssm_state_sample, conv_state_sample = trunk(x_sample, c_sample, state_ssm, state_conv, *weights)
    return (y_prompt, y_sample, gm_v_prompt, gm_v_sample, ssm_state_prompt, conv_state_prompt,
            ssm_state_sample, conv_state_sample)
```

```python
import functools

import jax
import jax.numpy as jnp
from jax import lax
from jax.experimental import pallas as pl
from jax.experimental.pallas import tpu as pltpu

F32 = jnp.float32
BF16 = jnp.bfloat16
HIGHEST = lax.Precision.HIGHEST

D_MODEL = 1024
DEPTH = 2
N_MOD = 6
EPS = 1e-6
GM_CHUNK = 128
D_GM = 2 * D_MODEL
GM_GROUPS = 8
GM_GROUP_W = D_GM // GM_GROUPS
D_INNER = 2 * D_MODEL
SSM_HEAD_DIM = 64
SSM_HEADS = D_INNER // SSM_HEAD_DIM
SSM_GROUPS = 8
HEADS_PER_GROUP = SSM_HEADS // SSM_GROUPS
SSM_GROUP_W = D_INNER // SSM_GROUPS
D_STATE = 128
CONV_W = 4
CONV_DIM = D_INNER + 2 * SSM_GROUPS * D_STATE
D_ZX = D_INNER + CONV_DIM
SSM_CHUNK = 128
D_FF = 4 * D_MODEL
FF_CHUNK = 1024
SUBLANES = 8
CONV_PAD = SUBLANES
VMEM_LIMIT = 56 * 1024 * 1024


def _silu(x):
    return x * (1.0 / (1.0 + jnp.exp(-x)))


def _softplus(x):
    return jnp.maximum(x, 0.0) + jnp.log1p(jnp.exp(-jnp.abs(x)))


def _gelu_tanh(x):
    return x * (0.5 * (1.0 + jnp.tanh(0.7978845608028654 * (x + 0.044715 * (x * x * x)))))


def _rms(x, g):
    return x * lax.rsqrt(jnp.mean(x * x, axis=-1, keepdims=True) + EPS) * g


def _norm_modulate(x3, g, shift, scale):
    r, tl, d = x3.shape
    hn = _rms(x3.reshape(r * tl, d), g)
    h3 = hn.reshape(r, tl, d) * (1.0 + scale) + shift
    return h3.reshape(r * tl, d).astype(BF16)


def _const_spec(shape):
    return pl.BlockSpec(shape, lambda *_: (0,) * len(shape), pipeline_mode=pl.Buffered(1))


def _params(*sem):
    return pltpu.CompilerParams(dimension_semantics=sem, vmem_limit_bytes=VMEM_LIMIT)


def _ada_kernel(c_ref, w_ref, b_ref, o_ref):
    sc = _silu(c_ref[...]).astype(BF16)
    o_ref[...] = jnp.dot(sc, w_ref[...].astype(BF16), preferred_element_type=F32) + b_ref[...]


def _ada(c_all, ada_w, ada_b):
    nb = c_all.shape[0]
    out = pl.pallas_call(
        _ada_kernel,
        grid=(DEPTH, N_MOD),
        in_specs=[pl.BlockSpec((nb, D_MODEL), lambda i, k: (0, 0)),
                  pl.BlockSpec((None, D_MODEL, D_MODEL), lambda i, k: (i, 0, k)),
                  pl.BlockSpec((None, None, 1, D_MODEL), lambda i, k: (i, k, 0, 0))],
        out_specs=pl.BlockSpec((None, None, nb, D_MODEL), lambda i, k: (i, k, 0, 0)),
        out_shape=jax.ShapeDtypeStruct((DEPTH, N_MOD, nb, D_MODEL), F32),
        compiler_params=_params("parallel", "parallel"),
        name="ada_mod",
    )(c_all, ada_w, ada_b.reshape(DEPTH, N_MOD, 1, D_MODEL))
    return out.reshape(DEPTH, N_MOD, nb, 1, D_MODEL)


def _mod_spec(layer, half, r, row0):
    return pl.BlockSpec((None, 3, r, 1, D_MODEL), lambda i, j: (layer, half, row0 // r + i, 0, 0))


def _gm_kernel(x_ref, mod_ref, g_ref, win_ref, lng_ref, lnb_ref, wmix_ref, bexp_ref, wout_ref,
               xo_ref, v_ref, *, v_tail):
    r, tl, d = x_ref.shape
    tm = r * tl
    x3 = x_ref[...]
    h = _norm_modulate(x3, g_ref[...], mod_ref[0], mod_ref[1])
    z = _gelu_tanh(jnp.dot(h, win_ref[...], preferred_element_type=F32))
    u = z[:, :D_GM]
    vr = z[:, D_GM:]
    xc = vr - jnp.mean(vr, axis=-1, keepdims=True)
    v = xc * lax.rsqrt(jnp.mean(xc * xc, axis=-1, keepdims=True) + EPS) * lng_ref[...] + lnb_ref[...]
    vb = v.astype(BF16)
    chunks = []
    for c in range(tm // GM_CHUNK):
        vc = vb[c * GM_CHUNK:(c + 1) * GM_CHUNK]
        parts = [jnp.dot(wmix_ref[g], vc[:, g * GM_GROUP_W:(g + 1) * GM_GROUP_W], preferred_element_type=F32)
                 for g in range(GM_GROUPS)]
        chunks.append(jnp.concatenate(parts, axis=1) + bexp_ref[...])
    s = jnp.concatenate(chunks, axis=0) if len(chunks) > 1 else chunks[0]
    y = jnp.dot((u * s).astype(BF16), wout_ref[...], preferred_element_type=F32)
    xo_ref[...] = x3 + mod_ref[2] * y.reshape(r, tl, d)

    @pl.when(pl.program_id(1) == pl.num_programs(1) - 1)
    def _():
        v_ref[...] = v.reshape(r, tl, D_GM)[:, tl - v_tail:, :]


def _gm_layer(x, mod, layer, row0, r, tl, g, w_in, ln_g, ln_b, w_s, b_s, w_out):
    nb, l, d = x.shape
    q = min(l, GM_CHUNK)
    assert l % q == 0 and GM_CHUNK % q == 0 and (r * tl) % GM_CHUNK == 0 and tl % q == 0
    rep = GM_CHUNK // q
    tri = jnp.tril(jnp.ones((q, q), F32))
    wq = w_s[:, :q, :q] * tri
    wmix = jnp.einsum("ab,gts->gatbs", jnp.eye(rep, dtype=F32), wq).reshape(GM_GROUPS, GM_CHUNK, GM_CHUNK)
    bq = jnp.tile(b_s[:, :q].T, (rep, 1))
    bexp = jnp.repeat(bq, GM_GROUP_W, axis=1)
    v_tail = l - ((l - 1) // GM_CHUNK) * GM_CHUNK
    assert v_tail <= tl
    kern = functools.partial(_gm_kernel, v_tail=v_tail)
    xo, v = pl.pallas_call(
        kern,
        grid=(nb // r, l // tl),
        in_specs=[pl.BlockSpec((r, tl, d), lambda i, j: (i, j, 0)),
                  _mod_spec(layer, 0, r, row0),
                  _const_spec((1, d)),
                  _const_spec((d, 2 * D_GM)),
                  _const_spec((1, D_GM)),
                  _const_spec((1, D_GM)),
                  _const_spec((GM_GROUPS, GM_CHUNK, GM_CHUNK)),
                  _const_spec((GM_CHUNK, D_GM)),
                  _const_spec((D_GM, d))],
        out_specs=[pl.BlockSpec((r, tl, d), lambda i, j: (i, j, 0)),
                   pl.BlockSpec((r, v_tail, D_GM), lambda i, j: (i, 0, 0))],
        out_shape=[jax.ShapeDtypeStruct((nb, l, d), F32),
                   jax.ShapeDtypeStruct((nb, v_tail, D_GM), F32)],
        compiler_params=_params("parallel", "arbitrary"),
        name="gmlp_mixer",
    )(x, mod, g.reshape(1, d), w_in.astype(BF16), ln_g.reshape(1, D_GM), ln_b.reshape(1, D_GM),
      wmix.astype(BF16), bexp, w_out.astype(BF16))
    return xo, v


def _mlp_kernel(x_ref, mod_ref, g_ref, w1_ref, w2_ref, gf_ref, o_ref, *, final):
    r, tl, d = x_ref.shape
    x3 = x_ref[...]
    h = _norm_modulate(x3, g_ref[...], mod_ref[0], mod_ref[1])
    acc = jnp.zeros((r * tl, d), F32)
    for k in range(D_FF // FF_CHUNK):
        a = jnp.dot(h, w1_ref[:, k * FF_CHUNK:(k + 1) * FF_CHUNK], preferred_element_type=F32)
        a = jnp.square(jnp.maximum(a, 0.0)).astype(BF16)
        acc = acc + jnp.dot(a, w2_ref[k * FF_CHUNK:(k + 1) * FF_CHUNK, :], preferred_element_type=F32)
    xo = x3 + mod_ref[2] * acc.reshape(r, tl, d)
    if final:
        xo = _rms(xo.reshape(r * tl, d), gf_ref[...]).reshape(r, tl, d)
    o_ref[...] = xo


def _mlp_layer(x, mod, layer, row0, r, tl, g, w1, w2, final_g, final):
    nb, l, d = x.shape
    kern = functools.partial(_mlp_kernel, final=final)
    return pl.pallas_call(
        kern,
        grid=(nb // r, l // tl),
        in_specs=[pl.BlockSpec((r, tl, d), lambda i, j: (i, j, 0)),
                  _mod_spec(layer, 1, r, row0),
                  _const_spec((1, d)),
                  _const_spec((d, D_FF)),
                  _const_spec((D_FF, d)),
                  _const_spec((1, d))],
        out_specs=pl.BlockSpec((r, tl, d), lambda i, j: (i, j, 0)),
        out_shape=jax.ShapeDtypeStruct((nb, l, d), F32),
        compiler_params=_params("parallel", "parallel"),
        name="relu2_mlp",
    )(x, mod, g.reshape(1, d), w1.astype(BF16), w2.astype(BF16), final_g.reshape(1, d))


def _ssm_in_kernel(x_ref, mod_ref, g_ref, wzx_ref, wdt_ref, cw_ref, cb_ref, dtb_ref, cs_ref,
                   z_ref, xbc_ref, dt_ref, cnew_ref, xp_ref):
    r, tl, d = x_ref.shape
    tm = r * tl

    @pl.when(pl.program_id(1) == 0)
    def _():
        xp_ref[:, 0:CONV_PAD, :] = cs_ref[...]

    h = _norm_modulate(x_ref[...], g_ref[...], mod_ref[0], mod_ref[1])
    z_ref[...] = jnp.dot(h, wzx_ref[:, :D_INNER], preferred_element_type=F32).reshape(r, tl, D_INNER)
    xbc = jnp.dot(h, wzx_ref[:, D_INNER:], preferred_element_type=F32)
    dt_raw = jnp.dot(h, wdt_ref[...], preferred_element_type=F32)
    dt_ref[...] = _softplus(dt_raw + dtb_ref[...]).reshape(r, tl, SSM_HEADS)
    xp_ref[:, CONV_PAD:CONV_PAD + tl, :] = xbc.reshape(r, tl, CONV_DIM)
    conv = cb_ref[...].reshape(1, 1, CONV_DIM)
    for k in range(CONV_W):
        off = CONV_PAD - (CONV_W - 1) + k
        conv = conv + xp_ref[:, off:off + tl, :] * cw_ref[k:k + 1, :].reshape(1, 1, CONV_DIM)
    xbc_ref[...] = _silu(conv)
    tail = xp_ref[:, tl:tl + CONV_PAD, :]
    cnew_ref[...] = tail
    xp_ref[:, 0:CONV_PAD, :] = tail


def _ssm_in(x, mod, layer, row0, r, tl, g, w_in, conv_w, conv_b, dt_bias, conv_state8):
    nb, l, d = x.shape
    w_bf = w_in.astype(BF16)
    return pl.pallas_call(
        _ssm_in_kernel,
        grid=(nb // r, l // tl),
        in_specs=[pl.BlockSpec((r, tl, d), lambda i, j: (i, j, 0)),
                  _mod_spec(layer, 0, r, row0),
                  _const_spec((1, d)),
                  _const_spec((d, D_ZX)),
                  _const_spec((d, SSM_HEADS)),
                  _const_spec((CONV_W, CONV_DIM)),
                  _const_spec((1, CONV_DIM)),
                  _const_spec((1, SSM_HEADS)),
                  pl.BlockSpec((r, CONV_PAD, CONV_DIM), lambda i, j: (i, 0, 0))],
        out_specs=[pl.BlockSpec((r, tl, D_INNER), lambda i, j: (i, j, 0)),
                   pl.BlockSpec((r, tl, CONV_DIM), lambda i, j: (i, j, 0)),
                   pl.BlockSpec((r, tl, SSM_HEADS), lambda i, j: (i, j, 0)),
                   pl.BlockSpec((r, CONV_PAD, CONV_DIM), lambda i, j: (i, 0, 0))],
        out_shape=[jax.ShapeDtypeStruct((nb, l, D_INNER), F32),
                   jax.ShapeDtypeStruct((nb, l, CONV_DIM), F32),
                   jax.ShapeDtypeStruct((nb, l, SSM_HEADS), F32),
                   jax.ShapeDtypeStruct((nb, CONV_PAD, CONV_DIM), F32)],
        scratch_shapes=[pltpu.VMEM((r, CONV_PAD + tl, CONV_DIM), F32)],
        compiler_params=_params("parallel", "arbitrary"),
        name="ssm_in_conv",
    )(x, mod, g.reshape(1, d), w_bf[:, :D_ZX], w_bf[:, D_ZX:], conv_w, conv_b.reshape(1, CONV_DIM),
      dt_bias.reshape(1, SSM_HEADS), conv_state8)


def _ssd_kernel(xbc_ref, dt_ref, alog_ref, dexp_ref, lcum_ref, bd_ref, e_ref, et_ref, s0_ref,
                y_ref, s_ref):
    r, tl, _ = xbc_ref.shape
    tm = r * tl

    @pl.when(pl.program_id(1) == 0)
    def _():
        s_ref[...] = s0_ref[...]

    xbc = xbc_ref[...].reshape(tm, CONV_DIM)
    xs = xbc[:, :D_INNER]
    bm = xbc[:, D_INNER:D_INNER + SSM_GROUPS * D_STATE]
    cm = xbc[:, D_INNER + SSM_GROUPS * D_STATE:]
    dt = dt_ref[...].reshape(tm, SSM_HEADS)
    da = dt * (-jnp.exp(alog_ref[...]))
    lcum = lcum_ref[...]
    causal = lcum > 0.5
    acum = jnp.dot(lcum, da, precision=HIGHEST, preferred_element_type=F32)
    alast = jnp.dot(bd_ref[...], da, precision=HIGHEST, preferred_element_type=F32)
    acum_t = acum.T
    dec_end_t = jnp.exp(alast.T)
    e = e_ref[...]
    dt_x = jnp.dot(dt, e, precision=HIGHEST, preferred_element_type=F32)
    ea_x = jnp.dot(jnp.exp(acum), e, precision=HIGHEST, preferred_element_type=F32)
    w_x = jnp.dot(dt * jnp.exp(alast - acum), e, precision=HIGHEST, preferred_element_type=F32)
    xdt = xs * dt_x
    xw = xs * w_x
    lane = lax.broadcasted_iota(jnp.int32, (1, SSM_GROUP_W), 1) // SSM_HEAD_DIM
    row = lax.broadcasted_iota(jnp.int32, (tm, 1), 0) // tl
    nt = (((1,), (1,)), ((), ()))
    tn = (((0,), (0,)), ((), ()))
    s_decay = []
    for b in range(r):
        col = jnp.broadcast_to(dec_end_t[:, b * tl:b * tl + 1], (SSM_HEADS, D_STATE))
        s_decay.append(jnp.dot(et_ref[...], col, precision=HIGHEST, preferred_element_type=F32))
    for g in range(SSM_GROUPS):
        ch = slice(g * SSM_GROUP_W, (g + 1) * SSM_GROUP_W)
        bg = bm[:, g * D_STATE:(g + 1) * D_STATE].astype(BF16)
        cg32 = cm[:, g * D_STATE:(g + 1) * D_STATE]
        cg = cg32.astype(BF16)
        cb = lax.dot_general(cg, bg, nt, preferred_element_type=F32)
        xdt_g = xdt[:, ch]
        y_g = xs[:, ch] * dexp_ref[:, ch]
        for hr in range(HEADS_PER_GROUP):
            hh = g * HEADS_PER_GROUP + hr
            seg = acum[:, hh:hh + 1] - acum_t[hh:hh + 1, :]
            lmat = (cb * jnp.exp(jnp.where(causal, seg, -jnp.inf))).astype(BF16)
            xh = jnp.where(lane == hr, xdt_g, 0.0).astype(BF16)
            y_g = y_g + jnp.dot(lmat, xh, preferred_element_type=F32)
        y_off = jnp.zeros((tm, SSM_GROUP_W), F32)
        xw_g = xw[:, ch]
        for b in range(r):
            s_old = s_ref[b, ch, :]
            if r > 1:
                c_b = jnp.where(row == b, cg32, 0.0).astype(BF16)
                xw_b = jnp.where(row == b, xw_g, 0.0).astype(BF16)
            else:
                c_b = cg
                xw_b = xw_g.astype(BF16)
            y_off = y_off + lax.dot_general(c_b, s_old.astype(BF16), nt, preferred_element_type=F32)
            upd = lax.dot_general(xw_b, bg, tn, preferred_element_type=F32)
            s_ref[b, ch, :] = s_old * s_decay[b][ch, :] + upd
        y_g = y_g + y_off * ea_x[:, ch]
        y_ref[:, :, ch] = y_g.reshape(r, tl, SSM_GROUP_W)


def _ssd(xbc, dt, a_log, d_skip, state0, r, tl):
    nb, l, _ = xbc.shape
    tm = r * tl
    eye_r = jnp.eye(r, dtype=F32)
    lcum = jnp.kron(eye_r, jnp.tril(jnp.ones((tl, tl), F32)))
    bd = jnp.kron(eye_r, jnp.ones((tl, tl), F32))
    e = jnp.repeat(jnp.eye(SSM_HEADS, dtype=F32), SSM_HEAD_DIM, axis=1)
    dexp = jnp.repeat(d_skip, SSM_HEAD_DIM).reshape(1, D_INNER)
    return pl.pallas_call(
        _ssd_kernel,
        grid=(nb // r, l // tl),
        in_specs=[pl.BlockSpec((r, tl, CONV_DIM), lambda i, j: (i, j, 0)),
                  pl.BlockSpec((r, tl, SSM_HEADS), lambda i, j: (i, j, 0)),
                  _const_spec((1, SSM_HEADS)),
                  _const_spec((1, D_INNER)),
                  _const_spec((tm, tm)),
                  _const_spec((tm, tm)),
                  _const_spec((SSM_HEADS, D_INNER)),
                  _const_spec((D_INNER, SSM_HEADS)),
                  pl.BlockSpec((r, D_INNER, D_STATE), lambda i, j: (i, 0, 0))],
        out_specs=[pl.BlockSpec((r, tl, D_INNER), lambda i, j: (i, j, 0)),
                   pl.BlockSpec((r, D_INNER, D_STATE), lambda i, j: (i, 0, 0))],
        out_shape=[jax.ShapeDtypeStruct((nb, l, D_INNER), F32),
                   jax.ShapeDtypeStruct((nb, D_INNER, D_STATE), F32)],
        compiler_params=_params("parallel", "arbitrary"),
        name="ssd_scan",
    )(xbc, dt, a_log.reshape(1, SSM_HEADS), dexp, lcum, bd, e, e.T, state0)


def _ssm_out_kernel(x_ref, y_ref, z_ref, mod_ref, ng_ref, wout_ref, o_ref):
    r, tl, d = x_ref.shape
    tm = r * tl
    yz = y_ref[...].reshape(tm, D_INNER) * _silu(z_ref[...].reshape(tm, D_INNER))
    parts = []
    for g in range(SSM_GROUPS):
        ch = slice(g * SSM_GROUP_W, (g + 1) * SSM_GROUP_W)
        parts.append(_rms(yz[:, ch], ng_ref[:, ch]).astype(BF16))
    yn = jnp.concatenate(parts, axis=1)
    out = jnp.dot(yn, wout_ref[...], preferred_element_type=F32)
    o_ref[...] = x_ref[...] + mod_ref[2] * out.reshape(r, tl, d)


def _ssm_out(x, y, z, mod, layer, row0, r, tl, norm_g, w_out):
    nb, l, d = x.shape
    return pl.pallas_call(
        _ssm_out_kernel,
        grid=(nb // r, l // tl),
        in_specs=[pl.BlockSpec((r, tl, d), lambda i, j: (i, j, 0)),
                  pl.BlockSpec((r, tl, D_INNER), lambda i, j: (i, j, 0)),
                  pl.BlockSpec((r, tl, D_INNER), lambda i, j: (i, j, 0)),
                  _mod_spec(layer, 0, r, row0),
                  _const_spec((1, D_INNER)),
                  _const_spec((D_INNER, d))],
        out_specs=pl.BlockSpec((r, tl, d), lambda i, j: (i, j, 0)),
        out_shape=jax.ShapeDtypeStruct((nb, l, d), F32),
        compiler_params=_params("parallel", "parallel"),
        name="ssm_gate_out",
    )(x, y, z, mod, norm_g.reshape(1, D_INNER), w_out.astype(BF16))


def _trunk(x, mod, row0, tiles, ssm_state, conv_state, p):
    nb, l, _ = x.shape
    new_v, new_ssm, new_conv = [], [], []
    for i in range(DEPTH):
        j = i // 2
        if i % 2 == 0:
            x, v = _gm_layer(x, mod, i, row0, *tiles["gm"], p["norm1_g"][i], p["gm_w_in"][j], p["gm_ln_g"][j],
                             p["gm_ln_b"][j], p["gm_w_s"][j], p["gm_b_s"][j], p["gm_w_out"][j])
            new_v.append(v)
        else:
            cs8 = jnp.pad(conv_state[j], ((0, 0), (CONV_PAD - (CONV_W - 1), 0), (0, 0)))
            z, xbc, dt, cnew = _ssm_in(x, mod, i, row0, *tiles["ssm_in"], p["norm1_g"][i], p["ssm_w_in"][j],
                                       p["ssm_conv_w"][j], p["ssm_conv_b"][j], p["ssm_dt_bias"][j], cs8)
            s0 = ssm_state[j].reshape(nb, D_INNER, D_STATE)
            y, s_new = _ssd(xbc, dt, p["ssm_a_log"][j], p["ssm_d"][j], s0, *tiles["ssd"])
            x = _ssm_out(x, y, z, mod, i, row0, *tiles["ssm_out"], p["ssm_norm_g"][j], p["ssm_w_out"][j])
            new_conv.append(cnew[:, CONV_PAD - (CONV_W - 1):, :])
            new_ssm.append(s_new.reshape(nb, SSM_HEADS, SSM_HEAD_DIM, D_STATE))
        x = _mlp_layer(x, mod, i, row0, *tiles["mlp"], p["norm2_g"][i], p["mlp_w1"][i], p["mlp_w2"][i],
                       p["final_g"], final=(i == DEPTH - 1))
    return x, jnp.stack(new_v), jnp.stack(new_ssm), jnp.stack(new_conv)


PROMPT_TILES = {"gm": (1, 256), "mlp": (1, 512), "ssm_in": (1, 256), "ssd": (1, SSM_CHUNK), "ssm_out": (1, 512)}
SAMPLE_TILES = {"gm": (32, 8), "mlp": (64, 8), "ssm_in": (32, 8), "ssd": (8, 8), "ssm_out": (64, 8)}


def kernel(x_prompt, x_sample, c_prompt, c_sample, state_ssm, state_conv, ada_w, ada_b, norm1_g, norm2_g, gm_w_in, gm_ln_g, gm_ln_b, gm_w_s, gm_b_s, gm_w_out, ssm_w_in, ssm_conv_w, ssm_conv_b, ssm_dt_bias, ssm_a_log, ssm_d, ssm_norm_g, ssm_w_out, mlp_w1, mlp_w2, final_g):
    p = dict(norm1_g=norm1_g, norm2_g=norm2_g, gm_w_in=gm_w_in, gm_ln_g=gm_ln_g, gm_ln_b=gm_ln_b, gm_w_s=gm_w_s,
             gm_b_s=gm_b_s, gm_w_out=gm_w_out, ssm_w_in=ssm_w_in, ssm_conv_w=ssm_conv_w, ssm_conv_b=ssm_conv_b,
             ssm_dt_bias=ssm_dt_bias, ssm_a_log=ssm_a_log, ssm_d=ssm_d, ssm_norm_g=ssm_norm_g, ssm_w_out=ssm_w_out,
             mlp_w1=mlp_w1, mlp_w2=mlp_w2, final_g=final_g)
    n_sample = x_sample.shape[0]
    n_prompt = x_prompt.shape[0]
    n_ssm = state_ssm.shape[0]
    mod = _ada(jnp.concatenate([c_sample, c_prompt], axis=0), ada_w, ada_b)
    ssm0 = jnp.zeros((n_ssm, n_prompt, SSM_HEADS, SSM_HEAD_DIM, D_STATE), F32)
    conv0 = jnp.zeros((n_ssm, n_prompt, CONV_W - 1, CONV_DIM), F32)
    y_p, v_p, ssm_p, conv_p = _trunk(x_prompt, mod, n_sample, PROMPT_TILES, ssm0, conv0, p)
    y_s, v_s, ssm_s, conv_s = _trunk(x_sample, mod, 0, SAMPLE_TILES, state_ssm, state_conv, p)
    return (y_p, y_s, v_p, v_s, ssm_p, conv_p, ssm_s, conv_s)
```

```python
import functools

import jax
import jax.numpy as jnp
from jax import lax
from jax.experimental import pallas as pl
from jax.experimental.pallas import tpu as pltpu

F32 = jnp.float32
BF16 = jnp.bfloat16
HIGHEST = lax.Precision.HIGHEST

D_MODEL = 1024
DEPTH = 2
N_MOD = 6
EPS = 1e-6
GM_CHUNK = 128
D_GM = 2 * D_MODEL
GM_GROUPS = 8
GM_GROUP_W = D_GM // GM_GROUPS
D_INNER = 2 * D_MODEL
SSM_HEAD_DIM = 64
SSM_HEADS = D_INNER // SSM_HEAD_DIM
SSM_GROUPS = 8
HEADS_PER_GROUP = SSM_HEADS // SSM_GROUPS
SSM_GROUP_W = D_INNER // SSM_GROUPS
D_STATE = 128
CONV_W = 4
CONV_DIM = D_INNER + 2 * SSM_GROUPS * D_STATE
D_ZX = D_INNER + CONV_DIM
SSM_CHUNK = 128
D_FF = 4 * D_MODEL
FF_CHUNK = 1024
SUBLANES = 8
CONV_PAD = SUBLANES
VMEM_LIMIT = 56 * 1024 * 1024


def _silu(x):
    return x * (1.0 / (1.0 + jnp.exp(-x)))


def _softplus(x):
    return jnp.maximum(x, 0.0) + jnp.log1p(jnp.exp(-jnp.abs(x)))


def _gelu_tanh(x):
    return x * (0.5 * (1.0 + jnp.tanh(0.7978845608028654 * (x + 0.044715 * (x * x * x)))))


def _rms(x, g):
    return x * lax.rsqrt(jnp.mean(x * x, axis=-1, keepdims=True) + EPS) * g


def _norm_modulate(x3, g, shift, scale):
    r, tl, d = x3.shape
    hn = _rms(x3.reshape(r * tl, d), g)
    h3 = hn.reshape(r, tl, d) * (1.0 + scale) + shift
    return h3.reshape(r * tl, d).astype(BF16)


def _const_spec(shape):
    return pl.BlockSpec(shape, lambda *_: (0,) * len(shape), pipeline_mode=pl.Buffered(1))


def _params(*sem):
    return pltpu.CompilerParams(dimension_semantics=sem, vmem_limit_bytes=VMEM_LIMIT)


def _ada_kernel(c_ref, w_ref, b_ref, o_ref):
    sc = _silu(c_ref[...]).astype(BF16)
    o_ref[...] = jnp.dot(sc, w_ref[...].astype(BF16), preferred_element_type=F32) + b_ref[...]


def _ada(c_all, ada_w, ada_b):
    nb = c_all.shape[0]
    out = pl.pallas_call(
        _ada_kernel,
        grid=(DEPTH, N_MOD),
        in_specs=[pl.BlockSpec((nb, D_MODEL), lambda i, k: (0, 0)),
                  pl.BlockSpec((None, D_MODEL, D_MODEL), lambda i, k: (i, 0, k)),
                  pl.BlockSpec((None, None, 1, D_MODEL), lambda i, k: (i, k, 0, 0))],
        out_specs=pl.BlockSpec((None, None, nb, D_MODEL), lambda i, k: (i, k, 0, 0)),
        out_shape=jax.ShapeDtypeStruct((DEPTH, N_MOD, nb, D_MODEL), F32),
        compiler_params=_params("parallel", "parallel"),
        name="ada_mod",
    )(c_all, ada_w, ada_b.reshape(DEPTH, N_MOD, 1, D_MODEL))
    return out.reshape(DEPTH, N_MOD, nb, 1, D_MODEL)


def _mod_spec(layer, half, r, row0):
    return pl.BlockSpec((None, 3, r, 1, D_MODEL), lambda i, j: (layer, half, row0 // r + i, 0, 0))


def _gm_kernel(x_ref, mod_ref, g_ref, win_ref, lng_ref, lnb_ref, wmix_ref, bexp_ref, wout_ref,
               xo_ref, v_ref, *, v_tail):
    r, tl, d = x_ref.shape
    tm = r * tl
    x3 = x_ref[...]
    h = _norm_modulate(x3, g_ref[...], mod_ref[0], mod_ref[1])
    z = _gelu_tanh(jnp.dot(h, win_ref[...], preferred_element_type=F32))
    u = z[:, :D_GM]
    vr = z[:, D_GM:]
    xc = vr - jnp.mean(vr, axis=-1, keepdims=True)
    v = xc * lax.rsqrt(jnp.mean(xc * xc, axis=-1, keepdims=True) + EPS) * lng_ref[...] + lnb_ref[...]
    vb = v.astype(BF16)
    chunks = []
    for c in range(tm // GM_CHUNK):
        vc = vb[c * GM_CHUNK:(c + 1) * GM_CHUNK]
        parts = [jnp.dot(wmix_ref[g], vc[:, g * GM_GROUP_W:(g + 1) * GM_GROUP_W], preferred_element_type=F32)
                 for g in range(GM_GROUPS)]
        chunks.append(jnp.concatenate(parts, axis=1) + bexp_ref[...])
    s = jnp.concatenate(chunks, axis=0) if len(chunks) > 1 else chunks[0]
    y = jnp.dot((u * s).astype(BF16), wout_ref[...], preferred_element_type=F32)
    xo_ref[...] = x3 + mod_ref[2] * y.reshape(r, tl, d)

    @pl.when(pl.program_id(1) == pl.num_programs(1) - 1)
    def _():
        v_ref[...] = v.reshape(r, tl, D_GM)[:, tl - v_tail:, :]


def _gm_layer(x, mod, layer, row0, r, tl, g, w_in, ln_g, ln_b, w_s, b_s, w_out):
    nb, l, d = x.shape
    q = min(l, GM_CHUNK)
    assert l % q == 0 and GM_CHUNK % q == 0 and (r * tl) % GM_CHUNK == 0 and tl % q == 0
    rep = GM_CHUNK // q
    tri = jnp.tril(jnp.ones((q, q), F32))
    wq = w_s[:, :q, :q] * tri
    wmix = jnp.einsum("ab,gts->gatbs", jnp.eye(rep, dtype=F32), wq).reshape(GM_GROUPS, GM_CHUNK, GM_CHUNK)
    bq = jnp.tile(b_s[:, :q].T, (rep, 1))
    bexp = jnp.repeat(bq, GM_GROUP_W, axis=1)
    v_tail = l - ((l - 1) // GM_CHUNK) * GM_CHUNK
    assert v_tail <= tl
    kern = functools.partial(_gm_kernel, v_tail=v_tail)
    xo, v = pl.pallas_call(
        kern,
        grid=(nb // r, l // tl),
        in_specs=[pl.BlockSpec((r, tl, d), lambda i, j: (i, j, 0)),
                  _mod_spec(layer, 0, r, row0),
                  _const_spec((1, d)),
                  _const_spec((d, 2 * D_GM)),
                  _const_spec((1, D_GM)),
                  _const_spec((1, D_GM)),
                  _const_spec((GM_GROUPS, GM_CHUNK, GM_CHUNK)),
                  _const_spec((GM_CHUNK, D_GM)),
                  _const_spec((D_GM, d))],
        out_specs=[pl.BlockSpec((r, tl, d), lambda i, j: (i, j, 0)),
                   pl.BlockSpec((r, v_tail, D_GM), lambda i, j: (i, 0, 0))],
        out_shape=[jax.ShapeDtypeStruct((nb, l, d), F32),
                   jax.ShapeDtypeStruct((nb, v_tail, D_GM), F32)],
        compiler_params=_params("parallel", "arbitrary"),
        name="gmlp_mixer",
    )(x, mod, g.reshape(1, d), w_in.astype(BF16), ln_g.reshape(1, D_GM), ln_b.reshape(1, D_GM),
      wmix.astype(BF16), bexp, w_out.astype(BF16))
    return xo, v


def _mlp_kernel(x_ref, mod_ref, g_ref, w1_ref, w2_ref, gf_ref, o_ref, *, final):
    r, tl, d = x_ref.shape
    x3 = x_ref[...]
    h = _norm_modulate(x3, g_ref[...], mod_ref[0], mod_ref[1])
    acc = jnp.zeros((r * tl, d), F32)
    for k in range(D_FF // FF_CHUNK):
        a = jnp.dot(h, w1_ref[:, k * FF_CHUNK:(k + 1) * FF_CHUNK], preferred_element_type=F32)
        a = jnp.square(jnp.maximum(a, 0.0)).astype(BF16)
        acc = acc + jnp.dot(a, w2_ref[k * FF_CHUNK:(k + 1) * FF_CHUNK, :], preferred_element_type=F32)
    xo = x3 + mod_ref[2] * acc.reshape(r, tl, d)
    if final:
        xo = _rms(xo.reshape(r * tl, d), gf_ref[...]).reshape(r, tl, d)
    o_ref[...] = xo


def _mlp_layer(x, mod, layer, row0, r, tl, g, w1, w2, final_g, final):
    nb, l, d = x.shape
    kern = functools.partial(_mlp_kernel, final=final)
    return pl.pallas_call(
        kern,
        grid=(nb // r, l // tl),
        in_specs=[pl.BlockSpec((r, tl, d), lambda i, j: (i, j, 0)),
                  _mod_spec(layer, 1, r, row0),
                  _const_spec((1, d)),
                  _const_spec((d, D_FF)),
                  _const_spec((D_FF, d)),
                  _const_spec((1, d))],
        out_specs=pl.BlockSpec((r, tl, d), lambda i, j: (i, j, 0)),
        out_shape=jax.ShapeDtypeStruct((nb, l, d), F32),
        compiler_params=_params("parallel", "parallel"),
        name="relu2_mlp",
    )(x, mod, g.reshape(1, d), w1.astype(BF16), w2.astype(BF16), final_g.reshape(1, d))


def _ssm_in_kernel(x_ref, mod_ref, g_ref, wzx_ref, wdt_ref, cw_ref, cb_ref, dtb_ref, cs_ref,
                   z_ref, xbc_ref, dt_ref, cnew_ref, xp_ref):
    r, tl, d = x_ref.shape
    tm = r * tl

    @pl.when(pl.program_id(1) == 0)
    def _():
        xp_ref[:, 0:CONV_PAD, :] = cs_ref[...]

    h = _norm_modulate(x_ref[...], g_ref[...], mod_ref[0], mod_ref[1])
    z_ref[...] = jnp.dot(h, wzx_ref[:, :D_INNER], preferred_element_type=F32).reshape(r, tl, D_INNER)
    xbc = jnp.dot(h, wzx_ref[:, D_INNER:], preferred_element_type=F32)
    dt_raw = jnp.dot(h, wdt_ref[...], preferred_element_type=F32)
    dt_ref[...] = _softplus(dt_raw + dtb_ref[...]).reshape(r, tl, SSM_HEADS)
    xp_ref[:, CONV_PAD:CONV_PAD + tl, :] = xbc.reshape(r, tl, CONV_DIM)
    conv = cb_ref[...].reshape(1, 1, CONV_DIM)
    for k in range(CONV_W):
        off = CONV_PAD - (CONV_W - 1) + k
        conv = conv + xp_ref[:, off:off + tl, :] * cw_ref[k:k + 1, :].reshape(1, 1, CONV_DIM)
    xbc_ref[...] = _silu(conv)
    tail = xp_ref[:, tl:tl + CONV_PAD, :]
    cnew_ref[...] = tail
    xp_ref[:, 0:CONV_PAD, :] = tail


def _ssm_in(x, mod, layer, row0, r, tl, g, w_in, conv_w, conv_b, dt_bias, conv_state8):
    nb, l, d = x.shape
    w_bf = w_in.astype(BF16)
    return pl.pallas_call(
        _ssm_in_kernel,
        grid=(nb // r, l // tl),
        in_specs=[pl.BlockSpec((r, tl, d), lambda i, j: (i, j, 0)),
                  _mod_spec(layer, 0, r, row0),
                  _const_spec((1, d)),
                  _const_spec((d, D_ZX)),
                  _const_spec((d, SSM_HEADS)),
                  _const_spec((CONV_W, CONV_DIM)),
                  _const_spec((1, CONV_DIM)),
                  _const_spec((1, SSM_HEADS)),
                  pl.BlockSpec((r, CONV_PAD, CONV_DIM), lambda i, j: (i, 0, 0))],
        out_specs=[pl.BlockSpec((r, tl, D_INNER), lambda i, j: (i, j, 0)),
                   pl.BlockSpec((r, tl, CONV_DIM), lambda i, j: (i, j, 0)),
                   pl.BlockSpec((r, tl, SSM_HEADS), lambda i, j: (i, j, 0)),
                   pl.BlockSpec((r, CONV_PAD, CONV_DIM), lambda i, j: (i, 0, 0))],
        out_shape=[jax.ShapeDtypeStruct((nb, l, D_INNER), F32),
                   jax.ShapeDtypeStruct((nb, l, CONV_DIM), F32),
                   jax.ShapeDtypeStruct((nb, l, SSM_HEADS), F32),
                   jax.ShapeDtypeStruct((nb, CONV_PAD, CONV_DIM), F32)],
        scratch_shapes=[pltpu.VMEM((r, CONV_PAD + tl, CONV_DIM), F32)],
        compiler_params=_params("parallel", "arbitrary"),
        name="ssm_in_conv",
    )(x, mod, g.reshape(1, d), w_bf[:, :D_ZX], w_bf[:, D_ZX:], conv_w, conv_b.reshape(1, CONV_DIM),
      dt_bias.reshape(1, SSM_HEADS), conv_state8)


NT_DIMS = (((1,), (1,)), ((), ()))
TN_DIMS = (((0,), (0,)), ((), ()))


def _expand_heads(vals, e_bf):
    tm = vals[0].shape[0]
    v = jnp.concatenate(vals, axis=0)
    hi = v.astype(BF16)
    lo = (v - hi.astype(F32)).astype(BF16)
    x = jnp.dot(hi, e_bf, preferred_element_type=F32) + jnp.dot(lo, e_bf, preferred_element_type=F32)
    return [x[k * tm:(k + 1) * tm] for k in range(len(vals))]


def _head_masks():
    lane = lax.broadcasted_iota(jnp.int32, (1, SSM_GROUP_W), 1) // SSM_HEAD_DIM
    return [jnp.where(lane == hr, 1.0, 0.0).astype(BF16) for hr in range(HEADS_PER_GROUP)]


def _ssd_intra(g, cbm, acum, acum_t, xdt_g, hmask):
    acc = None
    for hr in range(HEADS_PER_GROUP):
        hh = g * HEADS_PER_GROUP + hr
        seg = jnp.minimum(acum[:, hh:hh + 1] - acum_t[hh:hh + 1, :], 0.0)
        lmat = (cbm * jnp.exp(seg)).astype(BF16)
        part = jnp.dot(lmat, xdt_g * hmask[hr], preferred_element_type=F32)
        acc = part if acc is None else acc + part
    return acc


def _ssd_prompt_kernel(xbc_ref, dt_ref, alog_ref, dexp_ref, lcum_ref, e_ref, y_ref, s_ref, st_ref):
    _, tl, _ = xbc_ref.shape

    @pl.when(pl.program_id(1) == 0)
    def _():
        st_ref[...] = jnp.zeros_like(st_ref)

    xs = xbc_ref[0, :, :D_INNER]
    dt = dt_ref[0]
    da = dt * (-jnp.exp(alog_ref[...]))
    lcum = lcum_ref[...]
    causal = lcum > 0.5
    acum = jnp.dot(lcum, da, precision=HIGHEST, preferred_element_type=F32)
    acum_t = acum.T
    alast = acum[tl - 1:tl, :]
    dt_x, ea_x, w_x = _expand_heads([dt, jnp.exp(acum), dt * jnp.exp(alast - acum)], e_ref[...])
    dec_x = ea_x[tl - 1:tl, :]
    xdt = (xs * dt_x).astype(BF16)
    xw = (xs * w_x).astype(BF16)
    hmask = _head_masks()
    for g in range(SSM_GROUPS):
        ch = slice(g * SSM_GROUP_W, (g + 1) * SSM_GROUP_W)
        bgt = xbc_ref[0, :, D_INNER + g * D_STATE:D_INNER + (g + 1) * D_STATE].T.astype(BF16)
        cg = xbc_ref[0, :, D_INNER + (SSM_GROUPS + g) * D_STATE:D_INNER + (SSM_GROUPS + g + 1) * D_STATE].astype(BF16)
        cbm = jnp.where(causal, jnp.dot(cg, bgt, preferred_element_type=F32), 0.0)
        y_g = xs[:, ch] * dexp_ref[:, ch] + _ssd_intra(g, cbm, acum, acum_t, xdt[:, ch], hmask)
        s_old = st_ref[:, ch]
        y_g = y_g + jnp.dot(cg, s_old.astype(BF16), preferred_element_type=F32) * ea_x[:, ch]
        st_ref[:, ch] = s_old * dec_x[:, ch] + jnp.dot(bgt, xw[:, ch], preferred_element_type=F32)
        y_ref[0, :, ch] = y_g

    @pl.when(pl.program_id(1) == pl.num_programs(1) - 1)
    def _():
        s_ref[0] = st_ref[...].T


def _ssd_sample_kernel(xbc_ref, dt_ref, alog_ref, dexp_ref, lcum_ref, bd_ref, e_ref, s0_ref, y_ref, s_ref):
    r, tl, _ = xbc_ref.shape
    tm = r * tl
    xbc = xbc_ref[...].reshape(tm, CONV_DIM)
    xs = xbc[:, :D_INNER]
    dt = dt_ref[...].reshape(tm, SSM_HEADS)
    da = dt * (-jnp.exp(alog_ref[...]))
    lcum = lcum_ref[...]
    causal = lcum > 0.5
    acum = jnp.dot(lcum, da, precision=HIGHEST, preferred_element_type=F32)
    alast = jnp.dot(bd_ref[...], da, precision=HIGHEST, preferred_element_type=F32)
    acum_t = acum.T
    dec_end_t = jnp.exp(alast.T)
    dt_x, ea_x, w_x = _expand_heads([dt, jnp.exp(acum), dt * jnp.exp(alast - acum)], e_ref[...])
    xdt = (xs * dt_x).astype(BF16)
    xw = xs * w_x
    hmask = _head_masks()
    row = lax.broadcasted_iota(jnp.int32, (tm, 1), 0) // tl
    dec_cols = [jnp.broadcast_to(dec_end_t[:, b * tl:b * tl + 1], (SSM_HEADS, D_STATE)) for b in range(r)]
    for g in range(SSM_GROUPS):
        ch = slice(g * SSM_GROUP_W, (g + 1) * SSM_GROUP_W)
        bg = xbc[:, D_INNER + g * D_STATE:D_INNER + (g + 1) * D_STATE].astype(BF16)
        cg32 = xbc[:, D_INNER + (SSM_GROUPS + g) * D_STATE:D_INNER + (SSM_GROUPS + g + 1) * D_STATE]
        cbm = jnp.where(causal, lax.dot_general(cg32.astype(BF16), bg, NT_DIMS, preferred_element_type=F32), 0.0)
        y_g = xs[:, ch] * dexp_ref[:, ch] + _ssd_intra(g, cbm, acum, acum_t, xdt[:, ch], hmask)
        y_off = jnp.zeros((tm, SSM_GROUP_W), F32)
        xw_g = xw[:, ch]
        for b in range(r):
            s_old = s0_ref[b, ch, :]
            c_b = jnp.where(row == b, cg32, 0.0).astype(BF16)
            xw_b = jnp.where(row == b, xw_g, 0.0).astype(BF16)
            y_off = y_off + lax.dot_general(c_b, s_old.astype(BF16), NT_DIMS, preferred_element_type=F32)
            upd = lax.dot_general(xw_b, bg, TN_DIMS, preferred_element_type=F32)
            dec = jnp.concatenate(
                [jnp.broadcast_to(dec_cols[b][g * HEADS_PER_GROUP + hr:g * HEADS_PER_GROUP + hr + 1, :],
                                  (SSM_HEAD_DIM, D_STATE)) for hr in range(HEADS_PER_GROUP)], axis=0)
            s_ref[b, ch, :] = s_old * dec + upd
        y_ref[:, :, ch] = (y_g + y_off * ea_x[:, ch]).reshape(r, tl, SSM_GROUP_W)


def _ssd_consts(d_skip):
    e_bf = jnp.repeat(jnp.eye(SSM_HEADS, dtype=F32), SSM_HEAD_DIM, axis=1).astype(BF16)
    dexp = jnp.repeat(d_skip, SSM_HEAD_DIM).reshape(1, D_INNER)
    return e_bf, dexp


def _ssd_prompt(xbc, dt, a_log, d_skip):
    nb, l, _ = xbc.shape
    tl = SSM_CHUNK
    assert l % tl == 0
    e_bf, dexp = _ssd_consts(d_skip)
    return pl.pallas_call(
        _ssd_prompt_kernel,
        grid=(nb, l // tl),
        in_specs=[pl.BlockSpec((1, tl, CONV_DIM), lambda i, j: (i, j, 0)),
                  pl.BlockSpec((1, tl, SSM_HEADS), lambda i, j: (i, j, 0)),
                  _const_spec((1, SSM_HEADS)),
                  _const_spec((1, D_INNER)),
                  _const_spec((tl, tl)),
                  _const_spec((SSM_HEADS, D_INNER))],
        out_specs=[pl.BlockSpec((1, tl, D_INNER), lambda i, j: (i, j, 0)),
                   pl.BlockSpec((1, D_INNER, D_STATE), lambda i, j: (i, 0, 0))],
        out_shape=[jax.ShapeDtypeStruct((nb, l, D_INNER), F32),
                   jax.ShapeDtypeStruct((nb, D_INNER, D_STATE), F32)],
        scratch_shapes=[pltpu.VMEM((D_STATE, D_INNER), F32)],
        compiler_params=_params("parallel", "arbitrary"),
        name="ssd_scan_prompt",
    )(xbc, dt, a_log.reshape(1, SSM_HEADS), dexp, jnp.tril(jnp.ones((tl, tl), F32)), e_bf)


def _ssd_sample(xbc, dt, a_log, d_skip, state0, r):
    nb, l, _ = xbc.shape
    tm = r * l
    assert l <= SSM_CHUNK and nb % r == 0
    eye_r = jnp.eye(r, dtype=F32)
    lcum = jnp.kron(eye_r, jnp.tril(jnp.ones((l, l), F32)))
    bd = jnp.kron(eye_r, jnp.ones((l, l), F32))
    e_bf, dexp = _ssd_consts(d_skip)
    return pl.pallas_call(
        _ssd_sample_kernel,
        grid=(nb // r,),
        in_specs=[pl.BlockSpec((r, l, CONV_DIM), lambda i: (i, 0, 0)),
                  pl.BlockSpec((r, l, SSM_HEADS), lambda i: (i, 0, 0)),
                  _const_spec((1, SSM_HEADS)),
                  _const_spec((1, D_INNER)),
                  _const_spec((tm, tm)),
                  _const_spec((tm, tm)),
                  _const_spec((SSM_HEADS, D_INNER)),
                  pl.BlockSpec((r, D_INNER, D_STATE), lambda i: (i, 0, 0))],
        out_specs=[pl.BlockSpec((r, l, D_INNER), lambda i: (i, 0, 0)),
                   pl.BlockSpec((r, D_INNER, D_STATE), lambda i: (i, 0, 0))],
        out_shape=[jax.ShapeDtypeStruct((nb, l, D_INNER), F32),
                   jax.ShapeDtypeStruct((nb, D_INNER, D_STATE), F32)],
        compiler_params=_params("parallel"),
        name="ssd_scan_sample",
    )(xbc, dt, a_log.reshape(1, SSM_HEADS), dexp, lcum, bd, e_bf, state0)


def _ssm_out_kernel(x_ref, y_ref, z_ref, mod_ref, ng_ref, wout_ref, o_ref):
    r, tl, d = x_ref.shape
    tm = r * tl
    yz = y_ref[...].reshape(tm, D_INNER) * _silu(z_ref[...].reshape(tm, D_INNER))
    parts = []
    for g in range(SSM_GROUPS):
        ch = slice(g * SSM_GROUP_W, (g + 1) * SSM_GROUP_W)
        parts.append(_rms(yz[:, ch], ng_ref[:, ch]).astype(BF16))
    yn = jnp.concatenate(parts, axis=1)
    out = jnp.dot(yn, wout_ref[...], preferred_element_type=F32)
    o_ref[...] = x_ref[...] + mod_ref[2] * out.reshape(r, tl, d)


def _ssm_out(x, y, z, mod, layer, row0, r, tl, norm_g, w_out):
    nb, l, d = x.shape
    return pl.pallas_call(
        _ssm_out_kernel,
        grid=(nb // r, l // tl),
        in_specs=[pl.BlockSpec((r, tl, d), lambda i, j: (i, j, 0)),
                  pl.BlockSpec((r, tl, D_INNER), lambda i, j: (i, j, 0)),
                  pl.BlockSpec((r, tl, D_INNER), lambda i, j: (i, j, 0)),
                  _mod_spec(layer, 0, r, row0),
                  _const_spec((1, D_INNER)),
                  _const_spec((D_INNER, d))],
        out_specs=pl.BlockSpec((r, tl, d), lambda i, j: (i, j, 0)),
        out_shape=jax.ShapeDtypeStruct((nb, l, d), F32),
        compiler_params=_params("parallel", "parallel"),
        name="ssm_gate_out",
    )(x, y, z, mod, norm_g.reshape(1, D_INNER), w_out.astype(BF16))


def _trunk(x, mod, row0, tiles, ssm_state, conv_state, p):
    nb, l, _ = x.shape
    new_v, new_ssm, new_conv = [], [], []
    for i in range(DEPTH):
        j = i // 2
        if i % 2 == 0:
            x, v = _gm_layer(x, mod, i, row0, *tiles["gm"], p["norm1_g"][i], p["gm_w_in"][j], p["gm_ln_g"][j],
                             p["gm_ln_b"][j], p["gm_w_s"][j], p["gm_b_s"][j], p["gm_w_out"][j])
            new_v.append(v)
        else:
            cs8 = jnp.pad(conv_state[j], ((0, 0), (CONV_PAD - (CONV_W - 1), 0), (0, 0)))
            z, xbc, dt, cnew = _ssm_in(x, mod, i, row0, *tiles["ssm_in"], p["norm1_g"][i], p["ssm_w_in"][j],
                                       p["ssm_conv_w"][j], p["ssm_conv_b"][j], p["ssm_dt_bias"][j], cs8)
            if ssm_state is None:
                y, s_new = _ssd_prompt(xbc, dt, p["ssm_a_log"][j], p["ssm_d"][j])
            else:
                s0 = ssm_state[j].reshape(nb, D_INNER, D_STATE)
                y, s_new = _ssd_sample(xbc, dt, p["ssm_a_log"][j], p["ssm_d"][j], s0, tiles["ssd"])
            x = _ssm_out(x, y, z, mod, i, row0, *tiles["ssm_out"], p["ssm_norm_g"][j], p["ssm_w_out"][j])
            new_conv.append(cnew[:, CONV_PAD - (CONV_W - 1):, :])
            new_ssm.append(s_new.reshape(nb, SSM_HEADS, SSM_HEAD_DIM, D_STATE))
        x = _mlp_layer(x, mod, i, row0, *tiles["mlp"], p["norm2_g"][i], p["mlp_w1"][i], p["mlp_w2"][i],
                       p["final_g"], final=(i == DEPTH - 1))
    return x, jnp.stack(new_v), jnp.stack(new_ssm), jnp.stack(new_conv)


PROMPT_TILES = {"gm": (1, 256), "mlp": (1, 512), "ssm_in": (1, 256), "ssm_out": (1, 512)}
SAMPLE_TILES = {"gm": (32, 8), "mlp": (64, 8), "ssm_in": (32, 8), "ssd": 8, "ssm_out": (64, 8)}


def kernel(x_prompt, x_sample, c_prompt, c_sample, state_ssm, state_conv, ada_w, ada_b, norm1_g, norm2_g, gm_w_in, gm_ln_g, gm_ln_b, gm_w_s, gm_b_s, gm_w_out, ssm_w_in, ssm_conv_w, ssm_conv_b, ssm_dt_bias, ssm_a_log, ssm_d, ssm_norm_g, ssm_w_out, mlp_w1, mlp_w2, final_g):
    p = dict(norm1_g=norm1_g, norm2_g=norm2_g, gm_w_in=gm_w_in, gm_ln_g=gm_ln_g, gm_ln_b=gm_ln_b, gm_w_s=gm_w_s,
             gm_b_s=gm_b_s, gm_w_out=gm_w_out, ssm_w_in=ssm_w_in, ssm_conv_w=ssm_conv_w, ssm_conv_b=ssm_conv_b,
             ssm_dt_bias=ssm_dt_bias, ssm_a_log=ssm_a_log, ssm_d=ssm_d, ssm_norm_g=ssm_norm_g, ssm_w_out=ssm_w_out,
             mlp_w1=mlp_w1, mlp_w2=mlp_w2, final_g=final_g)
    n_sample = x_sample.shape[0]
    n_prompt = x_prompt.shape[0]
    n_ssm = state_ssm.shape[0]
    mod = _ada(jnp.concatenate([c_sample, c_prompt], axis=0), ada_w, ada_b)
    conv0 = jnp.zeros((n_ssm, n_prompt, CONV_W - 1, CONV_DIM), F32)
    y_p, v_p, ssm_p, conv_p = _trunk(x_prompt, mod, n_sample, PROMPT_TILES, None, conv0, p)
    y_s, v_s, ssm_s, conv_s = _trunk(x_sample, mod, 0, SAMPLE_TILES, state_ssm, state_conv, p)
    return (y_p, y_s, v_p, v_s, ssm_p, conv_p, ssm_s, conv_s)
```

```python
import functools

import jax
import jax.numpy as jnp
from jax import lax
from jax.experimental import pallas as pl
from jax.experimental.pallas import tpu as pltpu

F32 = jnp.float32
BF16 = jnp.bfloat16
HIGHEST = lax.Precision.HIGHEST

D_MODEL = 1024
DEPTH = 2
N_MOD = 6
EPS = 1e-6
GM_CHUNK = 128
D_GM = 2 * D_MODEL
GM_GROUPS = 8
GM_GROUP_W = D_GM // GM_GROUPS
D_INNER = 2 * D_MODEL
SSM_HEAD_DIM = 64
SSM_HEADS = D_INNER // SSM_HEAD_DIM
SSM_GROUPS = 8
HEADS_PER_GROUP = SSM_HEADS // SSM_GROUPS
SSM_GROUP_W = D_INNER // SSM_GROUPS
D_STATE = 128
CONV_W = 4
CONV_DIM = D_INNER + 2 * SSM_GROUPS * D_STATE
D_ZX = D_INNER + CONV_DIM
SSM_CHUNK = 128
D_FF = 4 * D_MODEL
FF_CHUNK = 1024
SUBLANES = 8
CONV_PAD = SUBLANES
VMEM_LIMIT = 56 * 1024 * 1024


def _silu(x):
    return x * (1.0 / (1.0 + jnp.exp(-x)))


def _softplus(x):
    return jnp.maximum(x, 0.0) + jnp.log1p(jnp.exp(-jnp.abs(x)))


def _gelu_tanh(x):
    return x * (0.5 * (1.0 + jnp.tanh(0.7978845608028654 * (x + 0.044715 * (x * x * x)))))


def _rms(x, g):
    return x * lax.rsqrt(jnp.mean(x * x, axis=-1, keepdims=True) + EPS) * g


def _norm_modulate(x3, g, shift, scale):
    r, tl, d = x3.shape
    hn = _rms(x3.reshape(r * tl, d), g)
    h3 = hn.reshape(r, tl, d) * (1.0 + scale) + shift
    return h3.reshape(r * tl, d).astype(BF16)


def _const_spec(shape):
    return pl.BlockSpec(shape, lambda *_: (0,) * len(shape), pipeline_mode=pl.Buffered(1))


def _params(*sem):
    return pltpu.CompilerParams(dimension_semantics=sem, vmem_limit_bytes=VMEM_LIMIT)


def _ada_kernel(c_ref, w_ref, b_ref, o_ref):
    sc = _silu(c_ref[...]).astype(BF16)
    o_ref[...] = jnp.dot(sc, w_ref[...].astype(BF16), preferred_element_type=F32) + b_ref[...]


def _ada(c_all, ada_w, ada_b):
    nb = c_all.shape[0]
    out = pl.pallas_call(
        _ada_kernel,
        grid=(DEPTH, N_MOD),
        in_specs=[pl.BlockSpec((nb, D_MODEL), lambda i, k: (0, 0)),
                  pl.BlockSpec((None, D_MODEL, D_MODEL), lambda i, k: (i, 0, k)),
                  pl.BlockSpec((None, None, 1, D_MODEL), lambda i, k: (i, k, 0, 0))],
        out_specs=pl.BlockSpec((None, None, nb, D_MODEL), lambda i, k: (i, k, 0, 0)),
        out_shape=jax.ShapeDtypeStruct((DEPTH, N_MOD, nb, D_MODEL), F32),
        compiler_params=_params("parallel", "parallel"),
        name="ada_mod",
    )(c_all, ada_w, ada_b.reshape(DEPTH, N_MOD, 1, D_MODEL))
    return out.reshape(DEPTH, N_MOD, nb, 1, D_MODEL)


def _mod_spec(layer, half, r, row0):
    return pl.BlockSpec((None, 3, r, 1, D_MODEL), lambda i, j: (layer, half, row0 // r + i, 0, 0))


def _gm_kernel(x_ref, mod_ref, g_ref, win_ref, lng_ref, lnb_ref, wmix_ref, bexp_ref, wout_ref,
               xo_ref, v_ref, *, v_tail):
    r, tl, d = x_ref.shape
    tm = r * tl
    x3 = x_ref[...]
    h = _norm_modulate(x3, g_ref[...], mod_ref[0], mod_ref[1])
    z = _gelu_tanh(jnp.dot(h, win_ref[...], preferred_element_type=F32))
    u = z[:, :D_GM]
    vr = z[:, D_GM:]
    xc = vr - jnp.mean(vr, axis=-1, keepdims=True)
    v = xc * lax.rsqrt(jnp.mean(xc * xc, axis=-1, keepdims=True) + EPS) * lng_ref[...] + lnb_ref[...]
    vb = v.astype(BF16)
    chunks = []
    for c in range(tm // GM_CHUNK):
        vc = vb[c * GM_CHUNK:(c + 1) * GM_CHUNK]
        parts = [jnp.dot(wmix_ref[g], vc[:, g * GM_GROUP_W:(g + 1) * GM_GROUP_W], preferred_element_type=F32)
                 for g in range(GM_GROUPS)]
        chunks.append(jnp.concatenate(parts, axis=1) + bexp_ref[...])
    s = jnp.concatenate(chunks, axis=0) if len(chunks) > 1 else chunks[0]
    y = jnp.dot((u * s).astype(BF16), wout_ref[...], preferred_element_type=F32)
    xo_ref[...] = x3 + mod_ref[2] * y.reshape(r, tl, d)

    @pl.when(pl.program_id(1) == pl.num_programs(1) - 1)
    def _():
        v_ref[...] = v.reshape(r, tl, D_GM)[:, tl - v_tail:, :]


def _gm_layer(x, mod, layer, row0, r, tl, g, w_in, ln_g, ln_b, w_s, b_s, w_out):
    nb, l, d = x.shape
    q = min(l, GM_CHUNK)
    assert l % q == 0 and GM_CHUNK % q == 0 and (r * tl) % GM_CHUNK == 0 and tl % q == 0
    rep = GM_CHUNK // q
    tri = jnp.tril(jnp.ones((q, q), F32))
    wq = w_s[:, :q, :q] * tri
    wmix = jnp.einsum("ab,gts->gatbs", jnp.eye(rep, dtype=F32), wq).reshape(GM_GROUPS, GM_CHUNK, GM_CHUNK)
    bq = jnp.tile(b_s[:, :q].T, (rep, 1))
    bexp = jnp.repeat(bq, GM_GROUP_W, axis=1)
    v_tail = l - ((l - 1) // GM_CHUNK) * GM_CHUNK
    assert v_tail <= tl
    kern = functools.partial(_gm_kernel, v_tail=v_tail)
    xo, v = pl.pallas_call(
        kern,
        grid=(nb // r, l // tl),
        in_specs=[pl.BlockSpec((r, tl, d), lambda i, j: (i, j, 0)),
                  _mod_spec(layer, 0, r, row0),
                  _const_spec((1, d)),
                  _const_spec((d, 2 * D_GM)),
                  _const_spec((1, D_GM)),
                  _const_spec((1, D_GM)),
                  _const_spec((GM_GROUPS, GM_CHUNK, GM_CHUNK)),
                  _const_spec((GM_CHUNK, D_GM)),
                  _const_spec((D_GM, d))],
        out_specs=[pl.BlockSpec((r, tl, d), lambda i, j: (i, j, 0)),
                   pl.BlockSpec((r, v_tail, D_GM), lambda i, j: (i, 0, 0))],
        out_shape=[jax.ShapeDtypeStruct((nb, l, d), F32),
                   jax.ShapeDtypeStruct((nb, v_tail, D_GM), F32)],
        compiler_params=_params("parallel", "arbitrary"),
        name="gmlp_mixer",
    )(x, mod, g.reshape(1, d), w_in.astype(BF16), ln_g.reshape(1, D_GM), ln_b.reshape(1, D_GM),
      wmix.astype(BF16), bexp, w_out.astype(BF16))
    return xo, v


def _mlp_kernel(x_ref, mod_ref, g_ref, w1_ref, w2_ref, gf_ref, o_ref, *, final):
    r, tl, d = x_ref.shape
    x3 = x_ref[...]
    h = _norm_modulate(x3, g_ref[...], mod_ref[0], mod_ref[1])
    acc = jnp.zeros((r * tl, d), F32)
    for k in range(D_FF // FF_CHUNK):
        a = jnp.dot(h, w1_ref[:, k * FF_CHUNK:(k + 1) * FF_CHUNK], preferred_element_type=F32)
        a = jnp.square(jnp.maximum(a, 0.0)).astype(BF16)
        acc = acc + jnp.dot(a, w2_ref[k * FF_CHUNK:(k + 1) * FF_CHUNK, :], preferred_element_type=F32)
    xo = x3 + mod_ref[2] * acc.reshape(r, tl, d)
    if final:
        xo = _rms(xo.reshape(r * tl, d), gf_ref[...]).reshape(r, tl, d)
    o_ref[...] = xo


def _mlp_layer(x, mod, layer, row0, r, tl, g, w1, w2, final_g, final):
    nb, l, d = x.shape
    kern = functools.partial(_mlp_kernel, final=final)
    return pl.pallas_call(
        kern,
        grid=(nb // r, l // tl),
        in_specs=[pl.BlockSpec((r, tl, d), lambda i, j: (i, j, 0)),
                  _mod_spec(layer, 1, r, row0),
                  _const_spec((1, d)),
                  _const_spec((d, D_FF)),
                  _const_spec((D_FF, d)),
                  _const_spec((1, d))],
        out_specs=pl.BlockSpec((r, tl, d), lambda i, j: (i, j, 0)),
        out_shape=jax.ShapeDtypeStruct((nb, l, d), F32),
        compiler_params=_params("parallel", "parallel"),
        name="relu2_mlp",
    )(x, mod, g.reshape(1, d), w1.astype(BF16), w2.astype(BF16), final_g.reshape(1, d))


def _ssm_in_kernel(x_ref, mod_ref, g_ref, wzx_ref, wdt_ref, cw_ref, cb_ref, dtb_ref, cs_ref,
                   z_ref, xbc_ref, dt_ref, cnew_ref, xp_ref):
    r, tl, d = x_ref.shape
    tm = r * tl

    @pl.when(pl.program_id(1) == 0)
    def _():
        xp_ref[:, 0:CONV_PAD, :] = cs_ref[...]

    h = _norm_modulate(x_ref[...], g_ref[...], mod_ref[0], mod_ref[1])
    z_ref[...] = jnp.dot(h, wzx_ref[:, :D_INNER], preferred_element_type=F32).reshape(r, tl, D_INNER)
    xbc = jnp.dot(h, wzx_ref[:, D_INNER:], preferred_element_type=F32)
    dt_raw = jnp.dot(h, wdt_ref[...], preferred_element_type=F32)
    dt_ref[...] = _softplus(dt_raw + dtb_ref[...]).reshape(r, tl, SSM_HEADS)
    xp_ref[:, CONV_PAD:CONV_PAD + tl, :] = xbc.reshape(r, tl, CONV_DIM)
    conv = cb_ref[...].reshape(1, 1, CONV_DIM)
    for k in range(CONV_W):
        off = CONV_PAD - (CONV_W - 1) + k
        conv = conv + xp_ref[:, off:off + tl, :] * cw_ref[k:k + 1, :].reshape(1, 1, CONV_DIM)
    xbc_ref[...] = _silu(conv)
    tail = xp_ref[:, tl:tl + CONV_PAD, :]
    cnew_ref[...] = tail
    xp_ref[:, 0:CONV_PAD, :] = tail


def _ssm_in(x, mod, layer, row0, r, tl, g, w_in, conv_w, conv_b, dt_bias, conv_state8):
    nb, l, d = x.shape
    w_bf = w_in.astype(BF16)
    return pl.pallas_call(
        _ssm_in_kernel,
        grid=(nb // r, l // tl),
        in_specs=[pl.BlockSpec((r, tl, d), lambda i, j: (i, j, 0)),
                  _mod_spec(layer, 0, r, row0),
                  _const_spec((1, d)),
                  _const_spec((d, D_ZX)),
                  _const_spec((d, SSM_HEADS)),
                  _const_spec((CONV_W, CONV_DIM)),
                  _const_spec((1, CONV_DIM)),
                  _const_spec((1, SSM_HEADS)),
                  pl.BlockSpec((r, CONV_PAD, CONV_DIM), lambda i, j: (i, 0, 0))],
        out_specs=[pl.BlockSpec((r, tl, D_INNER), lambda i, j: (i, j, 0)),
                   pl.BlockSpec((r, tl, CONV_DIM), lambda i, j: (i, j, 0)),
                   pl.BlockSpec((r, tl, SSM_HEADS), lambda i, j: (i, j, 0)),
                   pl.BlockSpec((r, CONV_PAD, CONV_DIM), lambda i, j: (i, 0, 0))],
        out_shape=[jax.ShapeDtypeStruct((nb, l, D_INNER), F32),
                   jax.ShapeDtypeStruct((nb, l, CONV_DIM), F32),
                   jax.ShapeDtypeStruct((nb, l, SSM_HEADS), F32),
                   jax.ShapeDtypeStruct((nb, CONV_PAD, CONV_DIM), F32)],
        scratch_shapes=[pltpu.VMEM((r, CONV_PAD + tl, CONV_DIM), F32)],
        compiler_params=_params("parallel", "arbitrary"),
        name="ssm_in_conv",
    )(x, mod, g.reshape(1, d), w_bf[:, :D_ZX], w_bf[:, D_ZX:], conv_w, conv_b.reshape(1, CONV_DIM),
      dt_bias.reshape(1, SSM_HEADS), conv_state8)


NT_DIMS = (((1,), (1,)), ((), ()))
TN_DIMS = (((0,), (0,)), ((), ()))


def _expand_heads(vals, e_bf):
    tm = vals[0].shape[0]
    v = jnp.concatenate(vals, axis=0)
    hi = v.astype(BF16)
    lo = (v - hi.astype(F32)).astype(BF16)
    x = jnp.dot(hi, e_bf, preferred_element_type=F32) + jnp.dot(lo, e_bf, preferred_element_type=F32)
    return [x[k * tm:(k + 1) * tm] for k in range(len(vals))]


def _head_masks():
    lane = lax.broadcasted_iota(jnp.int32, (1, SSM_GROUP_W), 1) // SSM_HEAD_DIM
    return [jnp.where(lane == hr, 1.0, 0.0).astype(BF16) for hr in range(HEADS_PER_GROUP)]


def _ssd_intra(g, cbm, acum, acum_t, xdt_g, hmask):
    acc = None
    for hr in range(HEADS_PER_GROUP):
        hh = g * HEADS_PER_GROUP + hr
        seg = jnp.minimum(acum[:, hh:hh + 1] - acum_t[hh:hh + 1, :], 0.0)
        lmat = (cbm * jnp.exp(seg)).astype(BF16)
        part = jnp.dot(lmat, xdt_g * hmask[hr], preferred_element_type=F32)
        acc = part if acc is None else acc + part
    return acc


def _gate_norm(y_g, z_g, ng_g):
    return _rms(y_g * _silu(z_g), ng_g)


def _ssm_prompt_kernel(x_ref, mod_ref, g_ref, wzx_ref, wdt_ref, cw_ref, cb_ref, dtb_ref, alog_ref, dexp_ref,
                       lcum_ref, e_ref, ng_ref, yn_ref, cnew_ref, s_ref, xp_ref, xc_ref, z_ref, st_ref):
    _, tl, _ = x_ref.shape
    j = pl.program_id(1)

    @pl.when(j == 0)
    def _():
        xp_ref[0:CONV_PAD, :] = jnp.zeros((CONV_PAD, CONV_DIM), F32)
        st_ref[...] = jnp.zeros_like(st_ref)

    h = _norm_modulate(x_ref[...], g_ref[...], mod_ref[0], mod_ref[1])
    z_ref[...] = jnp.dot(h, wzx_ref[:, :D_INNER], preferred_element_type=F32)
    xp_ref[CONV_PAD:CONV_PAD + tl, :] = jnp.dot(h, wzx_ref[:, D_INNER:], preferred_element_type=F32)
    dt_all = _softplus(jnp.dot(h, wdt_ref[...], preferred_element_type=F32) + dtb_ref[...])
    conv = cb_ref[...]
    for k in range(CONV_W):
        off = CONV_PAD - (CONV_W - 1) + k
        conv = conv + xp_ref[off:off + tl, :] * cw_ref[k:k + 1, :]
    xc_ref[...] = _silu(conv)
    tail = xp_ref[tl:tl + CONV_PAD, :]
    cnew_ref[0] = tail
    xp_ref[0:CONV_PAD, :] = tail

    a_neg = -jnp.exp(alog_ref[...])
    lcum = lcum_ref[...]
    causal = lcum > 0.5
    hmask = _head_masks()
    for c in range(tl // SSM_CHUNK):
        rows = slice(c * SSM_CHUNK, (c + 1) * SSM_CHUNK)
        xs = xc_ref[rows, :D_INNER]
        dt = dt_all[rows]
        acum = jnp.dot(lcum, dt * a_neg, precision=HIGHEST, preferred_element_type=F32)
        acum_t = acum.T
        alast = acum[SSM_CHUNK - 1:SSM_CHUNK, :]
        dt_x, ea_x, w_x = _expand_heads([dt, jnp.exp(acum), dt * jnp.exp(alast - acum)], e_ref[...])
        dec_x = ea_x[SSM_CHUNK - 1:SSM_CHUNK, :]
        xdt = (xs * dt_x).astype(BF16)
        xw = (xs * w_x).astype(BF16)
        for g in range(SSM_GROUPS):
            ch = slice(g * SSM_GROUP_W, (g + 1) * SSM_GROUP_W)
            b_lo = D_INNER + g * D_STATE
            c_lo = D_INNER + (SSM_GROUPS + g) * D_STATE
            bgt = xc_ref[rows, b_lo:b_lo + D_STATE].T.astype(BF16)
            cg = xc_ref[rows, c_lo:c_lo + D_STATE].astype(BF16)
            cbm = jnp.where(causal, jnp.dot(cg, bgt, preferred_element_type=F32), 0.0)
            y_g = xs[:, ch] * dexp_ref[:, ch] + _ssd_intra(g, cbm, acum, acum_t, xdt[:, ch], hmask)
            s_old = st_ref[:, ch]
            y_g = y_g + jnp.dot(cg, s_old.astype(BF16), preferred_element_type=F32) * ea_x[:, ch]
            st_ref[:, ch] = s_old * dec_x[:, ch] + jnp.dot(bgt, xw[:, ch], preferred_element_type=F32)
            yn_ref[0, rows, ch] = _gate_norm(y_g, z_ref[rows, ch], ng_ref[:, ch]).astype(BF16)

    @pl.when(j == pl.num_programs(1) - 1)
    def _():
        s_ref[0] = st_ref[...].T


def _ssd_sample_kernel(xbc_ref, dt_ref, z_ref, alog_ref, dexp_ref, lcum_ref, bd_ref, e_ref, ng_ref, s0_ref,
                       yn_ref, s_ref):
    r, tl, _ = xbc_ref.shape
    tm = r * tl
    xbc = xbc_ref[...].reshape(tm, CONV_DIM)
    xs = xbc[:, :D_INNER]
    dt = dt_ref[...].reshape(tm, SSM_HEADS)
    da = dt * (-jnp.exp(alog_ref[...]))
    lcum = lcum_ref[...]
    causal = lcum > 0.5
    acum = jnp.dot(lcum, da, precision=HIGHEST, preferred_element_type=F32)
    alast = jnp.dot(bd_ref[...], da, precision=HIGHEST, preferred_element_type=F32)
    acum_t = acum.T
    dec_end_t = jnp.exp(alast.T)
    dt_x, ea_x, w_x = _expand_heads([dt, jnp.exp(acum), dt * jnp.exp(alast - acum)], e_ref[...])
    xdt = (xs * dt_x).astype(BF16)
    xw = xs * w_x
    hmask = _head_masks()
    row = lax.broadcasted_iota(jnp.int32, (tm, 1), 0) // tl
    dec_cols = [jnp.broadcast_to(dec_end_t[:, b * tl:b * tl + 1], (SSM_HEADS, D_STATE)) for b in range(r)]
    for g in range(SSM_GROUPS):
        ch = slice(g * SSM_GROUP_W, (g + 1) * SSM_GROUP_W)
        bg = xbc[:, D_INNER + g * D_STATE:D_INNER + (g + 1) * D_STATE].astype(BF16)
        cg32 = xbc[:, D_INNER + (SSM_GROUPS + g) * D_STATE:D_INNER + (SSM_GROUPS + g + 1) * D_STATE]
        cbm = jnp.where(causal, lax.dot_general(cg32.astype(BF16), bg, NT_DIMS, preferred_element_type=F32), 0.0)
        y_g = xs[:, ch] * dexp_ref[:, ch] + _ssd_intra(g, cbm, acum, acum_t, xdt[:, ch], hmask)
        y_off = jnp.zeros((tm, SSM_GROUP_W), F32)
        xw_g = xw[:, ch]
        for b in range(r):
            s_old = s0_ref[b, ch, :]
            c_b = jnp.where(row == b, cg32, 0.0).astype(BF16)
            xw_b = jnp.where(row == b, xw_g, 0.0).astype(BF16)
            y_off = y_off + lax.dot_general(c_b, s_old.astype(BF16), NT_DIMS, preferred_element_type=F32)
            upd = lax.dot_general(xw_b, bg, TN_DIMS, preferred_element_type=F32)
            dec = jnp.concatenate(
                [jnp.broadcast_to(dec_cols[b][g * HEADS_PER_GROUP + hr:g * HEADS_PER_GROUP + hr + 1, :],
                                  (SSM_HEAD_DIM, D_STATE)) for hr in range(HEADS_PER_GROUP)], axis=0)
            s_ref[b, ch, :] = s_old * dec + upd
        z_g = z_ref[:, :, ch].reshape(tm, SSM_GROUP_W)
        yn_g = _gate_norm(y_g + y_off * ea_x[:, ch], z_g, ng_ref[:, ch])
        yn_ref[:, :, ch] = yn_g.reshape(r, tl, SSM_GROUP_W).astype(BF16)


def _ssd_consts(d_skip):
    e_bf = jnp.repeat(jnp.eye(SSM_HEADS, dtype=F32), SSM_HEAD_DIM, axis=1).astype(BF16)
    dexp = jnp.repeat(d_skip, SSM_HEAD_DIM).reshape(1, D_INNER)
    return e_bf, dexp


def _ssm_prompt(x, mod, layer, row0, tl, g, w_in, conv_w, conv_b, dt_bias, a_log, d_skip, norm_g):
    nb, l, d = x.shape
    assert l % tl == 0 and tl % SSM_CHUNK == 0
    w_bf = w_in.astype(BF16)
    e_bf, dexp = _ssd_consts(d_skip)
    return pl.pallas_call(
        _ssm_prompt_kernel,
        grid=(nb, l // tl),
        in_specs=[pl.BlockSpec((1, tl, d), lambda i, j: (i, j, 0)),
                  _mod_spec(layer, 0, 1, row0),
                  _const_spec((1, d)),
                  _const_spec((d, D_ZX)),
                  _const_spec((d, SSM_HEADS)),
                  _const_spec((CONV_W, CONV_DIM)),
                  _const_spec((1, CONV_DIM)),
                  _const_spec((1, SSM_HEADS)),
                  _const_spec((1, SSM_HEADS)),
                  _const_spec((1, D_INNER)),
                  _const_spec((SSM_CHUNK, SSM_CHUNK)),
                  _const_spec((SSM_HEADS, D_INNER)),
                  _const_spec((1, D_INNER))],
        out_specs=[pl.BlockSpec((1, tl, D_INNER), lambda i, j: (i, j, 0)),
                   pl.BlockSpec((1, CONV_PAD, CONV_DIM), lambda i, j: (i, 0, 0)),
                   pl.BlockSpec((1, D_INNER, D_STATE), lambda i, j: (i, 0, 0))],
        out_shape=[jax.ShapeDtypeStruct((nb, l, D_INNER), BF16),
                   jax.ShapeDtypeStruct((nb, CONV_PAD, CONV_DIM), F32),
                   jax.ShapeDtypeStruct((nb, D_INNER, D_STATE), F32)],
        scratch_shapes=[pltpu.VMEM((CONV_PAD + tl, CONV_DIM), F32),
                        pltpu.VMEM((tl, CONV_DIM), F32),
                        pltpu.VMEM((tl, D_INNER), F32),
                        pltpu.VMEM((D_STATE, D_INNER), F32)],
        compiler_params=_params("parallel", "arbitrary"),
        name="ssm_prompt",
    )(x, mod, g.reshape(1, d), w_bf[:, :D_ZX], w_bf[:, D_ZX:], conv_w, conv_b.reshape(1, CONV_DIM),
      dt_bias.reshape(1, SSM_HEADS), a_log.reshape(1, SSM_HEADS), dexp,
      jnp.tril(jnp.ones((SSM_CHUNK, SSM_CHUNK), F32)), e_bf, norm_g.reshape(1, D_INNER))


def _ssd_sample(xbc, dt, z, a_log, d_skip, norm_g, state0, r):
    nb, l, _ = xbc.shape
    tm = r * l
    assert l <= SSM_CHUNK and nb % r == 0
    eye_r = jnp.eye(r, dtype=F32)
    lcum = jnp.kron(eye_r, jnp.tril(jnp.ones((l, l), F32)))
    bd = jnp.kron(eye_r, jnp.ones((l, l), F32))
    e_bf, dexp = _ssd_consts(d_skip)
    return pl.pallas_call(
        _ssd_sample_kernel,
        grid=(nb // r,),
        in_specs=[pl.BlockSpec((r, l, CONV_DIM), lambda i: (i, 0, 0)),
                  pl.BlockSpec((r, l, SSM_HEADS), lambda i: (i, 0, 0)),
                  pl.BlockSpec((r, l, D_INNER), lambda i: (i, 0, 0)),
                  _const_spec((1, SSM_HEADS)),
                  _const_spec((1, D_INNER)),
                  _const_spec((tm, tm)),
                  _const_spec((tm, tm)),
                  _const_spec((SSM_HEADS, D_INNER)),
                  _const_spec((1, D_INNER)),
                  pl.BlockSpec((r, D_INNER, D_STATE), lambda i: (i, 0, 0))],
        out_specs=[pl.BlockSpec((r, l, D_INNER), lambda i: (i, 0, 0)),
                   pl.BlockSpec((r, D_INNER, D_STATE), lambda i: (i, 0, 0))],
        out_shape=[jax.ShapeDtypeStruct((nb, l, D_INNER), BF16),
                   jax.ShapeDtypeStruct((nb, D_INNER, D_STATE), F32)],
        compiler_params=_params("parallel"),
        name="ssd_scan_sample",
    )(xbc, dt, z, a_log.reshape(1, SSM_HEADS), dexp, lcum, bd, e_bf, norm_g.reshape(1, D_INNER), state0)


def _ssm_out_kernel(x_ref, yn_ref, mod_ref, wout_ref, o_ref):
    r, tl, d = x_ref.shape
    if r == 1:
        yn = yn_ref[0]
    else:
        yn = yn_ref[...].astype(F32).reshape(r * tl, D_INNER).astype(BF16)
    out = jnp.dot(yn, wout_ref[...], preferred_element_type=F32)
    o_ref[...] = x_ref[...] + mod_ref[2] * out.reshape(r, tl, d)


def _ssm_out(x, yn, mod, layer, row0, r, tl, w_out):
    nb, l, d = x.shape
    return pl.pallas_call(
        _ssm_out_kernel,
        grid=(nb // r, l // tl),
        in_specs=[pl.BlockSpec((r, tl, d), lambda i, j: (i, j, 0)),
                  pl.BlockSpec((r, tl, D_INNER), lambda i, j: (i, j, 0)),
                  _mod_spec(layer, 0, r, row0),
                  _const_spec((D_INNER, d))],
        out_specs=pl.BlockSpec((r, tl, d), lambda i, j: (i, j, 0)),
        out_shape=jax.ShapeDtypeStruct((nb, l, d), F32),
        compiler_params=_params("parallel", "parallel"),
        name="ssm_out_proj",
    )(x, yn, mod, w_out.astype(BF16))


def _trunk(x, mod, row0, tiles, ssm_state, conv_state, p):
    nb, l, _ = x.shape
    new_v, new_ssm, new_conv = [], [], []
    for i in range(DEPTH):
        j = i // 2
        if i % 2 == 0:
            x, v = _gm_layer(x, mod, i, row0, *tiles["gm"], p["norm1_g"][i], p["gm_w_in"][j], p["gm_ln_g"][j],
                             p["gm_ln_b"][j], p["gm_w_s"][j], p["gm_b_s"][j], p["gm_w_out"][j])
            new_v.append(v)
        else:
            if ssm_state is None:
                yn, cnew, s_new = _ssm_prompt(x, mod, i, row0, tiles["ssm"], p["norm1_g"][i], p["ssm_w_in"][j],
                                              p["ssm_conv_w"][j], p["ssm_conv_b"][j], p["ssm_dt_bias"][j],
                                              p["ssm_a_log"][j], p["ssm_d"][j], p["ssm_norm_g"][j])
            else:
                cs8 = jnp.pad(conv_state[j], ((0, 0), (CONV_PAD - (CONV_W - 1), 0), (0, 0)))
                z, xbc, dt, cnew = _ssm_in(x, mod, i, row0, *tiles["ssm_in"], p["norm1_g"][i], p["ssm_w_in"][j],
                                           p["ssm_conv_w"][j], p["ssm_conv_b"][j], p["ssm_dt_bias"][j], cs8)
                s0 = ssm_state[j].reshape(nb, D_INNER, D_STATE)
                yn, s_new = _ssd_sample(xbc, dt, z, p["ssm_a_log"][j], p["ssm_d"][j], p["ssm_norm_g"][j], s0,
                                        tiles["ssd"])
            x = _ssm_out(x, yn, mod, i, row0, *tiles["ssm_out"], p["ssm_w_out"][j])
            new_conv.append(cnew[:, CONV_PAD - (CONV_W - 1):, :])
            new_ssm.append(s_new.reshape(nb, SSM_HEADS, SSM_HEAD_DIM, D_STATE))
        x = _mlp_layer(x, mod, i, row0, *tiles["mlp"], p["norm2_g"][i], p["mlp_w1"][i], p["mlp_w2"][i],
                       p["final_g"], final=(i == DEPTH - 1))
    return x, jnp.stack(new_v), jnp.stack(new_ssm), jnp.stack(new_conv)


PROMPT_TILES = {"gm": (1, 256), "mlp": (1, 512), "ssm": 256, "ssm_out": (1, 512)}
SAMPLE_TILES = {"gm": (32, 8), "mlp": (64, 8), "ssm_in": (32, 8), "ssd": 8, "ssm_out": (64, 8)}


def kernel(x_prompt, x_sample, c_prompt, c_sample, state_ssm, state_conv, ada_w, ada_b, norm1_g, norm2_g, gm_w_in, gm_ln_g, gm_ln_b, gm_w_s, gm_b_s, gm_w_out, ssm_w_in, ssm_conv_w, ssm_conv_b, ssm_dt_bias, ssm_a_log, ssm_d, ssm_norm_g, ssm_w_out, mlp_w1, mlp_w2, final_g):
    p = dict(norm1_g=norm1_g, norm2_g=norm2_g, gm_w_in=gm_w_in, gm_ln_g=gm_ln_g, gm_ln_b=gm_ln_b, gm_w_s=gm_w_s,
             gm_b_s=gm_b_s, gm_w_out=gm_w_out, ssm_w_in=ssm_w_in, ssm_conv_w=ssm_conv_w, ssm_conv_b=ssm_conv_b,
             ssm_dt_bias=ssm_dt_bias, ssm_a_log=ssm_a_log, ssm_d=ssm_d, ssm_norm_g=ssm_norm_g, ssm_w_out=ssm_w_out,
             mlp_w1=mlp_w1, mlp_w2=mlp_w2, final_g=final_g)
    n_sample = x_sample.shape[0]
    n_prompt = x_prompt.shape[0]
    n_ssm = state_ssm.shape[0]
    mod = _ada(jnp.concatenate([c_sample, c_prompt], axis=0), ada_w, ada_b)
    y_p, v_p, ssm_p, conv_p = _trunk(x_prompt, mod, n_sample, PROMPT_TILES, None, None, p)
    y_s, v_s, ssm_s, conv_s = _trunk(x_sample, mod, 0, SAMPLE_TILES, state_ssm, state_conv, p)
    return (y_p, y_s, v_p, v_s, ssm_p, conv_p, ssm_s, conv_s)
```

```python
import functools

import jax
import jax.numpy as jnp
from jax import lax
from jax.experimental import pallas as pl
from jax.experimental.pallas import tpu as pltpu

F32 = jnp.float32
BF16 = jnp.bfloat16
HIGHEST = lax.Precision.HIGHEST

D_MODEL = 1024
DEPTH = 2
N_MOD = 6
EPS = 1e-6
LOG2E = 1.4426950408889634
GM_CHUNK = 128
D_GM = 2 * D_MODEL
GM_GROUPS = 8
GM_GROUP_W = D_GM // GM_GROUPS
D_INNER = 2 * D_MODEL
SSM_HEAD_DIM = 64
SSM_HEADS = D_INNER // SSM_HEAD_DIM
SSM_GROUPS = 8
HEADS_PER_GROUP = SSM_HEADS // SSM_GROUPS
SSM_GROUP_W = D_INNER // SSM_GROUPS
D_STATE = 128
CONV_W = 4
CONV_DIM = D_INNER + 2 * SSM_GROUPS * D_STATE
D_ZX = D_INNER + CONV_DIM
SSM_CHUNK = 128
D_FF = 4 * D_MODEL
FF_CHUNK = 1024
SUBLANES = 8
CONV_PAD = SUBLANES
ROW_TILE = 256
LANE_TILE = 1024
VMEM_LIMIT = 56 * 1024 * 1024


def _silu(x):
    return x * (1.0 / (1.0 + jnp.exp(-x)))


def _softplus(x):
    return jnp.maximum(x, 0.0) + jnp.log1p(jnp.exp(-jnp.abs(x)))


def _gelu_tanh(x):
    return x * (0.5 * (1.0 + jnp.tanh(0.7978845608028654 * (x + 0.044715 * (x * x * x)))))


def _rms(x, g):
    return x * lax.rsqrt(jnp.mean(x * x, axis=-1, keepdims=True) + EPS) * g


def _norm_modulate(x3, g, shift, scale):
    r, tl, d = x3.shape
    hn = _rms(x3.reshape(r * tl, d), g)
    h3 = hn.reshape(r, tl, d) * (1.0 + scale) + shift
    return h3.reshape(r * tl, d).astype(BF16)


def _const_spec(shape):
    return pl.BlockSpec(shape, lambda *_: (0,) * len(shape), pipeline_mode=pl.Buffered(1))


def _params(*sem):
    return pltpu.CompilerParams(dimension_semantics=sem, vmem_limit_bytes=VMEM_LIMIT)


def _ada_kernel(c_ref, w_ref, b_ref, o_ref):
    sc = _silu(c_ref[...]).astype(BF16)
    o_ref[...] = jnp.dot(sc, w_ref[...].astype(BF16), preferred_element_type=F32) + b_ref[...]


def _ada(c_all, ada_w, ada_b):
    nb = c_all.shape[0]
    out = pl.pallas_call(
        _ada_kernel,
        grid=(DEPTH, N_MOD),
        in_specs=[pl.BlockSpec((nb, D_MODEL), lambda i, k: (0, 0)),
                  pl.BlockSpec((None, D_MODEL, D_MODEL), lambda i, k: (i, 0, k)),
                  pl.BlockSpec((None, None, 1, D_MODEL), lambda i, k: (i, k, 0, 0))],
        out_specs=pl.BlockSpec((None, None, nb, D_MODEL), lambda i, k: (i, k, 0, 0)),
        out_shape=jax.ShapeDtypeStruct((DEPTH, N_MOD, nb, D_MODEL), F32),
        compiler_params=_params("parallel", "parallel"),
        name="ada_mod",
    )(c_all, ada_w, ada_b.reshape(DEPTH, N_MOD, 1, D_MODEL))
    return out.reshape(DEPTH, N_MOD, nb, 1, D_MODEL)


def _mod_spec(layer, half, r, row0):
    return pl.BlockSpec((None, 3, r, 1, D_MODEL), lambda i, j: (layer, half, row0 // r + i, 0, 0))


def _gm_kernel(x_ref, mod_ref, g_ref, win_ref, lng_ref, lnb_ref, wmix_ref, bexp_ref, wout_ref,
               xo_ref, v_ref, *, v_tail):
    r, tl, d = x_ref.shape
    tm = r * tl
    x3 = x_ref[...]
    h = _norm_modulate(x3, g_ref[...], mod_ref[0], mod_ref[1])
    z = _gelu_tanh(jnp.dot(h, win_ref[...], preferred_element_type=F32))
    u = z[:, :D_GM]
    vr = z[:, D_GM:]
    xc = vr - jnp.mean(vr, axis=-1, keepdims=True)
    v = xc * lax.rsqrt(jnp.mean(xc * xc, axis=-1, keepdims=True) + EPS) * lng_ref[...] + lnb_ref[...]
    vb = v.astype(BF16)
    chunks = []
    for c in range(tm // GM_CHUNK):
        vc = vb[c * GM_CHUNK:(c + 1) * GM_CHUNK]
        parts = [jnp.dot(wmix_ref[g], vc[:, g * GM_GROUP_W:(g + 1) * GM_GROUP_W], preferred_element_type=F32)
                 for g in range(GM_GROUPS)]
        chunks.append(jnp.concatenate(parts, axis=1) + bexp_ref[...])
    s = jnp.concatenate(chunks, axis=0) if len(chunks) > 1 else chunks[0]
    y = jnp.dot((u * s).astype(BF16), wout_ref[...], preferred_element_type=F32)
    xo_ref[...] = x3 + mod_ref[2] * y.reshape(r, tl, d)

    @pl.when(pl.program_id(1) == pl.num_programs(1) - 1)
    def _():
        v_ref[...] = v.reshape(r, tl, D_GM)[:, tl - v_tail:, :]


def _gm_layer(x, mod, layer, row0, r, tl, g, w_in, ln_g, ln_b, w_s, b_s, w_out):
    nb, l, d = x.shape
    q = min(l, GM_CHUNK)
    assert l % q == 0 and GM_CHUNK % q == 0 and (r * tl) % GM_CHUNK == 0 and tl % q == 0
    rep = GM_CHUNK // q
    tri = jnp.tril(jnp.ones((q, q), F32))
    wq = w_s[:, :q, :q] * tri
    wmix = jnp.einsum("ab,gts->gatbs", jnp.eye(rep, dtype=F32), wq).reshape(GM_GROUPS, GM_CHUNK, GM_CHUNK)
    bq = jnp.tile(b_s[:, :q].T, (rep, 1))
    bexp = jnp.repeat(bq, GM_GROUP_W, axis=1)
    v_tail = l - ((l - 1) // GM_CHUNK) * GM_CHUNK
    assert v_tail <= tl
    kern = functools.partial(_gm_kernel, v_tail=v_tail)
    xo, v = pl.pallas_call(
        kern,
        grid=(nb // r, l // tl),
        in_specs=[pl.BlockSpec((r, tl, d), lambda i, j: (i, j, 0)),
                  _mod_spec(layer, 0, r, row0),
                  _const_spec((1, d)),
                  _const_spec((d, 2 * D_GM)),
                  _const_spec((1, D_GM)),
                  _const_spec((1, D_GM)),
                  _const_spec((GM_GROUPS, GM_CHUNK, GM_CHUNK)),
                  _const_spec((GM_CHUNK, D_GM)),
                  _const_spec((D_GM, d))],
        out_specs=[pl.BlockSpec((r, tl, d), lambda i, j: (i, j, 0)),
                   pl.BlockSpec((r, v_tail, D_GM), lambda i, j: (i, 0, 0))],
        out_shape=[jax.ShapeDtypeStruct((nb, l, d), F32),
                   jax.ShapeDtypeStruct((nb, v_tail, D_GM), F32)],
        compiler_params=_params("parallel", "arbitrary"),
        name="gmlp_mixer",
    )(x, mod, g.reshape(1, d), w_in.astype(BF16), ln_g.reshape(1, D_GM), ln_b.reshape(1, D_GM),
      wmix.astype(BF16), bexp, w_out.astype(BF16))
    return xo, v


def _mlp_kernel(x_ref, mod_ref, g_ref, w1_ref, w2_ref, gf_ref, o_ref, *, final):
    r, tl, d = x_ref.shape
    x3 = x_ref[...]
    h = _norm_modulate(x3, g_ref[...], mod_ref[0], mod_ref[1])
    acc = jnp.zeros((r * tl, d), F32)
    for k in range(D_FF // FF_CHUNK):
        a = jnp.dot(h, w1_ref[:, k * FF_CHUNK:(k + 1) * FF_CHUNK], preferred_element_type=F32)
        a = jnp.square(jnp.maximum(a, 0.0)).astype(BF16)
        acc = acc + jnp.dot(a, w2_ref[k * FF_CHUNK:(k + 1) * FF_CHUNK, :], preferred_element_type=F32)
    xo = x3 + mod_ref[2] * acc.reshape(r, tl, d)
    if final:
        xo = _rms(xo.reshape(r * tl, d), gf_ref[...]).reshape(r, tl, d)
    o_ref[...] = xo


def _mlp_layer(x, mod, layer, row0, r, tl, g, w1, w2, final_g, final):
    nb, l, d = x.shape
    kern = functools.partial(_mlp_kernel, final=final)
    return pl.pallas_call(
        kern,
        grid=(nb // r, l // tl),
        in_specs=[pl.BlockSpec((r, tl, d), lambda i, j: (i, j, 0)),
                  _mod_spec(layer, 1, r, row0),
                  _const_spec((1, d)),
                  _const_spec((d, D_FF)),
                  _const_spec((D_FF, d)),
                  _const_spec((1, d))],
        out_specs=pl.BlockSpec((r, tl, d), lambda i, j: (i, j, 0)),
        out_shape=jax.ShapeDtypeStruct((nb, l, d), F32),
        compiler_params=_params("parallel", "parallel"),
        name="relu2_mlp",
    )(x, mod, g.reshape(1, d), w1.astype(BF16), w2.astype(BF16), final_g.reshape(1, d))


def _ssm_in_kernel(x_ref, mod_ref, g_ref, wzx_ref, wdt_ref, cw_ref, cb_ref, dtb_ref, cs_ref,
                   z_ref, xbc_ref, dt_ref, cnew_ref, xp_ref):
    r, tl, d = x_ref.shape
    tm = r * tl

    @pl.when(pl.program_id(1) == 0)
    def _():
        xp_ref[:, 0:CONV_PAD, :] = cs_ref[...]

    h = _norm_modulate(x_ref[...], g_ref[...], mod_ref[0], mod_ref[1])
    z_ref[...] = jnp.dot(h, wzx_ref[:, :D_INNER], preferred_element_type=F32).reshape(r, tl, D_INNER)
    xbc = jnp.dot(h, wzx_ref[:, D_INNER:], preferred_element_type=F32)
    dt_raw = jnp.dot(h, wdt_ref[...], preferred_element_type=F32)
    dt_ref[...] = _softplus(dt_raw + dtb_ref[...]).reshape(r, tl, SSM_HEADS)
    xp_ref[:, CONV_PAD:CONV_PAD + tl, :] = xbc.reshape(r, tl, CONV_DIM)
    conv = cb_ref[...].reshape(1, 1, CONV_DIM)
    for k in range(CONV_W):
        off = CONV_PAD - (CONV_W - 1) + k
        conv = conv + xp_ref[:, off:off + tl, :] * cw_ref[k:k + 1, :].reshape(1, 1, CONV_DIM)
    xbc_ref[...] = _silu(conv)
    tail = xp_ref[:, tl:tl + CONV_PAD, :]
    cnew_ref[...] = tail
    xp_ref[:, 0:CONV_PAD, :] = tail


def _ssm_in(x, mod, layer, row0, r, tl, g, w_in, conv_w, conv_b, dt_bias, conv_state8):
    nb, l, d = x.shape
    w_bf = w_in.astype(BF16)
    return pl.pallas_call(
        _ssm_in_kernel,
        grid=(nb // r, l // tl),
        in_specs=[pl.BlockSpec((r, tl, d), lambda i, j: (i, j, 0)),
                  _mod_spec(layer, 0, r, row0),
                  _const_spec((1, d)),
                  _const_spec((d, D_ZX)),
                  _const_spec((d, SSM_HEADS)),
                  _const_spec((CONV_W, CONV_DIM)),
                  _const_spec((1, CONV_DIM)),
                  _const_spec((1, SSM_HEADS)),
                  pl.BlockSpec((r, CONV_PAD, CONV_DIM), lambda i, j: (i, 0, 0))],
        out_specs=[pl.BlockSpec((r, tl, D_INNER), lambda i, j: (i, j, 0)),
                   pl.BlockSpec((r, tl, CONV_DIM), lambda i, j: (i, j, 0)),
                   pl.BlockSpec((r, tl, SSM_HEADS), lambda i, j: (i, j, 0)),
                   pl.BlockSpec((r, CONV_PAD, CONV_DIM), lambda i, j: (i, 0, 0))],
        out_shape=[jax.ShapeDtypeStruct((nb, l, D_INNER), F32),
                   jax.ShapeDtypeStruct((nb, l, CONV_DIM), F32),
                   jax.ShapeDtypeStruct((nb, l, SSM_HEADS), F32),
                   jax.ShapeDtypeStruct((nb, CONV_PAD, CONV_DIM), F32)],
        scratch_shapes=[pltpu.VMEM((r, CONV_PAD + tl, CONV_DIM), F32)],
        compiler_params=_params("parallel", "arbitrary"),
        name="ssm_in_conv",
    )(x, mod, g.reshape(1, d), w_bf[:, :D_ZX], w_bf[:, D_ZX:], conv_w, conv_b.reshape(1, CONV_DIM),
      dt_bias.reshape(1, SSM_HEADS), conv_state8)


NT_DIMS = (((1,), (1,)), ((), ()))
TN_DIMS = (((0,), (0,)), ((), ()))


def _expand_heads(vals, e_bf):
    tm = vals[0].shape[0]
    v = jnp.concatenate(vals, axis=0)
    hi = v.astype(BF16)
    lo = (v - hi.astype(F32)).astype(BF16)
    x = jnp.dot(hi, e_bf, preferred_element_type=F32) + jnp.dot(lo, e_bf, preferred_element_type=F32)
    return [x[k * tm:(k + 1) * tm] for k in range(len(vals))]


def _head_masks():
    lane = lax.broadcasted_iota(jnp.int32, (1, SSM_GROUP_W), 1) // SSM_HEAD_DIM
    return [jnp.where(lane == hr, 1.0, 0.0).astype(BF16) for hr in range(HEADS_PER_GROUP)]


def _ssd_intra(g, cbm, acum, acum_t, xdt_g, hmask):
    acc = None
    for hr in range(HEADS_PER_GROUP):
        hh = g * HEADS_PER_GROUP + hr
        seg = jnp.minimum(acum[:, hh:hh + 1] - acum_t[hh:hh + 1, :], 0.0)
        lmat = (cbm * jnp.exp(seg)).astype(BF16)
        part = jnp.dot(lmat, xdt_g * hmask[hr], preferred_element_type=F32)
        acc = part if acc is None else acc + part
    return acc


def _cumsum_rows(lcum3, da):
    hi = da.astype(BF16)
    rest = da - hi.astype(F32)
    mid = rest.astype(BF16)
    lo = (rest - mid.astype(F32)).astype(BF16)
    return jnp.dot(lcum3, jnp.concatenate([hi, mid, lo], axis=0), preferred_element_type=F32)


def _gate_norm(y_g, z_g, ng_g):
    return _rms(y_g * _silu(z_g), ng_g)


def _ssm_prompt_kernel(x_ref, xn_ref, mod_ref, modn_ref, g_ref, wzx_ref, wdt_ref, cw_ref, cb_ref, dtb_ref, alog_ref,
                       dexp_ref, lcum_ref, e_ref, ng_ref, wout_ref, xo_ref, cnew_ref, s_ref,
                       xpa_ref, xpb_ref, za_ref, zb_ref, dta_ref, dtb2_ref, xca_ref, xcb_ref, yna_ref, ynb_ref, st_ref):
    _, tl, _ = xn_ref.shape
    j = pl.program_id(1)
    buf_a = (xpa_ref, za_ref, dta_ref, xca_ref, yna_ref)
    buf_b = (xpb_ref, zb_ref, dtb2_ref, xcb_ref, ynb_ref)

    def in_project(x3, m_ref, buf):
        xp_ref, z_ref, dt_ref = buf[:3]
        h = _norm_modulate(x3, g_ref[...], m_ref[0], m_ref[1])
        z_ref[...] = jnp.dot(h, wzx_ref[:, :D_INNER], preferred_element_type=F32)
        xp_ref[CONV_PAD:CONV_PAD + tl, :] = jnp.dot(h, wzx_ref[:, D_INNER:], preferred_element_type=F32)
        dt_ref[...] = _softplus(jnp.dot(h, wdt_ref[...], preferred_element_type=F32) + dtb_ref[...])

    def consume(buf, next_xp_ref, r_out):
        xp_ref, z_ref, dt_ref, xc_ref, yn_ref = buf
        sub = lax.broadcasted_iota(jnp.int32, (1, SUBLANES, 1), 1)
        for lb in range(CONV_DIM // LANE_TILE):
            lns = slice(lb * LANE_TILE, (lb + 1) * LANE_TILE)
            taps = [cw_ref[k:k + 1, lns].reshape(1, 1, LANE_TILE) for k in range(CONV_W)]
            bias = cb_ref[:, lns].reshape(1, 1, LANE_TILE)
            for rb in range(tl // ROW_TILE):
                r0 = rb * ROW_TILE
                cur = xp_ref[CONV_PAD + r0:CONV_PAD + r0 + ROW_TILE, lns]
                prv = xp_ref[r0:r0 + ROW_TILE, lns]
                cur = cur.reshape(ROW_TILE // SUBLANES, SUBLANES, LANE_TILE)
                prv = prv.reshape(ROW_TILE // SUBLANES, SUBLANES, LANE_TILE)
                conv = bias + cur * taps[CONV_W - 1]
                for s in range(1, CONV_W):
                    conv = conv + pltpu.roll(jnp.where(sub < SUBLANES - s, cur, prv), s, axis=1) * taps[CONV_W - 1 - s]
                xc_ref[r0:r0 + ROW_TILE, lns] = _silu(conv).reshape(ROW_TILE, LANE_TILE)
        tail = xp_ref[tl:tl + CONV_PAD, :]
        cnew_ref[0] = tail
        next_xp_ref[0:CONV_PAD, :] = tail

        a_neg = -jnp.exp(alog_ref[...])
        lcum3 = lcum_ref[...]
        causal = lcum3[:, :SSM_CHUNK].astype(F32) > 0.5
        hmask = _head_masks()
        for c in range(tl // SSM_CHUNK):
            rows = slice(c * SSM_CHUNK, (c + 1) * SSM_CHUNK)
            dt = dt_ref[rows, :]
            acum = _cumsum_rows(lcum3, dt * a_neg)
            alast = acum[SSM_CHUNK - 1:SSM_CHUNK, :]
            acum2 = acum * LOG2E
            acum2_t = acum2.T
            dt_t = dt.T
            heads = jnp.concatenate([jnp.exp(acum), dt * jnp.exp(alast - acum)], axis=0)
            heads_hi = heads.astype(BF16)
            heads_lo = (heads - heads_hi.astype(F32)).astype(BF16)
            ex = jnp.dot(jnp.concatenate([heads_hi, heads_lo], axis=1), e_ref[...], preferred_element_type=F32)
            for g in range(SSM_GROUPS):
                ch = slice(g * SSM_GROUP_W, (g + 1) * SSM_GROUP_W)
                b_lo = D_INNER + g * D_STATE
                c_lo = D_INNER + (SSM_GROUPS + g) * D_STATE
                ea_x = ex[:SSM_CHUNK, ch]
                w_x = ex[SSM_CHUNK:, ch]
                xs = xc_ref[rows, ch]
                xs_bf = xs.astype(BF16)
                bgt = xc_ref[rows, b_lo:b_lo + D_STATE].T.astype(BF16)
                cg = xc_ref[rows, c_lo:c_lo + D_STATE].astype(BF16)
                cbm = jnp.where(causal, jnp.dot(cg, bgt, preferred_element_type=F32), 0.0)
                lmats = []
                for hr in range(HEADS_PER_GROUP):
                    hh = g * HEADS_PER_GROUP + hr
                    seg = jnp.minimum(acum2[:, hh:hh + 1] - acum2_t[hh:hh + 1, :], 0.0)
                    lmats.append((cbm * jnp.exp2(seg) * dt_t[hh:hh + 1, :]).astype(BF16))
                x_stack = jnp.concatenate([xs_bf * hmask[hr] for hr in range(HEADS_PER_GROUP)], axis=0)
                y_g = xs * dexp_ref[:, ch] + jnp.dot(jnp.concatenate(lmats, axis=1), x_stack,
                                                     preferred_element_type=F32)
                s_old = st_ref[:, ch]
                y_g = y_g + jnp.dot(cg, s_old.astype(BF16), preferred_element_type=F32) * ea_x
                st_ref[:, ch] = (s_old * ea_x[SSM_CHUNK - 1:SSM_CHUNK, :]
                                 + jnp.dot(bgt, (xs * w_x).astype(BF16), preferred_element_type=F32))
                yn_ref[rows, ch] = _gate_norm(y_g, z_ref[rows, ch], ng_ref[:, ch]).astype(BF16)
        out = jnp.dot(yn_ref[...], wout_ref[...], preferred_element_type=F32)
        xo_ref[0, r_out:r_out + tl, :] = x_ref[0, r_out:r_out + tl, :] + mod_ref[2, 0] * out

    @pl.when((pl.program_id(0) == 0) & (j == 0))
    def _():
        in_project(x_ref[:, 0:tl, :], mod_ref, buf_a)

    @pl.when(j == 0)
    def _():
        xpa_ref[0:CONV_PAD, :] = jnp.zeros((CONV_PAD, CONV_DIM), F32)
        st_ref[...] = jnp.zeros_like(st_ref)

    in_project(x_ref[:, tl:2 * tl, :], mod_ref, buf_b)
    consume(buf_a, xpb_ref, 0)
    in_project(xn_ref[...], modn_ref, buf_a)
    consume(buf_b, xpa_ref, tl)

    @pl.when(j == pl.num_programs(1) - 1)
    def _():
        s_ref[0] = st_ref[...].T


def _ssd_sample_kernel(xbc_ref, dt_ref, z_ref, alog_ref, dexp_ref, lcum_ref, bd_ref, e_ref, ng_ref, s0_ref,
                       yn_ref, s_ref):
    r, tl, _ = xbc_ref.shape
    tm = r * tl
    xbc = xbc_ref[...].reshape(tm, CONV_DIM)
    xs = xbc[:, :D_INNER]
    dt = dt_ref[...].reshape(tm, SSM_HEADS)
    da = dt * (-jnp.exp(alog_ref[...]))
    lcum = lcum_ref[...]
    causal = lcum > 0.5
    acum = jnp.dot(lcum, da, precision=HIGHEST, preferred_element_type=F32)
    alast = jnp.dot(bd_ref[...], da, precision=HIGHEST, preferred_element_type=F32)
    acum_t = acum.T
    dec_end_t = jnp.exp(alast.T)
    dt_x, ea_x, w_x = _expand_heads([dt, jnp.exp(acum), dt * jnp.exp(alast - acum)], e_ref[...])
    xdt = (xs * dt_x).astype(BF16)
    xw = xs * w_x
    hmask = _head_masks()
    row = lax.broadcasted_iota(jnp.int32, (tm, 1), 0) // tl
    dec_cols = [jnp.broadcast_to(dec_end_t[:, b * tl:b * tl + 1], (SSM_HEADS, D_STATE)) for b in range(r)]
    for g in range(SSM_GROUPS):
        ch = slice(g * SSM_GROUP_W, (g + 1) * SSM_GROUP_W)
        bg = xbc[:, D_INNER + g * D_STATE:D_INNER + (g + 1) * D_STATE].astype(BF16)
        cg32 = xbc[:, D_INNER + (SSM_GROUPS + g) * D_STATE:D_INNER + (SSM_GROUPS + g + 1) * D_STATE]
        cbm = jnp.where(causal, lax.dot_general(cg32.astype(BF16), bg, NT_DIMS, preferred_element_type=F32), 0.0)
        y_g = xs[:, ch] * dexp_ref[:, ch] + _ssd_intra(g, cbm, acum, acum_t, xdt[:, ch], hmask)
        y_off = jnp.zeros((tm, SSM_GROUP_W), F32)
        xw_g = xw[:, ch]
        for b in range(r):
            s_old = s0_ref[b, ch, :]
            c_b = jnp.where(row == b, cg32, 0.0).astype(BF16)
            xw_b = jnp.where(row == b, xw_g, 0.0).astype(BF16)
            y_off = y_off + lax.dot_general(c_b, s_old.astype(BF16), NT_DIMS, preferred_element_type=F32)
            upd = lax.dot_general(xw_b, bg, TN_DIMS, preferred_element_type=F32)
            dec = jnp.concatenate(
                [jnp.broadcast_to(dec_cols[b][g * HEADS_PER_GROUP + hr:g * HEADS_PER_GROUP + hr + 1, :],
                                  (SSM_HEAD_DIM, D_STATE)) for hr in range(HEADS_PER_GROUP)], axis=0)
            s_ref[b, ch, :] = s_old * dec + upd
        z_g = z_ref[:, :, ch].reshape(tm, SSM_GROUP_W)
        yn_g = _gate_norm(y_g + y_off * ea_x[:, ch], z_g, ng_ref[:, ch])
        yn_ref[:, :, ch] = yn_g.reshape(r, tl, SSM_GROUP_W).astype(BF16)


def _ssd_consts(d_skip):
    e_bf = jnp.repeat(jnp.eye(SSM_HEADS, dtype=F32), SSM_HEAD_DIM, axis=1).astype(BF16)
    dexp = jnp.repeat(d_skip, SSM_HEAD_DIM).reshape(1, D_INNER)
    return e_bf, dexp


def _ssm_prompt(x, mod, layer, row0, tl, g, w_in, conv_w, conv_b, dt_bias, a_log, d_skip, norm_g, w_out):
    nb, l, d = x.shape
    assert l % (2 * tl) == 0 and tl % SSM_CHUNK == 0
    w_bf = w_in.astype(BF16)
    e_bf, dexp = _ssd_consts(d_skip)
    nj = l // (2 * tl)

    def next_step(i, j):
        flat = jnp.minimum(i * nj + j + 1, nb * nj - 1)
        return flat // nj, flat % nj

    def next_first_tile(i, j):
        ni, njj = next_step(i, j)
        return ni, 2 * njj, 0

    return pl.pallas_call(
        _ssm_prompt_kernel,
        grid=(nb, nj),
        in_specs=[pl.BlockSpec((1, 2 * tl, d), lambda i, j: (i, j, 0)),
                  pl.BlockSpec((1, tl, d), next_first_tile),
                  _mod_spec(layer, 0, 1, row0),
                  pl.BlockSpec((None, 3, 1, 1, d), lambda i, j: (layer, 0, row0 + next_step(i, j)[0], 0, 0)),
                  _const_spec((1, d)),
                  _const_spec((d, D_ZX)),
                  _const_spec((d, SSM_HEADS)),
                  _const_spec((CONV_W, CONV_DIM)),
                  _const_spec((1, CONV_DIM)),
                  _const_spec((1, SSM_HEADS)),
                  _const_spec((1, SSM_HEADS)),
                  _const_spec((1, D_INNER)),
                  _const_spec((SSM_CHUNK, 3 * SSM_CHUNK)),
                  _const_spec((2 * SSM_HEADS, D_INNER)),
                  _const_spec((1, D_INNER)),
                  _const_spec((D_INNER, d))],
        out_specs=[pl.BlockSpec((1, 2 * tl, d), lambda i, j: (i, j, 0)),
                   pl.BlockSpec((1, CONV_PAD, CONV_DIM), lambda i, j: (i, 0, 0)),
                   pl.BlockSpec((1, D_INNER, D_STATE), lambda i, j: (i, 0, 0))],
        out_shape=[jax.ShapeDtypeStruct((nb, l, d), F32),
                   jax.ShapeDtypeStruct((nb, CONV_PAD, CONV_DIM), F32),
                   jax.ShapeDtypeStruct((nb, D_INNER, D_STATE), F32)],
        scratch_shapes=[pltpu.VMEM((CONV_PAD + tl, CONV_DIM), F32),
                        pltpu.VMEM((CONV_PAD + tl, CONV_DIM), F32),
                        pltpu.VMEM((tl, D_INNER), F32),
                        pltpu.VMEM((tl, D_INNER), F32),
                        pltpu.VMEM((tl, SSM_HEADS), F32),
                        pltpu.VMEM((tl, SSM_HEADS), F32),
                        pltpu.VMEM((tl, CONV_DIM), F32),
                        pltpu.VMEM((tl, CONV_DIM), F32),
                        pltpu.VMEM((tl, D_INNER), BF16),
                        pltpu.VMEM((tl, D_INNER), BF16),
                        pltpu.VMEM((D_STATE, D_INNER), F32)],
        compiler_params=_params("arbitrary", "arbitrary"),
        name="ssm_prompt",
    )(x, x, mod, mod, g.reshape(1, d), w_bf[:, :D_ZX], w_bf[:, D_ZX:], conv_w, conv_b.reshape(1, CONV_DIM),
      dt_bias.reshape(1, SSM_HEADS), a_log.reshape(1, SSM_HEADS), dexp,
      jnp.tile(jnp.tril(jnp.ones((SSM_CHUNK, SSM_CHUNK), BF16)), (1, 3)), jnp.concatenate([e_bf, e_bf], axis=0),
      norm_g.reshape(1, D_INNER), w_out.astype(BF16))


def _ssd_sample(xbc, dt, z, a_log, d_skip, norm_g, state0, r):
    nb, l, _ = xbc.shape
    tm = r * l
    assert l <= SSM_CHUNK and nb % r == 0
    eye_r = jnp.eye(r, dtype=F32)
    lcum = jnp.kron(eye_r, jnp.tril(jnp.ones((l, l), F32)))
    bd = jnp.kron(eye_r, jnp.ones((l, l), F32))
    e_bf, dexp = _ssd_consts(d_skip)
    return pl.pallas_call(
        _ssd_sample_kernel,
        grid=(nb // r,),
        in_specs=[pl.BlockSpec((r, l, CONV_DIM), lambda i: (i, 0, 0)),
                  pl.BlockSpec((r, l, SSM_HEADS), lambda i: (i, 0, 0)),
                  pl.BlockSpec((r, l, D_INNER), lambda i: (i, 0, 0)),
                  _const_spec((1, SSM_HEADS)),
                  _const_spec((1, D_INNER)),
                  _const_spec((tm, tm)),
                  _const_spec((tm, tm)),
                  _const_spec((SSM_HEADS, D_INNER)),
                  _const_spec((1, D_INNER)),
                  pl.BlockSpec((r, D_INNER, D_STATE), lambda i: (i, 0, 0))],
        out_specs=[pl.BlockSpec((r, l, D_INNER), lambda i: (i, 0, 0)),
                   pl.BlockSpec((r, D_INNER, D_STATE), lambda i: (i, 0, 0))],
        out_shape=[jax.ShapeDtypeStruct((nb, l, D_INNER), BF16),
                   jax.ShapeDtypeStruct((nb, D_INNER, D_STATE), F32)],
        compiler_params=_params("parallel"),
        name="ssd_scan_sample",
    )(xbc, dt, z, a_log.reshape(1, SSM_HEADS), dexp, lcum, bd, e_bf, norm_g.reshape(1, D_INNER), state0)


def _ssm_out_kernel(x_ref, yn_ref, mod_ref, wout_ref, o_ref):
    r, tl, d = x_ref.shape
    if r == 1:
        yn = yn_ref[0]
    else:
        yn = yn_ref[...].astype(F32).reshape(r * tl, D_INNER).astype(BF16)
    out = jnp.dot(yn, wout_ref[...], preferred_element_type=F32)
    o_ref[...] = x_ref[...] + mod_ref[2] * out.reshape(r, tl, d)


def _ssm_out(x, yn, mod, layer, row0, r, tl, w_out):
    nb, l, d = x.shape
    return pl.pallas_call(
        _ssm_out_kernel,
        grid=(nb // r, l // tl),
        in_specs=[pl.BlockSpec((r, tl, d), lambda i, j: (i, j, 0)),
                  pl.BlockSpec((r, tl, D_INNER), lambda i, j: (i, j, 0)),
                  _mod_spec(layer, 0, r, row0),
                  _const_spec((D_INNER, d))],
        out_specs=pl.BlockSpec((r, tl, d), lambda i, j: (i, j, 0)),
        out_shape=jax.ShapeDtypeStruct((nb, l, d), F32),
        compiler_params=_params("parallel", "parallel"),
        name="ssm_out_proj",
    )(x, yn, mod, w_out.astype(BF16))


def _trunk(x, mod, row0, tiles, ssm_state, conv_state, p):
    nb, l, _ = x.shape
    new_v, new_ssm, new_conv = [], [], []
    for i in range(DEPTH):
        j = i // 2
        if i % 2 == 0:
            x, v = _gm_layer(x, mod, i, row0, *tiles["gm"], p["norm1_g"][i], p["gm_w_in"][j], p["gm_ln_g"][j],
                             p["gm_ln_b"][j], p["gm_w_s"][j], p["gm_b_s"][j], p["gm_w_out"][j])
            new_v.append(v)
        else:
            if ssm_state is None:
                x, cnew, s_new = _ssm_prompt(x, mod, i, row0, tiles["ssm"], p["norm1_g"][i], p["ssm_w_in"][j],
                                             p["ssm_conv_w"][j], p["ssm_conv_b"][j], p["ssm_dt_bias"][j],
                                             p["ssm_a_log"][j], p["ssm_d"][j], p["ssm_norm_g"][j],
                                             p["ssm_w_out"][j])
            else:
                cs8 = jnp.pad(conv_state[j], ((0, 0), (CONV_PAD - (CONV_W - 1), 0), (0, 0)))
                z, xbc, dt, cnew = _ssm_in(x, mod, i, row0, *tiles["ssm_in"], p["norm1_g"][i], p["ssm_w_in"][j],
                                           p["ssm_conv_w"][j], p["ssm_conv_b"][j], p["ssm_dt_bias"][j], cs8)
                s0 = ssm_state[j].reshape(nb, D_INNER, D_STATE)
                yn, s_new = _ssd_sample(xbc, dt, z, p["ssm_a_log"][j], p["ssm_d"][j], p["ssm_norm_g"][j], s0,
                                        tiles["ssd"])
                x = _ssm_out(x, yn, mod, i, row0, *tiles["ssm_out"], p["ssm_w_out"][j])
            new_conv.append(cnew[:, CONV_PAD - (CONV_W - 1):, :])
            new_ssm.append(s_new.reshape(nb, SSM_HEADS, SSM_HEAD_DIM, D_STATE))
        x = _mlp_layer(x, mod, i, row0, *tiles["mlp"], p["norm2_g"][i], p["mlp_w1"][i], p["mlp_w2"][i],
                       p["final_g"], final=(i == DEPTH - 1))
    return x, jnp.stack(new_v), jnp.stack(new_ssm), jnp.stack(new_conv)


PROMPT_TILES = {"gm": (1, 256), "mlp": (1, 512), "ssm": 256}
SAMPLE_TILES = {"gm": (32, 8), "mlp": (64, 8), "ssm_in": (32, 8), "ssd": 8, "ssm_out": (64, 8)}


def kernel(x_prompt, x_sample, c_prompt, c_sample, state_ssm, state_conv, ada_w, ada_b, norm1_g, norm2_g, gm_w_in, gm_ln_g, gm_ln_b, gm_w_s, gm_b_s, gm_w_out, ssm_w_in, ssm_conv_w, ssm_conv_b, ssm_dt_bias, ssm_a_log, ssm_d, ssm_norm_g, ssm_w_out, mlp_w1, mlp_w2, final_g):
    p = dict(norm1_g=norm1_g, norm2_g=norm2_g, gm_w_in=gm_w_in, gm_ln_g=gm_ln_g, gm_ln_b=gm_ln_b, gm_w_s=gm_w_s,
             gm_b_s=gm_b_s, gm_w_out=gm_w_out, ssm_w_in=ssm_w_in, ssm_conv_w=ssm_conv_w, ssm_conv_b=ssm_conv_b,
             ssm_dt_bias=ssm_dt_bias, ssm_a_log=ssm_a_log, ssm_d=ssm_d, ssm_norm_g=ssm_norm_g, ssm_w_out=ssm_w_out,
             mlp_w1=mlp_w1, mlp_w2=mlp_w2, final_g=final_g)
    n_sample = x_sample.shape[0]
    mod = _ada(jnp.concatenate([c_sample, c_prompt], axis=0), ada_w, ada_b)
    y_p, v_p, ssm_p, conv_p = _trunk(x_prompt, mod, n_sample, PROMPT_TILES, None, None, p)
    y_s, v_s, ssm_s, conv_s = _trunk(x_sample, mod, 0, SAMPLE_TILES, state_ssm, state_conv, p)
    return (y_p, y_s, v_p, v_s, ssm_p, conv_p, ssm_s, conv_s)
```

```python
import functools

import jax
import jax.numpy as jnp
from jax import lax
from jax.experimental import pallas as pl
from jax.experimental.pallas import tpu as pltpu

F32 = jnp.float32
BF16 = jnp.bfloat16
HIGHEST = lax.Precision.HIGHEST

D_MODEL = 1024
DEPTH = 2
N_MOD = 6
EPS = 1e-6
LOG2E = 1.4426950408889634
GM_CHUNK = 128
D_GM = 2 * D_MODEL
GM_GROUPS = 8
GM_GROUP_W = D_GM // GM_GROUPS
D_INNER = 2 * D_MODEL
SSM_HEAD_DIM = 64
SSM_HEADS = D_INNER // SSM_HEAD_DIM
SSM_GROUPS = 8
HEADS_PER_GROUP = SSM_HEADS // SSM_GROUPS
SSM_GROUP_W = D_INNER // SSM_GROUPS
D_STATE = 128
CONV_W = 4
CONV_DIM = D_INNER + 2 * SSM_GROUPS * D_STATE
D_ZX = D_INNER + CONV_DIM
SSM_CHUNK = 128
D_FF = 4 * D_MODEL
FF_CHUNK = 1024
SUBLANES = 8
CONV_PAD = SUBLANES
ROW_TILE = 256
LANE_TILE = 512
VMEM_LIMIT = 56 * 1024 * 1024


def _silu(x):
    return x * (1.0 / (1.0 + jnp.exp2(x * (-LOG2E))))


def _softplus(x):
    return jnp.maximum(x, 0.0) + jnp.log1p(jnp.exp(-jnp.abs(x)))


def _gelu_tanh(x):
    return x * (0.5 * (1.0 + jnp.tanh(0.7978845608028654 * (x + 0.044715 * (x * x * x)))))


def _rms(x, g):
    return x * lax.rsqrt(jnp.mean(x * x, axis=-1, keepdims=True) + EPS) * g


def _norm_modulate(x3, g, shift, scale):
    r, tl, d = x3.shape
    hn = _rms(x3.reshape(r * tl, d), g)
    h3 = hn.reshape(r, tl, d) * (1.0 + scale) + shift
    return h3.reshape(r * tl, d).astype(BF16)


def _const_spec(shape):
    return pl.BlockSpec(shape, lambda *_: (0,) * len(shape), pipeline_mode=pl.Buffered(1))


def _params(*sem):
    return pltpu.CompilerParams(dimension_semantics=sem, vmem_limit_bytes=VMEM_LIMIT)


def _ada_kernel(c_ref, w_ref, b_ref, o_ref):
    sc = _silu(c_ref[...]).astype(BF16)
    o_ref[...] = jnp.dot(sc, w_ref[...].astype(BF16), preferred_element_type=F32) + b_ref[...]


def _ada(c_all, ada_w, ada_b):
    nb = c_all.shape[0]
    out = pl.pallas_call(
        _ada_kernel,
        grid=(DEPTH, N_MOD),
        in_specs=[pl.BlockSpec((nb, D_MODEL), lambda i, k: (0, 0)),
                  pl.BlockSpec((None, D_MODEL, D_MODEL), lambda i, k: (i, 0, k)),
                  pl.BlockSpec((None, None, 1, D_MODEL), lambda i, k: (i, k, 0, 0))],
        out_specs=pl.BlockSpec((None, None, nb, D_MODEL), lambda i, k: (i, k, 0, 0)),
        out_shape=jax.ShapeDtypeStruct((DEPTH, N_MOD, nb, D_MODEL), F32),
        compiler_params=_params("parallel", "parallel"),
        name="ada_mod",
    )(c_all, ada_w, ada_b.reshape(DEPTH, N_MOD, 1, D_MODEL))
    return out.reshape(DEPTH, N_MOD, nb, 1, D_MODEL)


def _mod_spec(layer, half, r, row0):
    return pl.BlockSpec((None, 3, r, 1, D_MODEL), lambda i, j: (layer, half, row0 // r + i, 0, 0))


def _gm_kernel(x_ref, mod_ref, g_ref, win_ref, lng_ref, lnb_ref, wmix_ref, bexp_ref, wout_ref,
               xo_ref, v_ref, *, v_tail):
    r, tl, d = x_ref.shape
    tm = r * tl
    x3 = x_ref[...]
    h = _norm_modulate(x3, g_ref[...], mod_ref[0], mod_ref[1])
    z = _gelu_tanh(jnp.dot(h, win_ref[...], preferred_element_type=F32))
    u = z[:, :D_GM]
    vr = z[:, D_GM:]
    xc = vr - jnp.mean(vr, axis=-1, keepdims=True)
    v = xc * lax.rsqrt(jnp.mean(xc * xc, axis=-1, keepdims=True) + EPS) * lng_ref[...] + lnb_ref[...]
    vb = v.astype(BF16)
    chunks = []
    for c in range(tm // GM_CHUNK):
        vc = vb[c * GM_CHUNK:(c + 1) * GM_CHUNK]
        parts = [jnp.dot(wmix_ref[g], vc[:, g * GM_GROUP_W:(g + 1) * GM_GROUP_W], preferred_element_type=F32)
                 for g in range(GM_GROUPS)]
        chunks.append(jnp.concatenate(parts, axis=1) + bexp_ref[...])
    s = jnp.concatenate(chunks, axis=0) if len(chunks) > 1 else chunks[0]
    y = jnp.dot((u * s).astype(BF16), wout_ref[...], preferred_element_type=F32)
    xo_ref[...] = x3 + mod_ref[2] * y.reshape(r, tl, d)

    @pl.when(pl.program_id(1) == pl.num_programs(1) - 1)
    def _():
        v_ref[...] = v.reshape(r, tl, D_GM)[:, tl - v_tail:, :]


def _gm_layer(x, mod, layer, row0, r, tl, g, w_in, ln_g, ln_b, w_s, b_s, w_out):
    nb, l, d = x.shape
    q = min(l, GM_CHUNK)
    assert l % q == 0 and GM_CHUNK % q == 0 and (r * tl) % GM_CHUNK == 0 and tl % q == 0
    rep = GM_CHUNK // q
    tri = jnp.tril(jnp.ones((q, q), F32))
    wq = w_s[:, :q, :q] * tri
    wmix = jnp.einsum("ab,gts->gatbs", jnp.eye(rep, dtype=F32), wq).reshape(GM_GROUPS, GM_CHUNK, GM_CHUNK)
    bq = jnp.tile(b_s[:, :q].T, (rep, 1))
    bexp = jnp.repeat(bq, GM_GROUP_W, axis=1)
    v_tail = l - ((l - 1) // GM_CHUNK) * GM_CHUNK
    assert v_tail <= tl
    kern = functools.partial(_gm_kernel, v_tail=v_tail)
    xo, v = pl.pallas_call(
        kern,
        grid=(nb // r, l // tl),
        in_specs=[pl.BlockSpec((r, tl, d), lambda i, j: (i, j, 0)),
                  _mod_spec(layer, 0, r, row0),
                  _const_spec((1, d)),
                  _const_spec((d, 2 * D_GM)),
                  _const_spec((1, D_GM)),
                  _const_spec((1, D_GM)),
                  _const_spec((GM_GROUPS, GM_CHUNK, GM_CHUNK)),
                  _const_spec((GM_CHUNK, D_GM)),
                  _const_spec((D_GM, d))],
        out_specs=[pl.BlockSpec((r, tl, d), lambda i, j: (i, j, 0)),
                   pl.BlockSpec((r, v_tail, D_GM), lambda i, j: (i, 0, 0))],
        out_shape=[jax.ShapeDtypeStruct((nb, l, d), F32),
                   jax.ShapeDtypeStruct((nb, v_tail, D_GM), F32)],
        compiler_params=_params("parallel", "arbitrary"),
        name="gmlp_mixer",
    )(x, mod, g.reshape(1, d), w_in.astype(BF16), ln_g.reshape(1, D_GM), ln_b.reshape(1, D_GM),
      wmix.astype(BF16), bexp, w_out.astype(BF16))
    return xo, v


def _mlp_kernel(x_ref, mod_ref, g_ref, w1_ref, w2_ref, gf_ref, o_ref, *, final):
    r, tl, d = x_ref.shape
    x3 = x_ref[...]
    h = _norm_modulate(x3, g_ref[...], mod_ref[0], mod_ref[1])
    acc = jnp.zeros((r * tl, d), F32)
    for k in range(D_FF // FF_CHUNK):
        a = jnp.dot(h, w1_ref[:, k * FF_CHUNK:(k + 1) * FF_CHUNK], preferred_element_type=F32)
        a = jnp.square(jnp.maximum(a, 0.0)).astype(BF16)
        acc = acc + jnp.dot(a, w2_ref[k * FF_CHUNK:(k + 1) * FF_CHUNK, :], preferred_element_type=F32)
    xo = x3 + mod_ref[2] * acc.reshape(r, tl, d)
    if final:
        xo = _rms(xo.reshape(r * tl, d), gf_ref[...]).reshape(r, tl, d)
    o_ref[...] = xo


def _mlp_layer(x, mod, layer, row0, r, tl, g, w1, w2, final_g, final):
    nb, l, d = x.shape
    kern = functools.partial(_mlp_kernel, final=final)
    return pl.pallas_call(
        kern,
        grid=(nb // r, l // tl),
        in_specs=[pl.BlockSpec((r, tl, d), lambda i, j: (i, j, 0)),
                  _mod_spec(layer, 1, r, row0),
                  _const_spec((1, d)),
                  _const_spec((d, D_FF)),
                  _const_spec((D_FF, d)),
                  _const_spec((1, d))],
        out_specs=pl.BlockSpec((r, tl, d), lambda i, j: (i, j, 0)),
        out_shape=jax.ShapeDtypeStruct((nb, l, d), F32),
        compiler_params=_params("parallel", "parallel"),
        name="relu2_mlp",
    )(x, mod, g.reshape(1, d), w1.astype(BF16), w2.astype(BF16), final_g.reshape(1, d))


def _ssm_in_kernel(x_ref, mod_ref, g_ref, w_ref, cw_ref, cb_ref, dtb_ref, cs_ref,
                   z_ref, xbc_ref, dt_ref, cnew_ref, xp_ref):
    r, tl, d = x_ref.shape

    @pl.when(pl.program_id(1) == 0)
    def _():
        xp_ref[:, 0:CONV_PAD, :] = cs_ref[...]

    h = _norm_modulate(x_ref[...], g_ref[...], mod_ref[0], mod_ref[1])
    z_ref[...] = jnp.dot(h, w_ref[:, :D_INNER], preferred_element_type=F32).reshape(r, tl, D_INNER)
    xbc = jnp.dot(h, w_ref[:, D_INNER:D_ZX], preferred_element_type=F32)
    dt_raw = jnp.dot(h, w_ref[:, D_ZX:], preferred_element_type=F32)
    dt_ref[...] = _softplus(dt_raw + dtb_ref[...]).reshape(r, tl, SSM_HEADS)
    xp_ref[:, CONV_PAD:CONV_PAD + tl, :] = xbc.reshape(r, tl, CONV_DIM)
    conv = cb_ref[...].reshape(1, 1, CONV_DIM)
    for k in range(CONV_W):
        off = CONV_PAD - (CONV_W - 1) + k
        conv = conv + xp_ref[:, off:off + tl, :] * cw_ref[k:k + 1, :].reshape(1, 1, CONV_DIM)
    xbc_ref[...] = _silu(conv)
    tail = xp_ref[:, tl:tl + CONV_PAD, :]
    cnew_ref[...] = tail
    xp_ref[:, 0:CONV_PAD, :] = tail


def _ssm_in(x, mod, layer, row0, r, tl, g, w_in, conv_w, conv_b, dt_bias, conv_state8):
    nb, l, d = x.shape
    return pl.pallas_call(
        _ssm_in_kernel,
        grid=(nb // r, l // tl),
        in_specs=[pl.BlockSpec((r, tl, d), lambda i, j: (i, j, 0)),
                  _mod_spec(layer, 0, r, row0),
                  _const_spec((1, d)),
                  _const_spec((d, D_ZX + SSM_HEADS)),
                  _const_spec((CONV_W, CONV_DIM)),
                  _const_spec((1, CONV_DIM)),
                  _const_spec((1, SSM_HEADS)),
                  pl.BlockSpec((r, CONV_PAD, CONV_DIM), lambda i, j: (i, 0, 0))],
        out_specs=[pl.BlockSpec((r, tl, D_INNER), lambda i, j: (i, j, 0)),
                   pl.BlockSpec((r, tl, CONV_DIM), lambda i, j: (i, j, 0)),
                   pl.BlockSpec((r, tl, SSM_HEADS), lambda i, j: (i, j, 0)),
                   pl.BlockSpec((r, CONV_PAD, CONV_DIM), lambda i, j: (i, 0, 0))],
        out_shape=[jax.ShapeDtypeStruct((nb, l, D_INNER), F32),
                   jax.ShapeDtypeStruct((nb, l, CONV_DIM), F32),
                   jax.ShapeDtypeStruct((nb, l, SSM_HEADS), F32),
                   jax.ShapeDtypeStruct((nb, CONV_PAD, CONV_DIM), F32)],
        scratch_shapes=[pltpu.VMEM((r, CONV_PAD + tl, CONV_DIM), F32)],
        compiler_params=_params("parallel", "arbitrary"),
        name="ssm_in_conv",
    )(x, mod, g.reshape(1, d), w_in.astype(BF16), conv_w, conv_b.reshape(1, CONV_DIM),
      dt_bias.reshape(1, SSM_HEADS), conv_state8)


NT_DIMS = (((1,), (1,)), ((), ()))
TN_DIMS = (((0,), (0,)), ((), ()))


def _expand_heads(vals, e_bf):
    tm = vals[0].shape[0]
    v = jnp.concatenate(vals, axis=0)
    hi = v.astype(BF16)
    lo = (v - hi.astype(F32)).astype(BF16)
    x = jnp.dot(hi, e_bf, preferred_element_type=F32) + jnp.dot(lo, e_bf, preferred_element_type=F32)
    return [x[k * tm:(k + 1) * tm] for k in range(len(vals))]


def _head_masks():
    lane = lax.broadcasted_iota(jnp.int32, (1, SSM_GROUP_W), 1) // SSM_HEAD_DIM
    return [jnp.where(lane == hr, 1.0, 0.0).astype(BF16) for hr in range(HEADS_PER_GROUP)]


def _ssd_intra(g, cbm, acum, acum_t, xdt_g, hmask):
    acc = None
    for hr in range(HEADS_PER_GROUP):
        hh = g * HEADS_PER_GROUP + hr
        seg = jnp.minimum(acum[:, hh:hh + 1] - acum_t[hh:hh + 1, :], 0.0)
        lmat = (cbm * jnp.exp(seg)).astype(BF16)
        part = jnp.dot(lmat, xdt_g * hmask[hr], preferred_element_type=F32)
        acc = part if acc is None else acc + part
    return acc


def _cumsum_rows(lcum3, da):
    hi = da.astype(BF16)
    rest = da - hi.astype(F32)
    mid = rest.astype(BF16)
    lo = (rest - mid.astype(F32)).astype(BF16)
    return jnp.dot(lcum3, jnp.concatenate([hi, mid, lo], axis=0), preferred_element_type=F32)


def _gate_norm(y_g, z_g, ng_g):
    return _rms(y_g * _silu(z_g), ng_g)


def _ssm_prompt_kernel(x_ref, mod_ref, g_ref, w_ref, cw_ref, cb_ref, dtb_ref, alog_ref, dexp_ref, lcum_ref, e_ref,
                       ng_ref, wout_ref, xo_ref, cnew_ref, s_ref, xp_ref, z_ref, xc_ref, yn_ref, st_ref):
    _, tl, _ = x_ref.shape
    j = pl.program_id(1)

    @pl.when(j == 0)
    def _():
        xp_ref[0:CONV_PAD, :] = jnp.zeros((CONV_PAD, CONV_DIM), F32)
        st_ref[...] = jnp.zeros_like(st_ref)

    h = _norm_modulate(x_ref[...], g_ref[...], mod_ref[0], mod_ref[1])
    z_ref[...] = jnp.dot(h, w_ref[:, :D_INNER], preferred_element_type=F32)
    xp_ref[CONV_PAD:CONV_PAD + tl, :] = jnp.dot(h, w_ref[:, D_INNER:D_ZX], preferred_element_type=F32)
    dt_all = _softplus(jnp.dot(h, w_ref[:, D_ZX:], preferred_element_type=F32) + dtb_ref[...])

    sub = lax.broadcasted_iota(jnp.int32, (1, SUBLANES, 1), 1)
    for lb in range(CONV_DIM // LANE_TILE):
        lns = slice(lb * LANE_TILE, (lb + 1) * LANE_TILE)
        taps = [cw_ref[k:k + 1, lns].reshape(1, 1, LANE_TILE) for k in range(CONV_W)]
        bias = cb_ref[:, lns].reshape(1, 1, LANE_TILE)
        for rb in range(tl // ROW_TILE):
            r0 = rb * ROW_TILE
            cur = xp_ref[CONV_PAD + r0:CONV_PAD + r0 + ROW_TILE, lns]
            prv = xp_ref[r0:r0 + ROW_TILE, lns]
            cur = cur.reshape(ROW_TILE // SUBLANES, SUBLANES, LANE_TILE)
            prv = prv.reshape(ROW_TILE // SUBLANES, SUBLANES, LANE_TILE)
            conv = bias + cur * taps[CONV_W - 1]
            for s in range(1, CONV_W):
                conv = conv + pltpu.roll(jnp.where(sub < SUBLANES - s, cur, prv), s, axis=1) * taps[CONV_W - 1 - s]
            xc_ref[r0:r0 + ROW_TILE, lns] = _silu(conv).reshape(ROW_TILE, LANE_TILE)
    tail = xp_ref[tl:tl + CONV_PAD, :]
    cnew_ref[0] = tail
    xp_ref[0:CONV_PAD, :] = tail

    a_neg = -jnp.exp(alog_ref[...])
    lcum3 = lcum_ref[...]
    causal = lcum3[:, :SSM_CHUNK].astype(F32) > 0.5
    hmask = _head_masks()
    for c in range(tl // SSM_CHUNK):
        rows = slice(c * SSM_CHUNK, (c + 1) * SSM_CHUNK)
        dt = dt_all[rows]
        acum = _cumsum_rows(lcum3, dt * a_neg)
        alast = acum[SSM_CHUNK - 1:SSM_CHUNK, :]
        acum2 = acum * LOG2E
        acum2_t = acum2.T
        dt_t = dt.T
        heads = jnp.concatenate([jnp.exp(acum), dt * jnp.exp(alast - acum)], axis=0)
        heads_hi = heads.astype(BF16)
        heads_lo = (heads - heads_hi.astype(F32)).astype(BF16)
        ex = jnp.dot(jnp.concatenate([heads_hi, heads_lo], axis=1), e_ref[...], preferred_element_type=F32)
        for g in range(SSM_GROUPS):
            ch = slice(g * SSM_GROUP_W, (g + 1) * SSM_GROUP_W)
            b_lo = D_INNER + g * D_STATE
            c_lo = D_INNER + (SSM_GROUPS + g) * D_STATE
            ea_x = ex[:SSM_CHUNK, ch]
            w_x = ex[SSM_CHUNK:, ch]
            xs = xc_ref[rows, ch]
            xs_bf = xs.astype(BF16)
            bgt = xc_ref[rows, b_lo:b_lo + D_STATE].T.astype(BF16)
            cg = xc_ref[rows, c_lo:c_lo + D_STATE].astype(BF16)
            cbm = jnp.where(causal, jnp.dot(cg, bgt, preferred_element_type=F32), 0.0)
            lmats = []
            for hr in range(HEADS_PER_GROUP):
                hh = g * HEADS_PER_GROUP + hr
                seg = jnp.minimum(acum2[:, hh:hh + 1] - acum2_t[hh:hh + 1, :], 0.0)
                lmats.append((cbm * jnp.exp2(seg) * dt_t[hh:hh + 1, :]).astype(BF16))
            x_stack = jnp.concatenate([xs_bf * hmask[hr] for hr in range(HEADS_PER_GROUP)], axis=0)
            y_g = xs * dexp_ref[:, ch] + jnp.dot(jnp.concatenate(lmats, axis=1), x_stack,
                                                 preferred_element_type=F32)
            s_old = st_ref[:, ch]
            y_g = y_g + jnp.dot(cg, s_old.astype(BF16), preferred_element_type=F32) * ea_x
            st_ref[:, ch] = (s_old * ea_x[SSM_CHUNK - 1:SSM_CHUNK, :]
                             + jnp.dot(bgt, (xs * w_x).astype(BF16), preferred_element_type=F32))
            yn_ref[rows, ch] = _gate_norm(y_g, z_ref[rows, ch], ng_ref[:, ch]).astype(BF16)

    out = jnp.dot(yn_ref[...], wout_ref[...], preferred_element_type=F32)
    xo_ref[0] = x_ref[0] + mod_ref[2, 0] * out

    @pl.when(j == pl.num_programs(1) - 1)
    def _():
        s_ref[0] = st_ref[...].T


def _ssd_sample_kernel(xbc_ref, dt_ref, z_ref, alog_ref, dexp_ref, lcum_ref, bd_ref, e_ref, ng_ref, s0_ref,
                       yn_ref, s_ref):
    r, tl, _ = xbc_ref.shape
    tm = r * tl
    xbc = xbc_ref[...].reshape(tm, CONV_DIM)
    xs = xbc[:, :D_INNER]
    dt = dt_ref[...].reshape(tm, SSM_HEADS)
    da = dt * (-jnp.exp(alog_ref[...]))
    lcum = lcum_ref[...]
    causal = lcum > 0.5
    acum = jnp.dot(lcum, da, precision=HIGHEST, preferred_element_type=F32)
    alast = jnp.dot(bd_ref[...], da, precision=HIGHEST, preferred_element_type=F32)
    acum_t = acum.T
    dec_end_t = jnp.exp(alast.T)
    dt_x, ea_x, w_x = _expand_heads([dt, jnp.exp(acum), dt * jnp.exp(alast - acum)], e_ref[...])
    xdt = (xs * dt_x).astype(BF16)
    xw = xs * w_x
    hmask = _head_masks()
    row = lax.broadcasted_iota(jnp.int32, (tm, 1), 0) // tl
    dec_cols = [jnp.broadcast_to(dec_end_t[:, b * tl:b * tl + 1], (SSM_HEADS, D_STATE)) for b in range(r)]
    for g in range(SSM_GROUPS):
        ch = slice(g * SSM_GROUP_W, (g + 1) * SSM_GROUP_W)
        bg = xbc[:, D_INNER + g * D_STATE:D_INNER + (g + 1) * D_STATE].astype(BF16)
        cg32 = xbc[:, D_INNER + (SSM_GROUPS + g) * D_STATE:D_INNER + (SSM_GROUPS + g + 1) * D_STATE]
        cbm = jnp.where(causal, lax.dot_general(cg32.astype(BF16), bg, NT_DIMS, preferred_element_type=F32), 0.0)
        y_g = xs[:, ch] * dexp_ref[:, ch] + _ssd_intra(g, cbm, acum, acum_t, xdt[:, ch], hmask)
        y_off = jnp.zeros((tm, SSM_GROUP_W), F32)
        xw_g = xw[:, ch]
        for b in range(r):
            s_old = s0_ref[b, ch, :]
            c_b = jnp.where(row == b, cg32, 0.0).astype(BF16)
            xw_b = jnp.where(row == b, xw_g, 0.0).astype(BF16)
            y_off = y_off + lax.dot_general(c_b, s_old.astype(BF16), NT_DIMS, preferred_element_type=F32)
            upd = lax.dot_general(xw_b, bg, TN_DIMS, preferred_element_type=F32)
            dec = jnp.concatenate(
                [jnp.broadcast_to(dec_cols[b][g * HEADS_PER_GROUP + hr:g * HEADS_PER_GROUP + hr + 1, :],
                                  (SSM_HEAD_DIM, D_STATE)) for hr in range(HEADS_PER_GROUP)], axis=0)
            s_ref[b, ch, :] = s_old * dec + upd
        z_g = z_ref[:, :, ch].reshape(tm, SSM_GROUP_W)
        yn_g = _gate_norm(y_g + y_off * ea_x[:, ch], z_g, ng_ref[:, ch])
        yn_ref[:, :, ch] = yn_g.reshape(r, tl, SSM_GROUP_W).astype(BF16)


def _ssd_consts(d_skip):
    e_bf = jnp.repeat(jnp.eye(SSM_HEADS, dtype=F32), SSM_HEAD_DIM, axis=1).astype(BF16)
    dexp = jnp.repeat(d_skip, SSM_HEAD_DIM).reshape(1, D_INNER)
    return e_bf, dexp


def _ssm_prompt(x, mod, layer, row0, tl, g, w_in, conv_w, conv_b, dt_bias, a_log, d_skip, norm_g, w_out):
    nb, l, d = x.shape
    assert l % tl == 0 and tl % SSM_CHUNK == 0 and tl % ROW_TILE == 0
    e_bf, dexp = _ssd_consts(d_skip)
    return pl.pallas_call(
        _ssm_prompt_kernel,
        grid=(nb, l // tl),
        in_specs=[pl.BlockSpec((1, tl, d), lambda i, j: (i, j, 0)),
                  _mod_spec(layer, 0, 1, row0),
                  _const_spec((1, d)),
                  _const_spec((d, D_ZX + SSM_HEADS)),
                  _const_spec((CONV_W, CONV_DIM)),
                  _const_spec((1, CONV_DIM)),
                  _const_spec((1, SSM_HEADS)),
                  _const_spec((1, SSM_HEADS)),
                  _const_spec((1, D_INNER)),
                  _const_spec((SSM_CHUNK, 3 * SSM_CHUNK)),
                  _const_spec((2 * SSM_HEADS, D_INNER)),
                  _const_spec((1, D_INNER)),
                  _const_spec((D_INNER, d))],
        out_specs=[pl.BlockSpec((1, tl, d), lambda i, j: (i, j, 0)),
                   pl.BlockSpec((1, CONV_PAD, CONV_DIM), lambda i, j: (i, 0, 0)),
                   pl.BlockSpec((1, D_INNER, D_STATE), lambda i, j: (i, 0, 0))],
        out_shape=[jax.ShapeDtypeStruct((nb, l, d), F32),
                   jax.ShapeDtypeStruct((nb, CONV_PAD, CONV_DIM), F32),
                   jax.ShapeDtypeStruct((nb, D_INNER, D_STATE), F32)],
        scratch_shapes=[pltpu.VMEM((CONV_PAD + tl, CONV_DIM), F32),
                        pltpu.VMEM((tl, D_INNER), F32),
                        pltpu.VMEM((tl, CONV_DIM), F32),
                        pltpu.VMEM((tl, D_INNER), BF16),
                        pltpu.VMEM((D_STATE, D_INNER), F32)],
        compiler_params=_params("parallel", "arbitrary"),
        name="ssm_prompt",
    )(x, mod, g.reshape(1, d), w_in.astype(BF16), conv_w, conv_b.reshape(1, CONV_DIM),
      dt_bias.reshape(1, SSM_HEADS), a_log.reshape(1, SSM_HEADS), dexp,
      jnp.tile(jnp.tril(jnp.ones((SSM_CHUNK, SSM_CHUNK), BF16)), (1, 3)), jnp.concatenate([e_bf, e_bf], axis=0),
      norm_g.reshape(1, D_INNER), w_out.astype(BF16))


def _ssd_sample(xbc, dt, z, a_log, d_skip, norm_g, state0, r):
    nb, l, _ = xbc.shape
    tm = r * l
    assert l <= SSM_CHUNK and nb % r == 0
    eye_r = jnp.eye(r, dtype=F32)
    lcum = jnp.kron(eye_r, jnp.tril(jnp.ones((l, l), F32)))
    bd = jnp.kron(eye_r, jnp.ones((l, l), F32))
    e_bf, dexp = _ssd_consts(d_skip)
    return pl.pallas_call(
        _ssd_sample_kernel,
        grid=(nb // r,),
        in_specs=[pl.BlockSpec((r, l, CONV_DIM), lambda i: (i, 0, 0)),
                  pl.BlockSpec((r, l, SSM_HEADS), lambda i: (i, 0, 0)),
                  pl.BlockSpec((r, l, D_INNER), lambda i: (i, 0, 0)),
                  _const_spec((1, SSM_HEADS)),
                  _const_spec((1, D_INNER)),
                  _const_spec((tm, tm)),
                  _const_spec((tm, tm)),
                  _const_spec((SSM_HEADS, D_INNER)),
                  _const_spec((1, D_INNER)),
                  pl.BlockSpec((r, D_INNER, D_STATE), lambda i: (i, 0, 0))],
        out_specs=[pl.BlockSpec((r, l, D_INNER), lambda i: (i, 0, 0)),
                   pl.BlockSpec((r, D_INNER, D_STATE), lambda i: (i, 0, 0))],
        out_shape=[jax.ShapeDtypeStruct((nb, l, D_INNER), BF16),
                   jax.ShapeDtypeStruct((nb, D_INNER, D_STATE), F32)],
        compiler_params=_params("parallel"),
        name="ssd_scan_sample",
    )(xbc, dt, z, a_log.reshape(1, SSM_HEADS), dexp, lcum, bd, e_bf, norm_g.reshape(1, D_INNER), state0)


def _ssm_out_kernel(x_ref, yn_ref, mod_ref, wout_ref, o_ref):
    r, tl, d = x_ref.shape
    if r == 1:
        yn = yn_ref[0]
    else:
        yn = yn_ref[...].astype(F32).reshape(r * tl, D_INNER).astype(BF16)
    out = jnp.dot(yn, wout_ref[...], preferred_element_type=F32)
    o_ref[...] = x_ref[...] + mod_ref[2] * out.reshape(r, tl, d)


def _ssm_out(x, yn, mod, layer, row0, r, tl, w_out):
    nb, l, d = x.shape
    return pl.pallas_call(
        _ssm_out_kernel,
        grid=(nb // r, l // tl),
        in_specs=[pl.BlockSpec((r, tl, d), lambda i, j: (i, j, 0)),
                  pl.BlockSpec((r, tl, D_INNER), lambda i, j: (i, j, 0)),
                  _mod_spec(layer, 0, r, row0),
                  _const_spec((D_INNER, d))],
        out_specs=pl.BlockSpec((r, tl, d), lambda i, j: (i, j, 0)),
        out_shape=jax.ShapeDtypeStruct((nb, l, d), F32),
        compiler_params=_params("parallel", "parallel"),
        name="ssm_out_proj",
    )(x, yn, mod, w_out.astype(BF16))


def _trunk(x, mod, row0, tiles, ssm_state, conv_state, p):
    nb, l, _ = x.shape
    new_v, new_ssm, new_conv = [], [], []
    for i in range(DEPTH):
        j = i // 2
        if i % 2 == 0:
            x, v = _gm_layer(x, mod, i, row0, *tiles["gm"], p["norm1_g"][i], p["gm_w_in"][j], p["gm_ln_g"][j],
                             p["gm_ln_b"][j], p["gm_w_s"][j], p["gm_b_s"][j], p["gm_w_out"][j])
            new_v.append(v)
        else:
            if ssm_state is None:
                x, cnew, s_new = _ssm_prompt(x, mod, i, row0, tiles["ssm"], p["norm1_g"][i], p["ssm_w_in"][j],
                                             p["ssm_conv_w"][j], p["ssm_conv_b"][j], p["ssm_dt_bias"][j],
                                             p["ssm_a_log"][j], p["ssm_d"][j], p["ssm_norm_g"][j],
                                             p["ssm_w_out"][j])
            else:
                cs8 = jnp.pad(conv_state[j], ((0, 0), (CONV_PAD - (CONV_W - 1), 0), (0, 0)))
                z, xbc, dt, cnew = _ssm_in(x, mod, i, row0, *tiles["ssm_in"], p["norm1_g"][i], p["ssm_w_in"][j],
                                           p["ssm_conv_w"][j], p["ssm_conv_b"][j], p["ssm_dt_bias"][j], cs8)
                s0 = ssm_state[j].reshape(nb, D_INNER, D_STATE)
                yn, s_new = _ssd_sample(xbc, dt, z, p["ssm_a_log"][j], p["ssm_d"][j], p["ssm_norm_g"][j], s0,
                                        tiles["ssd"])
                x = _ssm_out(x, yn, mod, i, row0, *tiles["ssm_out"], p["ssm_w_out"][j])
            new_conv.append(cnew[:, CONV_PAD - (CONV_W - 1):, :])
            new_ssm.append(s_new.reshape(nb, SSM_HEADS, SSM_HEAD_DIM, D_STATE))
        x = _mlp_layer(x, mod, i, row0, *tiles["mlp"], p["norm2_g"][i], p["mlp_w1"][i], p["mlp_w2"][i],
                       p["final_g"], final=(i == DEPTH - 1))
    return x, jnp.stack(new_v), jnp.stack(new_ssm), jnp.stack(new_conv)


PROMPT_TILES = {"gm": (1, 256), "mlp": (1, 512), "ssm": 512}
SAMPLE_TILES = {"gm": (32, 8), "mlp": (64, 8), "ssm_in": (32, 8), "ssd": 8, "ssm_out": (64, 8)}


def kernel(x_prompt, x_sample, c_prompt, c_sample, state_ssm, state_conv, ada_w, ada_b, norm1_g, norm2_g, gm_w_in, gm_ln_g, gm_ln_b, gm_w_s, gm_b_s, gm_w_out, ssm_w_in, ssm_conv_w, ssm_conv_b, ssm_dt_bias, ssm_a_log, ssm_d, ssm_norm_g, ssm_w_out, mlp_w1, mlp_w2, final_g):
    p = dict(norm1_g=norm1_g, norm2_g=norm2_g, gm_w_in=gm_w_in, gm_ln_g=gm_ln_g, gm_ln_b=gm_ln_b, gm_w_s=gm_w_s,
             gm_b_s=gm_b_s, gm_w_out=gm_w_out, ssm_w_in=ssm_w_in, ssm_conv_w=ssm_conv_w, ssm_conv_b=ssm_conv_b,
             ssm_dt_bias=ssm_dt_bias, ssm_a_log=ssm_a_log, ssm_d=ssm_d, ssm_norm_g=ssm_norm_g, ssm_w_out=ssm_w_out,
             mlp_w1=mlp_w1, mlp_w2=mlp_w2, final_g=final_g)
    n_sample = x_sample.shape[0]
    mod = _ada(jnp.concatenate([c_sample, c_prompt], axis=0), ada_w, ada_b)
    y_p, v_p, ssm_p, conv_p = _trunk(x_prompt, mod, n_sample, PROMPT_TILES, None, None, p)
    y_s, v_s, ssm_s, conv_s = _trunk(x_sample, mod, 0, SAMPLE_TILES, state_ssm, state_conv, p)
    return (y_p, y_s, v_p, v_s, ssm_p, conv_p, ssm_s, conv_s)
```

```python
import functools

import jax
import jax.numpy as jnp
from jax import lax
from jax.experimental import pallas as pl
from jax.experimental.pallas import tpu as pltpu

F32 = jnp.float32
BF16 = jnp.bfloat16
HIGHEST = lax.Precision.HIGHEST

D_MODEL = 1024
DEPTH = 2
N_MOD = 6
EPS = 1e-6
LOG2E = 1.4426950408889634
GM_CHUNK = 128
D_GM = 2 * D_MODEL
GM_GROUPS = 8
GM_GROUP_W = D_GM // GM_GROUPS
D_INNER = 2 * D_MODEL
SSM_HEAD_DIM = 64
SSM_HEADS = D_INNER // SSM_HEAD_DIM
SSM_GROUPS = 8
HEADS_PER_GROUP = SSM_HEADS // SSM_GROUPS
SSM_GROUP_W = D_INNER // SSM_GROUPS
D_STATE = 128
CONV_W = 4
CONV_DIM = D_INNER + 2 * SSM_GROUPS * D_STATE
D_ZX = D_INNER + CONV_DIM
SSM_CHUNK = 128
D_FF = 4 * D_MODEL
FF_CHUNK = 1024
SUBLANES = 8
CONV_PAD = SUBLANES
ROW_TILE = 256
LANE_TILE = 512
VMEM_LIMIT = 56 * 1024 * 1024


def _silu(x):
    return x * (1.0 / (1.0 + jnp.exp2(x * (-LOG2E))))


def _softplus(x):
    return jnp.maximum(x, 0.0) + jnp.log1p(jnp.exp(-jnp.abs(x)))


def _gelu_tanh(x):
    return x * (0.5 * (1.0 + jnp.tanh(0.7978845608028654 * (x + 0.044715 * (x * x * x)))))


def _rms(x, g):
    return x * lax.rsqrt(jnp.mean(x * x, axis=-1, keepdims=True) + EPS) * g


def _norm_modulate(x3, g, shift, scale):
    r, tl, d = x3.shape
    hn = _rms(x3.reshape(r * tl, d), g)
    h3 = hn.reshape(r, tl, d) * (1.0 + scale) + shift
    return h3.reshape(r * tl, d).astype(BF16)


def _const_spec(shape):
    return pl.BlockSpec(shape, lambda *_: (0,) * len(shape), pipeline_mode=pl.Buffered(1))


def _params(*sem):
    return pltpu.CompilerParams(dimension_semantics=sem, vmem_limit_bytes=VMEM_LIMIT)


def _ada_kernel(c_ref, w_ref, b_ref, o_ref):
    sc = _silu(c_ref[...]).astype(BF16)
    o_ref[...] = jnp.dot(sc, w_ref[...].astype(BF16), preferred_element_type=F32) + b_ref[...]


def _ada(c_all, ada_w, ada_b):
    nb = c_all.shape[0]
    out = pl.pallas_call(
        _ada_kernel,
        grid=(DEPTH, N_MOD),
        in_specs=[pl.BlockSpec((nb, D_MODEL), lambda i, k: (0, 0)),
                  pl.BlockSpec((None, D_MODEL, D_MODEL), lambda i, k: (i, 0, k)),
                  pl.BlockSpec((None, None, 1, D_MODEL), lambda i, k: (i, k, 0, 0))],
        out_specs=pl.BlockSpec((None, None, nb, D_MODEL), lambda i, k: (i, k, 0, 0)),
        out_shape=jax.ShapeDtypeStruct((DEPTH, N_MOD, nb, D_MODEL), F32),
        compiler_params=_params("parallel", "parallel"),
        name="ada_mod",
    )(c_all, ada_w, ada_b.reshape(DEPTH, N_MOD, 1, D_MODEL))
    return out.reshape(DEPTH, N_MOD, nb, 1, D_MODEL)


def _mod_spec(layer, half, r, row0):
    return pl.BlockSpec((None, 3, r, 1, D_MODEL), lambda i, j: (layer, half, row0 // r + i, 0, 0))


def _gm_kernel(x_ref, mod_ref, g_ref, win_ref, lng_ref, lnb_ref, wmix_ref, bexp_ref, wout_ref,
               xo_ref, v_ref, z_ref, *, v_tail):
    r, tl, d = x_ref.shape
    tm = r * tl
    x3 = x_ref[...]
    h = _norm_modulate(x3, g_ref[...], mod_ref[0], mod_ref[1])
    for k in range(2 * D_GM // FF_CHUNK):
        cols = slice(k * FF_CHUNK, (k + 1) * FF_CHUNK)
        z_ref[:, cols] = _gelu_tanh(jnp.dot(h, win_ref[:, cols], preferred_element_type=F32))
    u = z_ref[:, :D_GM]
    vr = z_ref[:, D_GM:]
    xc = vr - jnp.mean(vr, axis=-1, keepdims=True)
    v = xc * lax.rsqrt(jnp.mean(xc * xc, axis=-1, keepdims=True) + EPS) * lng_ref[...] + lnb_ref[...]
    vb = v.astype(BF16)
    chunks = []
    for c in range(tm // GM_CHUNK):
        vc = vb[c * GM_CHUNK:(c + 1) * GM_CHUNK]
        parts = [jnp.dot(wmix_ref[g], vc[:, g * GM_GROUP_W:(g + 1) * GM_GROUP_W], preferred_element_type=F32)
                 for g in range(GM_GROUPS)]
        chunks.append(jnp.concatenate(parts, axis=1) + bexp_ref[...])
    s = jnp.concatenate(chunks, axis=0) if len(chunks) > 1 else chunks[0]
    y = jnp.dot((u * s).astype(BF16), wout_ref[...], preferred_element_type=F32)
    xo_ref[...] = x3 + mod_ref[2] * y.reshape(r, tl, d)

    @pl.when(pl.program_id(1) == pl.num_programs(1) - 1)
    def _():
        v_ref[...] = v.reshape(r, tl, D_GM)[:, tl - v_tail:, :]


def _gm_layer(x, mod, layer, row0, r, tl, g, w_in, ln_g, ln_b, w_s, b_s, w_out):
    nb, l, d = x.shape
    q = min(l, GM_CHUNK)
    assert l % q == 0 and GM_CHUNK % q == 0 and (r * tl) % GM_CHUNK == 0 and tl % q == 0
    rep = GM_CHUNK // q
    tri = jnp.tril(jnp.ones((q, q), F32))
    wq = w_s[:, :q, :q] * tri
    wmix = jnp.einsum("ab,gts->gatbs", jnp.eye(rep, dtype=F32), wq).reshape(GM_GROUPS, GM_CHUNK, GM_CHUNK)
    bq = jnp.tile(b_s[:, :q].T, (rep, 1))
    bexp = jnp.repeat(bq, GM_GROUP_W, axis=1)
    v_tail = l - ((l - 1) // GM_CHUNK) * GM_CHUNK
    assert v_tail <= tl
    kern = functools.partial(_gm_kernel, v_tail=v_tail)
    xo, v = pl.pallas_call(
        kern,
        grid=(nb // r, l // tl),
        in_specs=[pl.BlockSpec((r, tl, d), lambda i, j: (i, j, 0)),
                  _mod_spec(layer, 0, r, row0),
                  _const_spec((1, d)),
                  _const_spec((d, 2 * D_GM)),
                  _const_spec((1, D_GM)),
                  _const_spec((1, D_GM)),
                  _const_spec((GM_GROUPS, GM_CHUNK, GM_CHUNK)),
                  _const_spec((GM_CHUNK, D_GM)),
                  _const_spec((D_GM, d))],
        out_specs=[pl.BlockSpec((r, tl, d), lambda i, j: (i, j, 0)),
                   pl.BlockSpec((r, v_tail, D_GM), lambda i, j: (i, 0, 0))],
        out_shape=[jax.ShapeDtypeStruct((nb, l, d), F32),
                   jax.ShapeDtypeStruct((nb, v_tail, D_GM), F32)],
        scratch_shapes=[pltpu.VMEM((r * tl, 2 * D_GM), F32)],
        compiler_params=_params("parallel", "arbitrary"),
        name="gmlp_mixer",
    )(x, mod, g.reshape(1, d), w_in.astype(BF16), ln_g.reshape(1, D_GM), ln_b.reshape(1, D_GM),
      wmix.astype(BF16), bexp, w_out.astype(BF16))
    return xo, v


def _mlp_kernel(x_ref, mod_ref, g_ref, w1_ref, w2_ref, gf_ref, o_ref, *, final):
    r, tl, d = x_ref.shape
    x3 = x_ref[...]
    h = _norm_modulate(x3, g_ref[...], mod_ref[0], mod_ref[1])
    acc = jnp.zeros((r * tl, d), F32)
    for k in range(D_FF // FF_CHUNK):
        a = jnp.dot(h, w1_ref[:, k * FF_CHUNK:(k + 1) * FF_CHUNK], preferred_element_type=F32)
        a = jnp.square(jnp.maximum(a, 0.0)).astype(BF16)
        acc = acc + jnp.dot(a, w2_ref[k * FF_CHUNK:(k + 1) * FF_CHUNK, :], preferred_element_type=F32)
    xo = x3 + mod_ref[2] * acc.reshape(r, tl, d)
    if final:
        xo = _rms(xo.reshape(r * tl, d), gf_ref[...]).reshape(r, tl, d)
    o_ref[...] = xo


def _mlp_layer(x, mod, layer, row0, r, tl, g, w1, w2, final_g, final):
    nb, l, d = x.shape
    kern = functools.partial(_mlp_kernel, final=final)
    return pl.pallas_call(
        kern,
        grid=(nb // r, l // tl),
        in_specs=[pl.BlockSpec((r, tl, d), lambda i, j: (i, j, 0)),
                  _mod_spec(layer, 1, r, row0),
                  _const_spec((1, d)),
                  _const_spec((d, D_FF)),
                  _const_spec((D_FF, d)),
                  _const_spec((1, d))],
        out_specs=pl.BlockSpec((r, tl, d), lambda i, j: (i, j, 0)),
        out_shape=jax.ShapeDtypeStruct((nb, l, d), F32),
        compiler_params=_params("parallel", "parallel"),
        name="relu2_mlp",
    )(x, mod, g.reshape(1, d), w1.astype(BF16), w2.astype(BF16), final_g.reshape(1, d))


def _ssm_in_kernel(x_ref, mod_ref, g_ref, w_ref, cw_ref, cb_ref, dtb_ref, cs_ref,
                   z_ref, xbc_ref, dt_ref, cnew_ref, xp_ref):
    r, tl, d = x_ref.shape

    @pl.when(pl.program_id(1) == 0)
    def _():
        xp_ref[:, 0:CONV_PAD, :] = jnp.zeros((r, CONV_PAD, CONV_DIM), F32)
        xp_ref[:, CONV_PAD - (CONV_W - 1):CONV_PAD, :] = cs_ref[...]

    h = _norm_modulate(x_ref[...], g_ref[...], mod_ref[0], mod_ref[1])
    z_ref[...] = jnp.dot(h, w_ref[:, :D_INNER], preferred_element_type=F32).reshape(r, tl, D_INNER)
    xbc = jnp.dot(h, w_ref[:, D_INNER:D_ZX], preferred_element_type=F32)
    dt_raw = jnp.dot(h, w_ref[:, D_ZX:], preferred_element_type=F32)
    dt_ref[...] = _softplus(dt_raw + dtb_ref[...]).reshape(r, tl, SSM_HEADS)
    xp_ref[:, CONV_PAD:CONV_PAD + tl, :] = xbc.reshape(r, tl, CONV_DIM)
    regs = r * tl // SUBLANES
    cur = xp_ref[:, CONV_PAD:CONV_PAD + tl, :].reshape(regs, SUBLANES, CONV_DIM)
    prv = xp_ref[:, 0:tl, :].reshape(regs, SUBLANES, CONV_DIM)
    sub = lax.broadcasted_iota(jnp.int32, (1, SUBLANES, 1), 1)
    conv = cb_ref[...].reshape(1, 1, CONV_DIM) + cur * cw_ref[CONV_W - 1:CONV_W, :].reshape(1, 1, CONV_DIM)
    for s in range(1, CONV_W):
        shifted = pltpu.roll(jnp.where(sub < SUBLANES - s, cur, prv), s, axis=1)
        conv = conv + shifted * cw_ref[CONV_W - 1 - s:CONV_W - s, :].reshape(1, 1, CONV_DIM)
    xbc_ref[...] = _silu(conv).reshape(r, tl, CONV_DIM)
    cnew_ref[...] = xp_ref[:, tl + CONV_PAD - (CONV_W - 1):tl + CONV_PAD, :]
    xp_ref[:, 0:CONV_PAD, :] = xp_ref[:, tl:tl + CONV_PAD, :]


def _ssm_in(x, mod, layer, row0, r, tl, g, w_in, conv_w, conv_b, dt_bias, conv_state8):
    nb, l, d = x.shape
    return pl.pallas_call(
        _ssm_in_kernel,
        grid=(nb // r, l // tl),
        in_specs=[pl.BlockSpec((r, tl, d), lambda i, j: (i, j, 0)),
                  _mod_spec(layer, 0, r, row0),
                  _const_spec((1, d)),
                  _const_spec((d, D_ZX + SSM_HEADS)),
                  _const_spec((CONV_W, CONV_DIM)),
                  _const_spec((1, CONV_DIM)),
                  _const_spec((1, SSM_HEADS)),
                  pl.BlockSpec((r, CONV_W - 1, CONV_DIM), lambda i, j: (i, 0, 0))],
        out_specs=[pl.BlockSpec((r, tl, D_INNER), lambda i, j: (i, j, 0)),
                   pl.BlockSpec((r, tl, CONV_DIM), lambda i, j: (i, j, 0)),
                   pl.BlockSpec((r, tl, SSM_HEADS), lambda i, j: (i, j, 0)),
                   pl.BlockSpec((r, CONV_W - 1, CONV_DIM), lambda i, j: (i, 0, 0))],
        out_shape=[jax.ShapeDtypeStruct((nb, l, D_INNER), F32),
                   jax.ShapeDtypeStruct((nb, l, CONV_DIM), F32),
                   jax.ShapeDtypeStruct((nb, l, SSM_HEADS), F32),
                   jax.ShapeDtypeStruct((nb, CONV_W - 1, CONV_DIM), F32)],
        scratch_shapes=[pltpu.VMEM((r, CONV_PAD + tl, CONV_DIM), F32)],
        compiler_params=_params("parallel", "arbitrary"),
        name="ssm_in_conv",
    )(x, mod, g.reshape(1, d), w_in.astype(BF16), conv_w, conv_b.reshape(1, CONV_DIM),
      dt_bias.reshape(1, SSM_HEADS), conv_state8)


NT_DIMS = (((1,), (1,)), ((), ()))
TN_DIMS = (((0,), (0,)), ((), ()))


def _expand_heads(vals, e_bf):
    tm = vals[0].shape[0]
    v = jnp.concatenate(vals, axis=0)
    hi = v.astype(BF16)
    lo = (v - hi.astype(F32)).astype(BF16)
    x = jnp.dot(hi, e_bf, preferred_element_type=F32) + jnp.dot(lo, e_bf, preferred_element_type=F32)
    return [x[k * tm:(k + 1) * tm] for k in range(len(vals))]


def _head_masks():
    lane = lax.broadcasted_iota(jnp.int32, (1, SSM_GROUP_W), 1) // SSM_HEAD_DIM
    return [jnp.where(lane == hr, 1.0, 0.0).astype(BF16) for hr in range(HEADS_PER_GROUP)]


def _ssd_intra(g, cbm, acum, acum_t, xdt_g, hmask):
    acc = None
    for hr in range(HEADS_PER_GROUP):
        hh = g * HEADS_PER_GROUP + hr
        seg = jnp.minimum(acum[:, hh:hh + 1] - acum_t[hh:hh + 1, :], 0.0)
        lmat = (cbm * jnp.exp(seg)).astype(BF16)
        part = jnp.dot(lmat, xdt_g * hmask[hr], preferred_element_type=F32)
        acc = part if acc is None else acc + part
    return acc


def _cumsum_rows(lcum3, da):
    hi = da.astype(BF16)
    rest = da - hi.astype(F32)
    mid = rest.astype(BF16)
    lo = (rest - mid.astype(F32)).astype(BF16)
    return jnp.dot(lcum3, jnp.concatenate([hi, mid, lo], axis=0), preferred_element_type=F32)


def _gate_norm(y_g, z_g, ng_g):
    return _rms(y_g * _silu(z_g), ng_g)


def _ssm_prompt_kernel(x_ref, mod_ref, g_ref, w_ref, cw_ref, cb_ref, dtb_ref, alog_ref, dexp_ref, lcum_ref, e_ref,
                       ng_ref, wout_ref, xo_ref, cnew_ref, s_ref, xp_ref, z_ref, xc_ref, yn_ref, st_ref):
    _, tl, _ = x_ref.shape
    j = pl.program_id(1)

    @pl.when(j == 0)
    def _():
        xp_ref[0:CONV_PAD, :] = jnp.zeros((CONV_PAD, CONV_DIM), F32)
        st_ref[...] = jnp.zeros_like(st_ref)

    h = _norm_modulate(x_ref[...], g_ref[...], mod_ref[0], mod_ref[1])
    z_ref[...] = jnp.dot(h, w_ref[:, :D_INNER], preferred_element_type=F32)
    xp_ref[CONV_PAD:CONV_PAD + tl, :] = jnp.dot(h, w_ref[:, D_INNER:D_ZX], preferred_element_type=F32)
    dt_all = _softplus(jnp.dot(h, w_ref[:, D_ZX:], preferred_element_type=F32) + dtb_ref[...])

    sub = lax.broadcasted_iota(jnp.int32, (1, SUBLANES, 1), 1)
    for lb in range(CONV_DIM // LANE_TILE):
        lns = slice(lb * LANE_TILE, (lb + 1) * LANE_TILE)
        taps = [cw_ref[k:k + 1, lns].reshape(1, 1, LANE_TILE) for k in range(CONV_W)]
        bias = cb_ref[:, lns].reshape(1, 1, LANE_TILE)
        for rb in range(tl // ROW_TILE):
            r0 = rb * ROW_TILE
            cur = xp_ref[CONV_PAD + r0:CONV_PAD + r0 + ROW_TILE, lns]
            prv = xp_ref[r0:r0 + ROW_TILE, lns]
            cur = cur.reshape(ROW_TILE // SUBLANES, SUBLANES, LANE_TILE)
            prv = prv.reshape(ROW_TILE // SUBLANES, SUBLANES, LANE_TILE)
            conv = bias + cur * taps[CONV_W - 1]
            for s in range(1, CONV_W):
                conv = conv + pltpu.roll(jnp.where(sub < SUBLANES - s, cur, prv), s, axis=1) * taps[CONV_W - 1 - s]
            xc_ref[r0:r0 + ROW_TILE, lns] = _silu(conv).reshape(ROW_TILE, LANE_TILE)
    cnew_ref[0] = xp_ref[tl + CONV_PAD - (CONV_W - 1):tl + CONV_PAD, :]
    xp_ref[0:CONV_PAD, :] = xp_ref[tl:tl + CONV_PAD, :]

    a_neg = -jnp.exp(alog_ref[...])
    lcum3 = lcum_ref[...]
    causal = lcum3[:, :SSM_CHUNK].astype(F32) > 0.5
    hmask = _head_masks()
    for c in range(tl // SSM_CHUNK):
        rows = slice(c * SSM_CHUNK, (c + 1) * SSM_CHUNK)
        dt = dt_all[rows]
        acum = _cumsum_rows(lcum3, dt * a_neg)
        alast = acum[SSM_CHUNK - 1:SSM_CHUNK, :]
        acum2 = acum * LOG2E
        ldt_t = jnp.log2(dt).T
        src_t = acum2.T - ldt_t
        heads = jnp.concatenate([jnp.exp(acum), dt * jnp.exp(alast - acum)], axis=0)
        heads_hi = heads.astype(BF16)
        heads_lo = (heads - heads_hi.astype(F32)).astype(BF16)
        ex = jnp.dot(jnp.concatenate([heads_hi, heads_lo], axis=1), e_ref[...], preferred_element_type=F32)
        for g in range(SSM_GROUPS):
            ch = slice(g * SSM_GROUP_W, (g + 1) * SSM_GROUP_W)
            b_lo = D_INNER + g * D_STATE
            c_lo = D_INNER + (SSM_GROUPS + g) * D_STATE
            ea_x = ex[:SSM_CHUNK, ch]
            w_x = ex[SSM_CHUNK:, ch]
            xs = xc_ref[rows, ch]
            xs_bf = xs.astype(BF16)
            bgt = xc_ref[rows, b_lo:b_lo + D_STATE].T.astype(BF16)
            cg = xc_ref[rows, c_lo:c_lo + D_STATE].astype(BF16)
            cbm = jnp.where(causal, jnp.dot(cg, bgt, preferred_element_type=F32), 0.0)
            lmats = []
            for hr in range(HEADS_PER_GROUP):
                hh = g * HEADS_PER_GROUP + hr
                seg = jnp.minimum(acum2[:, hh:hh + 1] - src_t[hh:hh + 1, :], ldt_t[hh:hh + 1, :])
                lmats.append((cbm * jnp.exp2(seg)).astype(BF16))
            x_stack = jnp.concatenate([xs_bf * hmask[hr] for hr in range(HEADS_PER_GROUP)], axis=0)
            y_g = xs * dexp_ref[:, ch] + jnp.dot(jnp.concatenate(lmats, axis=1), x_stack,
                                                 preferred_element_type=F32)
            s_old = st_ref[:, ch]
            y_g = y_g + jnp.dot(cg, s_old.astype(BF16), preferred_element_type=F32) * ea_x
            st_ref[:, ch] = (s_old * ea_x[SSM_CHUNK - 1:SSM_CHUNK, :]
                             + jnp.dot(bgt, (xs * w_x).astype(BF16), preferred_element_type=F32))
            yn_ref[rows, ch] = _gate_norm(y_g, z_ref[rows, ch], ng_ref[:, ch]).astype(BF16)

    out = jnp.dot(yn_ref[...], wout_ref[...], preferred_element_type=F32)
    xo_ref[0] = x_ref[0] + mod_ref[2, 0] * out

    @pl.when(j == pl.num_programs(1) - 1)
    def _():
        s_ref[0] = st_ref[...].T


def _ssd_sample_kernel(xbc_ref, dt_ref, z_ref, alog_ref, dexp_ref, lcum_ref, bd_ref, e_ref, ng_ref, s0_ref,
                       yn_ref, s_ref):
    r, tl, _ = xbc_ref.shape
    tm = r * tl
    xbc = xbc_ref[...].reshape(tm, CONV_DIM)
    xs = xbc[:, :D_INNER]
    dt = dt_ref[...].reshape(tm, SSM_HEADS)
    da = dt * (-jnp.exp(alog_ref[...]))
    lcum = lcum_ref[...]
    causal = lcum > 0.5
    acum = jnp.dot(lcum, da, precision=HIGHEST, preferred_element_type=F32)
    alast = jnp.dot(bd_ref[...], da, precision=HIGHEST, preferred_element_type=F32)
    acum_t = acum.T
    dec_end_t = jnp.exp(alast.T)
    dt_x, ea_x, w_x = _expand_heads([dt, jnp.exp(acum), dt * jnp.exp(alast - acum)], e_ref[...])
    xdt = (xs * dt_x).astype(BF16)
    xw = xs * w_x
    hmask = _head_masks()
    row = lax.broadcasted_iota(jnp.int32, (tm, 1), 0) // tl
    dec_cols = [jnp.broadcast_to(dec_end_t[:, b * tl:b * tl + 1], (SSM_HEADS, D_STATE)) for b in range(r)]
    for g in range(SSM_GROUPS):
        ch = slice(g * SSM_GROUP_W, (g + 1) * SSM_GROUP_W)
        bg = xbc[:, D_INNER + g * D_STATE:D_INNER + (g + 1) * D_STATE].astype(BF16)
        cg32 = xbc[:, D_INNER + (SSM_GROUPS + g) * D_STATE:D_INNER + (SSM_GROUPS + g + 1) * D_STATE]
        cbm = jnp.where(causal, lax.dot_general(cg32.astype(BF16), bg, NT_DIMS, preferred_element_type=F32), 0.0)
        y_g = xs[:, ch] * dexp_ref[:, ch] + _ssd_intra(g, cbm, acum, acum_t, xdt[:, ch], hmask)
        y_off = jnp.zeros((tm, SSM_GROUP_W), F32)
        xw_g = xw[:, ch]
        for b in range(r):
            s_old = s0_ref[b, ch, :]
            c_b = jnp.where(row == b, cg32, 0.0).astype(BF16)
            xw_b = jnp.where(row == b, xw_g, 0.0).astype(BF16)
            y_off = y_off + lax.dot_general(c_b, s_old.astype(BF16), NT_DIMS, preferred_element_type=F32)
            upd = lax.dot_general(xw_b, bg, TN_DIMS, preferred_element_type=F32)
            dec = jnp.concatenate(
                [jnp.broadcast_to(dec_cols[b][g * HEADS_PER_GROUP + hr:g * HEADS_PER_GROUP + hr + 1, :],
                                  (SSM_HEAD_DIM, D_STATE)) for hr in range(HEADS_PER_GROUP)], axis=0)
            s_ref[b, ch, :] = s_old * dec + upd
        z_g = z_ref[:, :, ch].reshape(tm, SSM_GROUP_W)
        yn_g = _gate_norm(y_g + y_off * ea_x[:, ch], z_g, ng_ref[:, ch])
        yn_ref[:, :, ch] = yn_g.reshape(r, tl, SSM_GROUP_W).astype(BF16)


def _ssd_consts(d_skip):
    e_bf = jnp.repeat(jnp.eye(SSM_HEADS, dtype=F32), SSM_HEAD_DIM, axis=1).astype(BF16)
    dexp = jnp.repeat(d_skip, SSM_HEAD_DIM).reshape(1, D_INNER)
    return e_bf, dexp


def _ssm_prompt(x, mod, layer, row0, tl, g, w_in, conv_w, conv_b, dt_bias, a_log, d_skip, norm_g, w_out):
    nb, l, d = x.shape
    assert l % tl == 0 and tl % SSM_CHUNK == 0 and tl % ROW_TILE == 0
    e_bf, dexp = _ssd_consts(d_skip)
    return pl.pallas_call(
        _ssm_prompt_kernel,
        grid=(nb, l // tl),
        in_specs=[pl.BlockSpec((1, tl, d), lambda i, j: (i, j, 0)),
                  _mod_spec(layer, 0, 1, row0),
                  _const_spec((1, d)),
                  _const_spec((d, D_ZX + SSM_HEADS)),
                  _const_spec((CONV_W, CONV_DIM)),
                  _const_spec((1, CONV_DIM)),
                  _const_spec((1, SSM_HEADS)),
                  _const_spec((1, SSM_HEADS)),
                  _const_spec((1, D_INNER)),
                  _const_spec((SSM_CHUNK, 3 * SSM_CHUNK)),
                  _const_spec((2 * SSM_HEADS, D_INNER)),
                  _const_spec((1, D_INNER)),
                  _const_spec((D_INNER, d))],
        out_specs=[pl.BlockSpec((1, tl, d), lambda i, j: (i, j, 0)),
                   pl.BlockSpec((1, CONV_W - 1, CONV_DIM), lambda i, j: (i, 0, 0)),
                   pl.BlockSpec((1, D_INNER, D_STATE), lambda i, j: (i, 0, 0))],
        out_shape=[jax.ShapeDtypeStruct((nb, l, d), F32),
                   jax.ShapeDtypeStruct((nb, CONV_W - 1, CONV_DIM), F32),
                   jax.ShapeDtypeStruct((nb, D_INNER, D_STATE), F32)],
        scratch_shapes=[pltpu.VMEM((CONV_PAD + tl, CONV_DIM), F32),
                        pltpu.VMEM((tl, D_INNER), F32),
                        pltpu.VMEM((tl, CONV_DIM), F32),
                        pltpu.VMEM((tl, D_INNER), BF16),
                        pltpu.VMEM((D_STATE, D_INNER), F32)],
        compiler_params=_params("parallel", "arbitrary"),
        name="ssm_prompt",
    )(x, mod, g.reshape(1, d), w_in.astype(BF16), conv_w, conv_b.reshape(1, CONV_DIM),
      dt_bias.reshape(1, SSM_HEADS), a_log.reshape(1, SSM_HEADS), dexp,
      jnp.tile(jnp.tril(jnp.ones((SSM_CHUNK, SSM_CHUNK), BF16)), (1, 3)), jnp.concatenate([e_bf, e_bf], axis=0),
      norm_g.reshape(1, D_INNER), w_out.astype(BF16))


def _ssd_sample(xbc, dt, z, a_log, d_skip, norm_g, state0, r):
    nb, l, _ = xbc.shape
    tm = r * l
    assert l <= SSM_CHUNK and nb % r == 0
    eye_r = jnp.eye(r, dtype=F32)
    lcum = jnp.kron(eye_r, jnp.tril(jnp.ones((l, l), F32)))
    bd = jnp.kron(eye_r, jnp.ones((l, l), F32))
    e_bf, dexp = _ssd_consts(d_skip)
    return pl.pallas_call(
        _ssd_sample_kernel,
        grid=(nb // r,),
        in_specs=[pl.BlockSpec((r, l, CONV_DIM), lambda i: (i, 0, 0)),
                  pl.BlockSpec((r, l, SSM_HEADS), lambda i: (i, 0, 0)),
                  pl.BlockSpec((r, l, D_INNER), lambda i: (i, 0, 0)),
                  _const_spec((1, SSM_HEADS)),
                  _const_spec((1, D_INNER)),
                  _const_spec((tm, tm)),
                  _const_spec((tm, tm)),
                  _const_spec((SSM_HEADS, D_INNER)),
                  _const_spec((1, D_INNER)),
                  pl.BlockSpec((r, D_INNER, D_STATE), lambda i: (i, 0, 0))],
        out_specs=[pl.BlockSpec((r, l, D_INNER), lambda i: (i, 0, 0)),
                   pl.BlockSpec((r, D_INNER, D_STATE), lambda i: (i, 0, 0))],
        out_shape=[jax.ShapeDtypeStruct((nb, l, D_INNER), BF16),
                   jax.ShapeDtypeStruct((nb, D_INNER, D_STATE), F32)],
        compiler_params=_params("parallel"),
        name="ssd_scan_sample",
    )(xbc, dt, z, a_log.reshape(1, SSM_HEADS), dexp, lcum, bd, e_bf, norm_g.reshape(1, D_INNER), state0)


def _ssm_out_kernel(x_ref, yn_ref, mod_ref, wout_ref, o_ref):
    r, tl, d = x_ref.shape
    if r == 1:
        yn = yn_ref[0]
    else:
        yn = yn_ref[...].astype(F32).reshape(r * tl, D_INNER).astype(BF16)
    out = jnp.dot(yn, wout_ref[...], preferred_element_type=F32)
    o_ref[...] = x_ref[...] + mod_ref[2] * out.reshape(r, tl, d)


def _ssm_out(x, yn, mod, layer, row0, r, tl, w_out):
    nb, l, d = x.shape
    return pl.pallas_call(
        _ssm_out_kernel,
        grid=(nb // r, l // tl),
        in_specs=[pl.BlockSpec((r, tl, d), lambda i, j: (i, j, 0)),
                  pl.BlockSpec((r, tl, D_INNER), lambda i, j: (i, j, 0)),
                  _mod_spec(layer, 0, r, row0),
                  _const_spec((D_INNER, d))],
        out_specs=pl.BlockSpec((r, tl, d), lambda i, j: (i, j, 0)),
        out_shape=jax.ShapeDtypeStruct((nb, l, d), F32),
        compiler_params=_params("parallel", "parallel"),
        name="ssm_out_proj",
    )(x, yn, mod, w_out.astype(BF16))


def _trunk(x, mod, row0, tiles, ssm_state, conv_state, p):
    nb, l, _ = x.shape
    new_v, new_ssm, new_conv = [], [], []
    for i in range(DEPTH):
        j = i // 2
        if i % 2 == 0:
            x, v = _gm_layer(x, mod, i, row0, *tiles["gm"], p["norm1_g"][i], p["gm_w_in"][j], p["gm_ln_g"][j],
                             p["gm_ln_b"][j], p["gm_w_s"][j], p["gm_b_s"][j], p["gm_w_out"][j])
            new_v.append(v)
        else:
            if ssm_state is None:
                x, cnew, s_new = _ssm_prompt(x, mod, i, row0, tiles["ssm"], p["norm1_g"][i], p["ssm_w_in"][j],
                                             p["ssm_conv_w"][j], p["ssm_conv_b"][j], p["ssm_dt_bias"][j],
                                             p["ssm_a_log"][j], p["ssm_d"][j], p["ssm_norm_g"][j],
                                             p["ssm_w_out"][j])
            else:
                z, xbc, dt, cnew = _ssm_in(x, mod, i, row0, *tiles["ssm_in"], p["norm1_g"][i], p["ssm_w_in"][j],
                                           p["ssm_conv_w"][j], p["ssm_conv_b"][j], p["ssm_dt_bias"][j],
                                           conv_state[j])
                s0 = ssm_state[j].reshape(nb, D_INNER, D_STATE)
                yn, s_new = _ssd_sample(xbc, dt, z, p["ssm_a_log"][j], p["ssm_d"][j], p["ssm_norm_g"][j], s0,
                                        tiles["ssd"])
                x = _ssm_out(x, yn, mod, i, row0, *tiles["ssm_out"], p["ssm_w_out"][j])
            new_conv.append(cnew)
            new_ssm.append(s_new.reshape(nb, SSM_HEADS, SSM_HEAD_DIM, D_STATE))
        x = _mlp_layer(x, mod, i, row0, *tiles["mlp"], p["norm2_g"][i], p["mlp_w1"][i], p["mlp_w2"][i],
                       p["final_g"], final=(i == DEPTH - 1))
    return x, jnp.stack(new_v), jnp.stack(new_ssm), jnp.stack(new_conv)


PROMPT_TILES = {"gm": (1, 512), "mlp": (1, 1024), "ssm": 512}
SAMPLE_TILES = {"gm": (64, 8), "mlp": (64, 8), "ssm_in": (32, 8), "ssd": 8, "ssm_out": (64, 8)}


def kernel(x_prompt, x_sample, c_prompt, c_sample, state_ssm, state_conv, ada_w, ada_b, norm1_g, norm2_g, gm_w_in, gm_ln_g, gm_ln_b, gm_w_s, gm_b_s, gm_w_out, ssm_w_in, ssm_conv_w, ssm_conv_b, ssm_dt_bias, ssm_a_log, ssm_d, ssm_norm_g, ssm_w_out, mlp_w1, mlp_w2, final_g):
    p = dict(norm1_g=norm1_g, norm2_g=norm2_g, gm_w_in=gm_w_in, gm_ln_g=gm_ln_g, gm_ln_b=gm_ln_b, gm_w_s=gm_w_s,
             gm_b_s=gm_b_s, gm_w_out=gm_w_out, ssm_w_in=ssm_w_in, ssm_conv_w=ssm_conv_w, ssm_conv_b=ssm_conv_b,
             ssm_dt_bias=ssm_dt_bias, ssm_a_log=ssm_a_log, ssm_d=ssm_d, ssm_norm_g=ssm_norm_g, ssm_w_out=ssm_w_out,
             mlp_w1=mlp_w1, mlp_w2=mlp_w2, final_g=final_g)
    n_sample = x_sample.shape[0]
    mod = _ada(jnp.concatenate([c_sample, c_prompt], axis=0), ada_w, ada_b)
    y_p, v_p, ssm_p, conv_p = _trunk(x_prompt, mod, n_sample, PROMPT_TILES, None, None, p)
    y_s, v_s, ssm_s, conv_s = _trunk(x_sample, mod, 0, SAMPLE_TILES, state_ssm, state_conv, p)
    return (y_p, y_s, v_p, v_s, ssm_p, conv_p, ssm_s, conv_s)
```

```python
import functools

import jax
import jax.numpy as jnp
from jax import lax
from jax.experimental import pallas as pl
from jax.experimental.pallas import tpu as pltpu

F32 = jnp.float32
BF16 = jnp.bfloat16
HIGHEST = lax.Precision.HIGHEST

D_MODEL = 1024
DEPTH = 2
N_MOD = 6
EPS = 1e-6
LOG2E = 1.4426950408889634
GM_CHUNK = 128
D_GM = 2 * D_MODEL
GM_GROUPS = 8
GM_GROUP_W = D_GM // GM_GROUPS
D_INNER = 2 * D_MODEL
SSM_HEAD_DIM = 64
SSM_HEADS = D_INNER // SSM_HEAD_DIM
SSM_GROUPS = 8
HEADS_PER_GROUP = SSM_HEADS // SSM_GROUPS
SSM_GROUP_W = D_INNER // SSM_GROUPS
D_STATE = 128
CONV_W = 4
CONV_DIM = D_INNER + 2 * SSM_GROUPS * D_STATE
D_ZX = D_INNER + CONV_DIM
SSM_CHUNK = 128
D_FF = 4 * D_MODEL
FF_CHUNK = 1024
SUBLANES = 8
CONV_PAD = SUBLANES
ROW_TILE = 256
LANE_TILE = 512
VMEM_LIMIT = 56 * 1024 * 1024


def _silu(x):
    return x * (1.0 / (1.0 + jnp.exp2(x * (-LOG2E))))


def _softplus(x):
    return jnp.maximum(x, 0.0) + jnp.log1p(jnp.exp(-jnp.abs(x)))


def _gelu_tanh(x):
    return x * (0.5 * (1.0 + jnp.tanh(0.7978845608028654 * (x + 0.044715 * (x * x * x)))))


def _rms(x, g):
    return x * lax.rsqrt(jnp.mean(x * x, axis=-1, keepdims=True) + EPS) * g


def _norm_modulate(x3, g, shift, scale):
    r, tl, d = x3.shape
    hn = _rms(x3.reshape(r * tl, d), g)
    h3 = hn.reshape(r, tl, d) * (1.0 + scale) + shift
    return h3.reshape(r * tl, d).astype(BF16)


def _const_spec(shape):
    return pl.BlockSpec(shape, lambda *_: (0,) * len(shape), pipeline_mode=pl.Buffered(1))


def _params(*sem):
    return pltpu.CompilerParams(dimension_semantics=sem, vmem_limit_bytes=VMEM_LIMIT)


def _ada_kernel(c_ref, w_ref, b_ref, o_ref):
    sc = _silu(c_ref[...]).astype(BF16)
    res = jnp.dot(sc, w_ref[...].astype(BF16), preferred_element_type=F32) + b_ref[...]
    o_ref[...] = res.reshape(o_ref.shape)


def _ada(c_all, ada_w, ada_b):
    nb = c_all.shape[0]
    return pl.pallas_call(
        _ada_kernel,
        grid=(DEPTH, N_MOD),
        in_specs=[pl.BlockSpec((nb, D_MODEL), lambda i, k: (0, 0)),
                  pl.BlockSpec((None, D_MODEL, D_MODEL), lambda i, k: (i, 0, k)),
                  pl.BlockSpec((None, None, 1, D_MODEL), lambda i, k: (i, k, 0, 0))],
        out_specs=pl.BlockSpec((None, None, nb, 1, D_MODEL), lambda i, k: (i, k, 0, 0, 0)),
        out_shape=jax.ShapeDtypeStruct((DEPTH, N_MOD, nb, 1, D_MODEL), F32),
        compiler_params=_params("parallel", "parallel"),
        name="ada_mod",
    )(c_all, ada_w, ada_b.reshape(DEPTH, N_MOD, 1, D_MODEL))


def _mod_spec(layer, half, r, row0):
    return pl.BlockSpec((None, 3, r, 1, D_MODEL), lambda i, j: (layer, half, row0 // r + i, 0, 0))


def _gm_kernel(x_ref, mod_ref, g_ref, win_ref, lng_ref, lnb_ref, wmix_ref, bexp_ref, wout_ref,
               xo_ref, v_ref, z_ref, *, v_tail):
    r, tl, d = x_ref.shape
    tm = r * tl
    x3 = x_ref[...]
    h = _norm_modulate(x3, g_ref[...], mod_ref[0], mod_ref[1])
    for k in range(2 * D_GM // FF_CHUNK):
        cols = slice(k * FF_CHUNK, (k + 1) * FF_CHUNK)
        z_ref[:, cols] = _gelu_tanh(jnp.dot(h, win_ref[:, cols], preferred_element_type=F32))
    u = z_ref[:, :D_GM]
    vr = z_ref[:, D_GM:]
    xc = vr - jnp.mean(vr, axis=-1, keepdims=True)
    v = xc * lax.rsqrt(jnp.mean(xc * xc, axis=-1, keepdims=True) + EPS) * lng_ref[...] + lnb_ref[...]
    vb = v.astype(BF16)
    chunks = []
    for c in range(tm // GM_CHUNK):
        vc = vb[c * GM_CHUNK:(c + 1) * GM_CHUNK]
        parts = [jnp.dot(wmix_ref[g], vc[:, g * GM_GROUP_W:(g + 1) * GM_GROUP_W], preferred_element_type=F32)
                 for g in range(GM_GROUPS)]
        chunks.append(jnp.concatenate(parts, axis=1) + bexp_ref[...])
    s = jnp.concatenate(chunks, axis=0) if len(chunks) > 1 else chunks[0]
    y = jnp.dot((u * s).astype(BF16), wout_ref[...], preferred_element_type=F32)
    xo_ref[...] = x3 + mod_ref[2] * y.reshape(r, tl, d)

    @pl.when(pl.program_id(1) == pl.num_programs(1) - 1)
    def _():
        v_ref[...] = v.reshape(r, tl, D_GM)[:, tl - v_tail:, :]


def _gm_layer(x, mod, layer, row0, r, tl, g, w_in, ln_g, ln_b, w_s, b_s, w_out):
    nb, l, d = x.shape
    q = min(l, GM_CHUNK)
    assert l % q == 0 and GM_CHUNK % q == 0 and (r * tl) % GM_CHUNK == 0 and tl % q == 0
    rep = GM_CHUNK // q
    tri = jnp.tril(jnp.ones((q, q), F32))
    wq = w_s[:, :q, :q] * tri
    wmix = jnp.tile(wq, (1, rep, rep)) * jnp.kron(jnp.eye(rep, dtype=F32), jnp.ones((q, q), F32))
    bq = jnp.tile(b_s[:, :q].T, (rep, 1))
    bexp = jnp.repeat(bq, GM_GROUP_W, axis=1)
    v_tail = l - ((l - 1) // GM_CHUNK) * GM_CHUNK
    assert v_tail <= tl
    kern = functools.partial(_gm_kernel, v_tail=v_tail)
    xo, v = pl.pallas_call(
        kern,
        grid=(nb // r, l // tl),
        in_specs=[pl.BlockSpec((r, tl, d), lambda i, j: (i, j, 0)),
                  _mod_spec(layer, 0, r, row0),
                  _const_spec((1, d)),
                  _const_spec((d, 2 * D_GM)),
                  _const_spec((1, D_GM)),
                  _const_spec((1, D_GM)),
                  _const_spec((GM_GROUPS, GM_CHUNK, GM_CHUNK)),
                  _const_spec((GM_CHUNK, D_GM)),
                  _const_spec((D_GM, d))],
        out_specs=[pl.BlockSpec((r, tl, d), lambda i, j: (i, j, 0)),
                   pl.BlockSpec((r, v_tail, D_GM), lambda i, j: (i, 0, 0))],
        out_shape=[jax.ShapeDtypeStruct((nb, l, d), F32),
                   jax.ShapeDtypeStruct((nb, v_tail, D_GM), F32)],
        scratch_shapes=[pltpu.VMEM((r * tl, 2 * D_GM), F32)],
        compiler_params=_params("parallel", "arbitrary"),
        name="gmlp_mixer",
    )(x, mod, g.reshape(1, d), w_in.astype(BF16), ln_g.reshape(1, D_GM), ln_b.reshape(1, D_GM),
      wmix.astype(BF16), bexp, w_out.astype(BF16))
    return xo, v


def _mlp_kernel(x_ref, mod_ref, g_ref, w1_ref, w2_ref, gf_ref, o_ref, *, final):
    r, tl, d = x_ref.shape
    x3 = x_ref[...]
    h = _norm_modulate(x3, g_ref[...], mod_ref[0], mod_ref[1])
    acc = jnp.zeros((r * tl, d), F32)
    for k in range(D_FF // FF_CHUNK):
        a = jnp.dot(h, w1_ref[:, k * FF_CHUNK:(k + 1) * FF_CHUNK], preferred_element_type=F32)
        a = jnp.square(jnp.maximum(a, 0.0)).astype(BF16)
        acc = acc + jnp.dot(a, w2_ref[k * FF_CHUNK:(k + 1) * FF_CHUNK, :], preferred_element_type=F32)
    xo = x3 + mod_ref[2] * acc.reshape(r, tl, d)
    if final:
        xo = _rms(xo.reshape(r * tl, d), gf_ref[...]).reshape(r, tl, d)
    o_ref[...] = xo


def _layer_spec(layer, shape):
    return pl.BlockSpec((None,) + shape, lambda *_: (layer,) + (0,) * len(shape), pipeline_mode=pl.Buffered(1))


def _mlp_layer(x, mod, layer, row0, r, tl, g, w1_all, w2_all, final_g, final):
    nb, l, d = x.shape
    kern = functools.partial(_mlp_kernel, final=final)
    return pl.pallas_call(
        kern,
        grid=(nb // r, l // tl),
        in_specs=[pl.BlockSpec((r, tl, d), lambda i, j: (i, j, 0)),
                  _mod_spec(layer, 1, r, row0),
                  _const_spec((1, d)),
                  _layer_spec(layer, (d, D_FF)),
                  _layer_spec(layer, (D_FF, d)),
                  _const_spec((1, d))],
        out_specs=pl.BlockSpec((r, tl, d), lambda i, j: (i, j, 0)),
        out_shape=jax.ShapeDtypeStruct((nb, l, d), F32),
        compiler_params=_params("parallel", "parallel"),
        name="relu2_mlp",
    )(x, mod, g.reshape(1, d), w1_all, w2_all, final_g.reshape(1, d))


def _ssm_in_kernel(x_ref, mod_ref, g_ref, w_ref, cw_ref, cb_ref, dtb_ref, cs_ref,
                   z_ref, xbc_ref, dt_ref, cnew_ref, xp_ref):
    r, tl, d = x_ref.shape

    @pl.when(pl.program_id(1) == 0)
    def _():
        xp_ref[:, 0:CONV_PAD, :] = jnp.zeros((r, CONV_PAD, CONV_DIM), F32)
        xp_ref[:, CONV_PAD - (CONV_W - 1):CONV_PAD, :] = cs_ref[...]

    h = _norm_modulate(x_ref[...], g_ref[...], mod_ref[0], mod_ref[1])
    z_ref[...] = jnp.dot(h, w_ref[:, :D_INNER], preferred_element_type=F32).reshape(r, tl, D_INNER)
    xbc = jnp.dot(h, w_ref[:, D_INNER:D_ZX], preferred_element_type=F32)
    dt_raw = jnp.dot(h, w_ref[:, D_ZX:], preferred_element_type=F32)
    dt_ref[...] = _softplus(dt_raw + dtb_ref[...]).reshape(r, tl, SSM_HEADS)
    xp_ref[:, CONV_PAD:CONV_PAD + tl, :] = xbc.reshape(r, tl, CONV_DIM)
    regs = r * tl // SUBLANES
    cur = xp_ref[:, CONV_PAD:CONV_PAD + tl, :].reshape(regs, SUBLANES, CONV_DIM)
    prv = xp_ref[:, 0:tl, :].reshape(regs, SUBLANES, CONV_DIM)
    sub = lax.broadcasted_iota(jnp.int32, (1, SUBLANES, 1), 1)
    conv = cb_ref[...].reshape(1, 1, CONV_DIM) + cur * cw_ref[CONV_W - 1:CONV_W, :].reshape(1, 1, CONV_DIM)
    for s in range(1, CONV_W):
        shifted = pltpu.roll(jnp.where(sub < SUBLANES - s, cur, prv), s, axis=1)
        conv = conv + shifted * cw_ref[CONV_W - 1 - s:CONV_W - s, :].reshape(1, 1, CONV_DIM)
    xbc_ref[...] = _silu(conv).reshape(r, tl, CONV_DIM)
    cnew_ref[...] = xp_ref[:, tl + CONV_PAD - (CONV_W - 1):tl + CONV_PAD, :]
    xp_ref[:, 0:CONV_PAD, :] = xp_ref[:, tl:tl + CONV_PAD, :]


def _ssm_in(x, mod, layer, row0, r, tl, g, w_in, conv_w, conv_b, dt_bias, conv_state8):
    nb, l, d = x.shape
    return pl.pallas_call(
        _ssm_in_kernel,
        grid=(nb // r, l // tl),
        in_specs=[pl.BlockSpec((r, tl, d), lambda i, j: (i, j, 0)),
                  _mod_spec(layer, 0, r, row0),
                  _const_spec((1, d)),
                  _const_spec((d, D_ZX + SSM_HEADS)),
                  _const_spec((CONV_W, CONV_DIM)),
                  _const_spec((1, CONV_DIM)),
                  _const_spec((1, SSM_HEADS)),
                  pl.BlockSpec((r, CONV_W - 1, CONV_DIM), lambda i, j: (i, 0, 0))],
        out_specs=[pl.BlockSpec((r, tl, D_INNER), lambda i, j: (i, j, 0)),
                   pl.BlockSpec((r, tl, CONV_DIM), lambda i, j: (i, j, 0)),
                   pl.BlockSpec((r, tl, SSM_HEADS), lambda i, j: (i, j, 0)),
                   pl.BlockSpec((r, CONV_W - 1, CONV_DIM), lambda i, j: (i, 0, 0))],
        out_shape=[jax.ShapeDtypeStruct((nb, l, D_INNER), F32),
                   jax.ShapeDtypeStruct((nb, l, CONV_DIM), F32),
                   jax.ShapeDtypeStruct((nb, l, SSM_HEADS), F32),
                   jax.ShapeDtypeStruct((nb, CONV_W - 1, CONV_DIM), F32)],
        scratch_shapes=[pltpu.VMEM((r, CONV_PAD + tl, CONV_DIM), F32)],
        compiler_params=_params("parallel", "arbitrary"),
        name="ssm_in_conv",
    )(x, mod, g.reshape(1, d), w_in.astype(BF16), conv_w, conv_b.reshape(1, CONV_DIM),
      dt_bias.reshape(1, SSM_HEADS), conv_state8)


NT_DIMS = (((1,), (1,)), ((), ()))
TN_DIMS = (((0,), (0,)), ((), ()))


def _expand_heads(vals, e_bf):
    tm = vals[0].shape[0]
    v = jnp.concatenate(vals, axis=0)
    hi = v.astype(BF16)
    lo = (v - hi.astype(F32)).astype(BF16)
    x = jnp.dot(hi, e_bf, preferred_element_type=F32) + jnp.dot(lo, e_bf, preferred_element_type=F32)
    return [x[k * tm:(k + 1) * tm] for k in range(len(vals))]


def _head_masks():
    lane = lax.broadcasted_iota(jnp.int32, (1, SSM_GROUP_W), 1) // SSM_HEAD_DIM
    return [jnp.where(lane == hr, 1.0, 0.0).astype(BF16) for hr in range(HEADS_PER_GROUP)]


def _ssd_intra(g, cbm, acum, acum_t, xdt_g, hmask):
    acc = None
    for hr in range(HEADS_PER_GROUP):
        hh = g * HEADS_PER_GROUP + hr
        seg = jnp.minimum(acum[:, hh:hh + 1] - acum_t[hh:hh + 1, :], 0.0)
        lmat = (cbm * jnp.exp(seg)).astype(BF16)
        part = jnp.dot(lmat, xdt_g * hmask[hr], preferred_element_type=F32)
        acc = part if acc is None else acc + part
    return acc


def _cumsum_rows(lcum3, da):
    hi = da.astype(BF16)
    rest = da - hi.astype(F32)
    mid = rest.astype(BF16)
    lo = (rest - mid.astype(F32)).astype(BF16)
    return jnp.dot(lcum3, jnp.concatenate([hi, mid, lo], axis=0), preferred_element_type=F32)


def _gate_norm(y_g, z_g, ng_g):
    return _rms(y_g * _silu(z_g), ng_g)


def _ssm_prompt_kernel(x_ref, mod_ref, g_ref, w_ref, cw_ref, cb_ref, dtb_ref, alog_ref, dexp_ref, lcum_ref, e_ref,
                       ng_ref, wout_ref, xo_ref, cnew_ref, s_ref, xp_ref, z_ref, xc_ref, yn_ref, st_ref):
    _, tl, _ = x_ref.shape
    j = pl.program_id(1)

    @pl.when(j == 0)
    def _():
        xp_ref[0:CONV_PAD, :] = jnp.zeros((CONV_PAD, CONV_DIM), F32)
        st_ref[...] = jnp.zeros_like(st_ref)

    h = _norm_modulate(x_ref[...], g_ref[...], mod_ref[0], mod_ref[1])
    z_ref[...] = jnp.dot(h, w_ref[:, :D_INNER], preferred_element_type=F32)
    xp_ref[CONV_PAD:CONV_PAD + tl, :] = jnp.dot(h, w_ref[:, D_INNER:D_ZX], preferred_element_type=F32)
    dt_all = _softplus(jnp.dot(h, w_ref[:, D_ZX:], preferred_element_type=F32) + dtb_ref[...])

    sub = lax.broadcasted_iota(jnp.int32, (1, SUBLANES, 1), 1)
    for lb in range(CONV_DIM // LANE_TILE):
        lns = slice(lb * LANE_TILE, (lb + 1) * LANE_TILE)
        taps = [cw_ref[k:k + 1, lns].reshape(1, 1, LANE_TILE) for k in range(CONV_W)]
        bias = cb_ref[:, lns].reshape(1, 1, LANE_TILE)
        for rb in range(tl // ROW_TILE):
            r0 = rb * ROW_TILE
            cur = xp_ref[CONV_PAD + r0:CONV_PAD + r0 + ROW_TILE, lns]
            prv = xp_ref[r0:r0 + ROW_TILE, lns]
            cur = cur.reshape(ROW_TILE // SUBLANES, SUBLANES, LANE_TILE)
            prv = prv.reshape(ROW_TILE // SUBLANES, SUBLANES, LANE_TILE)
            conv = bias + cur * taps[CONV_W - 1]
            for s in range(1, CONV_W):
                conv = conv + pltpu.roll(jnp.where(sub < SUBLANES - s, cur, prv), s, axis=1) * taps[CONV_W - 1 - s]
            xc_ref[r0:r0 + ROW_TILE, lns] = _silu(conv).reshape(ROW_TILE, LANE_TILE)
    cnew_ref[0] = xp_ref[tl + CONV_PAD - (CONV_W - 1):tl + CONV_PAD, :]
    xp_ref[0:CONV_PAD, :] = xp_ref[tl:tl + CONV_PAD, :]

    a_neg = -jnp.exp(alog_ref[...])
    lcum3 = lcum_ref[...]
    causal = lcum3[:, :SSM_CHUNK].astype(F32) > 0.5
    hmask = _head_masks()
    for c in range(tl // SSM_CHUNK):
        rows = slice(c * SSM_CHUNK, (c + 1) * SSM_CHUNK)
        dt = dt_all[rows]
        acum = _cumsum_rows(lcum3, dt * a_neg)
        alast = acum[SSM_CHUNK - 1:SSM_CHUNK, :]
        acum2 = acum * LOG2E
        ldt_t = jnp.log2(dt).T
        src_t = acum2.T - ldt_t
        heads = jnp.concatenate([jnp.exp(acum), dt * jnp.exp(alast - acum)], axis=0)
        heads_hi = heads.astype(BF16)
        heads_lo = (heads - heads_hi.astype(F32)).astype(BF16)
        ex = jnp.dot(jnp.concatenate([heads_hi, heads_lo], axis=1), e_ref[...], preferred_element_type=F32)
        for g in range(SSM_GROUPS):
            ch = slice(g * SSM_GROUP_W, (g + 1) * SSM_GROUP_W)
            b_lo = D_INNER + g * D_STATE
            c_lo = D_INNER + (SSM_GROUPS + g) * D_STATE
            ea_x = ex[:SSM_CHUNK, ch]
            w_x = ex[SSM_CHUNK:, ch]
            xs = xc_ref[rows, ch]
            xs_bf = xs.astype(BF16)
            bgt = xc_ref[rows, b_lo:b_lo + D_STATE].T.astype(BF16)
            cg = xc_ref[rows, c_lo:c_lo + D_STATE].astype(BF16)
            cbm = jnp.where(causal, jnp.dot(cg, bgt, preferred_element_type=F32), 0.0)
            lmats = []
            for hr in range(HEADS_PER_GROUP):
                hh = g * HEADS_PER_GROUP + hr
                seg = jnp.minimum(acum2[:, hh:hh + 1] - src_t[hh:hh + 1, :], ldt_t[hh:hh + 1, :])
                lmats.append((cbm * jnp.exp2(seg)).astype(BF16))
            x_stack = jnp.concatenate([xs_bf * hmask[hr] for hr in range(HEADS_PER_GROUP)], axis=0)
            y_g = xs * dexp_ref[:, ch] + jnp.dot(jnp.concatenate(lmats, axis=1), x_stack,
                                                 preferred_element_type=F32)
            s_old = st_ref[:, ch]
            y_g = y_g + jnp.dot(cg, s_old.astype(BF16), preferred_element_type=F32) * ea_x
            st_ref[:, ch] = (s_old * ea_x[SSM_CHUNK - 1:SSM_CHUNK, :]
                             + jnp.dot(bgt, (xs * w_x).astype(BF16), preferred_element_type=F32))
            yn_ref[rows, ch] = _gate_norm(y_g, z_ref[rows, ch], ng_ref[:, ch]).astype(BF16)

    out = jnp.dot(yn_ref[...], wout_ref[...], preferred_element_type=F32)
    xo_ref[0] = x_ref[0] + mod_ref[2, 0] * out

    @pl.when(j == pl.num_programs(1) - 1)
    def _():
        s_ref[0] = st_ref[...].T


def _ssd_sample_kernel(xbc_ref, dt_ref, z_ref, alog_ref, dexp_ref, lcum_ref, bd_ref, e_ref, ng_ref, s0_ref,
                       yn_ref, s_ref):
    r, tl, _ = xbc_ref.shape
    tm = r * tl
    xbc = xbc_ref[...].reshape(tm, CONV_DIM)
    xs = xbc[:, :D_INNER]
    dt = dt_ref[...].reshape(tm, SSM_HEADS)
    da = dt * (-jnp.exp(alog_ref[...]))
    lcum = lcum_ref[...]
    causal = lcum > 0.5
    acum = jnp.dot(lcum, da, precision=HIGHEST, preferred_element_type=F32)
    alast = jnp.dot(bd_ref[...], da, precision=HIGHEST, preferred_element_type=F32)
    acum_t = acum.T
    dec_end_t = jnp.exp(alast.T)
    dt_x, ea_x, w_x = _expand_heads([dt, jnp.exp(acum), dt * jnp.exp(alast - acum)], e_ref[...])
    xdt = (xs * dt_x).astype(BF16)
    xw = xs * w_x
    hmask = _head_masks()
    row = lax.broadcasted_iota(jnp.int32, (tm, 1), 0) // tl
    dec_cols = [jnp.broadcast_to(dec_end_t[:, b * tl:b * tl + 1], (SSM_HEADS, D_STATE)) for b in range(r)]
    for g in range(SSM_GROUPS):
        ch = slice(g * SSM_GROUP_W, (g + 1) * SSM_GROUP_W)
        bg = xbc[:, D_INNER + g * D_STATE:D_INNER + (g + 1) * D_STATE].astype(BF16)
        cg32 = xbc[:, D_INNER + (SSM_GROUPS + g) * D_STATE:D_INNER + (SSM_GROUPS + g + 1) * D_STATE]
        cbm = jnp.where(causal, lax.dot_general(cg32.astype(BF16), bg, NT_DIMS, preferred_element_type=F32), 0.0)
        y_g = xs[:, ch] * dexp_ref[:, ch] + _ssd_intra(g, cbm, acum, acum_t, xdt[:, ch], hmask)
        y_off = jnp.zeros((tm, SSM_GROUP_W), F32)
        xw_g = xw[:, ch]
        for b in range(r):
            s_old = s0_ref[b, ch, :]
            c_b = jnp.where(row == b, cg32, 0.0).astype(BF16)
            xw_b = jnp.where(row == b, xw_g, 0.0).astype(BF16)
            y_off = y_off + lax.dot_general(c_b, s_old.astype(BF16), NT_DIMS, preferred_element_type=F32)
            upd = lax.dot_general(xw_b, bg, TN_DIMS, preferred_element_type=F32)
            dec = jnp.concatenate(
                [jnp.broadcast_to(dec_cols[b][g * HEADS_PER_GROUP + hr:g * HEADS_PER_GROUP + hr + 1, :],
                                  (SSM_HEAD_DIM, D_STATE)) for hr in range(HEADS_PER_GROUP)], axis=0)
            s_ref[b, ch, :] = s_old * dec + upd
        z_g = z_ref[:, :, ch].reshape(tm, SSM_GROUP_W)
        yn_g = _gate_norm(y_g + y_off * ea_x[:, ch], z_g, ng_ref[:, ch])
        yn_ref[:, :, ch] = yn_g.reshape(r, tl, SSM_GROUP_W).astype(BF16)


def _ssd_consts(d_skip):
    e_bf = jnp.repeat(jnp.eye(SSM_HEADS, dtype=F32), SSM_HEAD_DIM, axis=1).astype(BF16)
    dexp = jnp.repeat(d_skip, SSM_HEAD_DIM).reshape(1, D_INNER)
    return e_bf, dexp


def _ssm_prompt(x, mod, layer, row0, tl, g, w_in, conv_w, conv_b, dt_bias, a_log, d_skip, norm_g, w_out):
    nb, l, d = x.shape
    assert l % tl == 0 and tl % SSM_CHUNK == 0 and tl % ROW_TILE == 0
    e_bf, dexp = _ssd_consts(d_skip)
    return pl.pallas_call(
        _ssm_prompt_kernel,
        grid=(nb, l // tl),
        in_specs=[pl.BlockSpec((1, tl, d), lambda i, j: (i, j, 0)),
                  _mod_spec(layer, 0, 1, row0),
                  _const_spec((1, d)),
                  _const_spec((d, D_ZX + SSM_HEADS)),
                  _const_spec((CONV_W, CONV_DIM)),
                  _const_spec((1, CONV_DIM)),
                  _const_spec((1, SSM_HEADS)),
                  _const_spec((1, SSM_HEADS)),
                  _const_spec((1, D_INNER)),
                  _const_spec((SSM_CHUNK, 3 * SSM_CHUNK)),
                  _const_spec((2 * SSM_HEADS, D_INNER)),
                  _const_spec((1, D_INNER)),
                  _const_spec((D_INNER, d))],
        out_specs=[pl.BlockSpec((1, tl, d), lambda i, j: (i, j, 0)),
                   pl.BlockSpec((1, CONV_W - 1, CONV_DIM), lambda i, j: (i, 0, 0)),
                   pl.BlockSpec((1, D_INNER, D_STATE), lambda i, j: (i, 0, 0))],
        out_shape=[jax.ShapeDtypeStruct((nb, l, d), F32),
                   jax.ShapeDtypeStruct((nb, CONV_W - 1, CONV_DIM), F32),
                   jax.ShapeDtypeStruct((nb, D_INNER, D_STATE), F32)],
        scratch_shapes=[pltpu.VMEM((CONV_PAD + tl, CONV_DIM), F32),
                        pltpu.VMEM((tl, D_INNER), F32),
                        pltpu.VMEM((tl, CONV_DIM), F32),
                        pltpu.VMEM((tl, D_INNER), BF16),
                        pltpu.VMEM((D_STATE, D_INNER), F32)],
        compiler_params=_params("parallel", "arbitrary"),
        name="ssm_prompt",
    )(x, mod, g.reshape(1, d), w_in.astype(BF16), conv_w, conv_b.reshape(1, CONV_DIM),
      dt_bias.reshape(1, SSM_HEADS), a_log.reshape(1, SSM_HEADS), dexp,
      jnp.tile(jnp.tril(jnp.ones((SSM_CHUNK, SSM_CHUNK), BF16)), (1, 3)), jnp.concatenate([e_bf, e_bf], axis=0),
      norm_g.reshape(1, D_INNER), w_out.astype(BF16))


def _ssd_sample(xbc, dt, z, a_log, d_skip, norm_g, state0, r):
    nb, l, _ = xbc.shape
    tm = r * l
    assert l <= SSM_CHUNK and nb % r == 0
    eye_r = jnp.eye(r, dtype=F32)
    lcum = jnp.kron(eye_r, jnp.tril(jnp.ones((l, l), F32)))
    bd = jnp.kron(eye_r, jnp.ones((l, l), F32))
    e_bf, dexp = _ssd_consts(d_skip)
    return pl.pallas_call(
        _ssd_sample_kernel,
        grid=(nb // r,),
        in_specs=[pl.BlockSpec((r, l, CONV_DIM), lambda i: (i, 0, 0)),
                  pl.BlockSpec((r, l, SSM_HEADS), lambda i: (i, 0, 0)),
                  pl.BlockSpec((r, l, D_INNER), lambda i: (i, 0, 0)),
                  _const_spec((1, SSM_HEADS)),
                  _const_spec((1, D_INNER)),
                  _const_spec((tm, tm)),
                  _const_spec((tm, tm)),
                  _const_spec((SSM_HEADS, D_INNER)),
                  _const_spec((1, D_INNER)),
                  pl.BlockSpec((r, D_INNER, D_STATE), lambda i: (i, 0, 0))],
        out_specs=[pl.BlockSpec((r, l, D_INNER), lambda i: (i, 0, 0)),
                   pl.BlockSpec((r, D_INNER, D_STATE), lambda i: (i, 0, 0))],
        out_shape=[jax.ShapeDtypeStruct((nb, l, D_INNER), BF16),
                   jax.ShapeDtypeStruct((nb, D_INNER, D_STATE), F32)],
        compiler_params=_params("parallel"),
        name="ssd_scan_sample",
    )(xbc, dt, z, a_log.reshape(1, SSM_HEADS), dexp, lcum, bd, e_bf, norm_g.reshape(1, D_INNER), state0)


def _ssm_out_kernel(x_ref, yn_ref, mod_ref, wout_ref, o_ref):
    r, tl, d = x_ref.shape
    if r == 1:
        yn = yn_ref[0]
    else:
        yn = yn_ref[...].astype(F32).reshape(r * tl, D_INNER).astype(BF16)
    out = jnp.dot(yn, wout_ref[...], preferred_element_type=F32)
    o_ref[...] = x_ref[...] + mod_ref[2] * out.reshape(r, tl, d)


def _ssm_out(x, yn, mod, layer, row0, r, tl, w_out):
    nb, l, d = x.shape
    return pl.pallas_call(
        _ssm_out_kernel,
        grid=(nb // r, l // tl),
        in_specs=[pl.BlockSpec((r, tl, d), lambda i, j: (i, j, 0)),
                  pl.BlockSpec((r, tl, D_INNER), lambda i, j: (i, j, 0)),
                  _mod_spec(layer, 0, r, row0),
                  _const_spec((D_INNER, d))],
        out_specs=pl.BlockSpec((r, tl, d), lambda i, j: (i, j, 0)),
        out_shape=jax.ShapeDtypeStruct((nb, l, d), F32),
        compiler_params=_params("parallel", "parallel"),
        name="ssm_out_proj",
    )(x, yn, mod, w_out.astype(BF16))


def _trunk(x, mod, row0, tiles, ssm_state, conv_state, p):
    nb, l, _ = x.shape
    new_v, new_ssm, new_conv = [], [], []
    for i in range(DEPTH):
        j = i // 2
        if i % 2 == 0:
            x, v = _gm_layer(x, mod, i, row0, *tiles["gm"], p["norm1_g"][i], p["gm_w_in"][j], p["gm_ln_g"][j],
                             p["gm_ln_b"][j], p["gm_w_s"][j], p["gm_b_s"][j], p["gm_w_out"][j])
            new_v.append(v)
        else:
            if ssm_state is None:
                x, cnew, s_new = _ssm_prompt(x, mod, i, row0, tiles["ssm"], p["norm1_g"][i], p["ssm_w_in"][j],
                                             p["ssm_conv_w"][j], p["ssm_conv_b"][j], p["ssm_dt_bias"][j],
                                             p["ssm_a_log"][j], p["ssm_d"][j], p["ssm_norm_g"][j],
                                             p["ssm_w_out"][j])
            else:
                z, xbc, dt, cnew = _ssm_in(x, mod, i, row0, *tiles["ssm_in"], p["norm1_g"][i], p["ssm_w_in"][j],
                                           p["ssm_conv_w"][j], p["ssm_conv_b"][j], p["ssm_dt_bias"][j],
                                           conv_state[j])
                s0 = ssm_state[j].reshape(nb, D_INNER, D_STATE)
                yn, s_new = _ssd_sample(xbc, dt, z, p["ssm_a_log"][j], p["ssm_d"][j], p["ssm_norm_g"][j], s0,
                                        tiles["ssd"])
                x = _ssm_out(x, yn, mod, i, row0, *tiles["ssm_out"], p["ssm_w_out"][j])
            new_conv.append(cnew)
            new_ssm.append(s_new.reshape(nb, SSM_HEADS, SSM_HEAD_DIM, D_STATE))
        x = _mlp_layer(x, mod, i, row0, *tiles["mlp"], p["norm2_g"][i], p["mlp_w1_bf"], p["mlp_w2_bf"],
                       p["final_g"], final=(i == DEPTH - 1))
    return x, jnp.stack(new_v), jnp.stack(new_ssm), jnp.stack(new_conv)


PROMPT_TILES = {"gm": (1, 512), "mlp": (1, 1024), "ssm": 512}
SAMPLE_TILES = {"gm": (64, 8), "mlp": (64, 8), "ssm_in": (32, 8), "ssd": 8, "ssm_out": (64, 8)}


def kernel(x_prompt, x_sample, c_prompt, c_sample, state_ssm, state_conv, ada_w, ada_b, norm1_g, norm2_g, gm_w_in, gm_ln_g, gm_ln_b, gm_w_s, gm_b_s, gm_w_out, ssm_w_in, ssm_conv_w, ssm_conv_b, ssm_dt_bias, ssm_a_log, ssm_d, ssm_norm_g, ssm_w_out, mlp_w1, mlp_w2, final_g):
    p = dict(norm1_g=norm1_g, norm2_g=norm2_g, gm_w_in=gm_w_in, gm_ln_g=gm_ln_g, gm_ln_b=gm_ln_b, gm_w_s=gm_w_s,
             gm_b_s=gm_b_s, gm_w_out=gm_w_out, ssm_w_in=ssm_w_in, ssm_conv_w=ssm_conv_w, ssm_conv_b=ssm_conv_b,
             ssm_dt_bias=ssm_dt_bias, ssm_a_log=ssm_a_log, ssm_d=ssm_d, ssm_norm_g=ssm_norm_g, ssm_w_out=ssm_w_out,
             mlp_w1_bf=mlp_w1.astype(BF16), mlp_w2_bf=mlp_w2.astype(BF16), final_g=final_g)
    n_sample = x_sample.shape[0]
    mod = _ada(jnp.concatenate([c_sample, c_prompt], axis=0), ada_w, ada_b)
    y_p, v_p, ssm_p, conv_p = _trunk(x_prompt, mod, n_sample, PROMPT_TILES, None, None, p)
    y_s, v_s, ssm_s, conv_s = _trunk(x_sample, mod, 0, SAMPLE_TILES, state_ssm, state_conv, p)
    return (y_p, y_s, v_p, v_s, ssm_p, conv_p, ssm_s, conv_s)
```

```python
import functools

import jax
import jax.numpy as jnp
from jax import lax
from jax.experimental import pallas as pl
from jax.experimental.pallas import tpu as pltpu

F32 = jnp.float32
BF16 = jnp.bfloat16
HIGHEST = lax.Precision.HIGHEST

D_MODEL = 1024
DEPTH = 2
N_MOD = 6
EPS = 1e-6
LOG2E = 1.4426950408889634
GM_CHUNK = 128
D_GM = 2 * D_MODEL
GM_GROUPS = 8
GM_GROUP_W = D_GM // GM_GROUPS
D_INNER = 2 * D_MODEL
SSM_HEAD_DIM = 64
SSM_HEADS = D_INNER // SSM_HEAD_DIM
SSM_GROUPS = 8
HEADS_PER_GROUP = SSM_HEADS // SSM_GROUPS
SSM_GROUP_W = D_INNER // SSM_GROUPS
D_STATE = 128
CONV_W = 4
CONV_DIM = D_INNER + 2 * SSM_GROUPS * D_STATE
D_ZX = D_INNER + CONV_DIM
SSM_CHUNK = 128
D_FF = 4 * D_MODEL
FF_CHUNK = 1024
SUBLANES = 8
CONV_PAD = SUBLANES
ROW_TILE = 256
LANE_TILE = 512
VMEM_LIMIT = 56 * 1024 * 1024


def _silu(x):
    return x * (1.0 / (1.0 + jnp.exp2(x * (-LOG2E))))


def _softplus(x):
    return jnp.maximum(x, 0.0) + jnp.log1p(jnp.exp(-jnp.abs(x)))


def _gelu_tanh(x):
    return x * (0.5 * (1.0 + jnp.tanh(0.7978845608028654 * (x + 0.044715 * (x * x * x)))))


def _rms(x, g):
    return x * lax.rsqrt(jnp.mean(x * x, axis=-1, keepdims=True) + EPS) * g


def _norm_modulate(x3, g, shift, scale):
    r, tl, d = x3.shape
    hn = _rms(x3.reshape(r * tl, d), g)
    h3 = hn.reshape(r, tl, d) * (1.0 + scale) + shift
    return h3.reshape(r * tl, d).astype(BF16)


def _const_spec(shape):
    return pl.BlockSpec(shape, lambda *_: (0,) * len(shape), pipeline_mode=pl.Buffered(1))


def _params(*sem):
    return pltpu.CompilerParams(dimension_semantics=sem, vmem_limit_bytes=VMEM_LIMIT)


def _ada_kernel(c_ref, w_ref, b_ref, o_ref):
    sc = _silu(c_ref[...]).astype(BF16)
    res = jnp.dot(sc, w_ref[...].astype(BF16), preferred_element_type=F32) + b_ref[...]
    o_ref[...] = res.reshape(o_ref.shape)


def _ada(c_all, ada_w, ada_b):
    nb = c_all.shape[0]
    return pl.pallas_call(
        _ada_kernel,
        grid=(DEPTH, N_MOD),
        in_specs=[pl.BlockSpec((nb, D_MODEL), lambda i, k: (0, 0)),
                  pl.BlockSpec((None, D_MODEL, D_MODEL), lambda i, k: (i, 0, k)),
                  pl.BlockSpec((None, None, 1, D_MODEL), lambda i, k: (i, k, 0, 0))],
        out_specs=pl.BlockSpec((None, None, nb, 1, D_MODEL), lambda i, k: (i, k, 0, 0, 0)),
        out_shape=jax.ShapeDtypeStruct((DEPTH, N_MOD, nb, 1, D_MODEL), F32),
        compiler_params=_params("parallel", "parallel"),
        name="ada_mod",
    )(c_all, ada_w, ada_b.reshape(DEPTH, N_MOD, 1, D_MODEL))


def _mod_spec(layer, half, r, row0):
    return pl.BlockSpec((None, 3, r, 1, D_MODEL), lambda i, j: (layer, half, row0 // r + i, 0, 0))


def _gm_kernel(x_ref, mod_ref, g_ref, win_ref, lng_ref, lnb_ref, wmix_ref, bexp_ref, wout_ref,
               xo_ref, v_ref, z_ref, *, v_tail):
    r, tl, d = x_ref.shape
    tm = r * tl
    x3 = x_ref[...]
    h = _norm_modulate(x3, g_ref[...], mod_ref[0], mod_ref[1])
    for k in range(2 * D_GM // FF_CHUNK):
        cols = slice(k * FF_CHUNK, (k + 1) * FF_CHUNK)
        z_ref[:, cols] = _gelu_tanh(jnp.dot(h, win_ref[:, cols], preferred_element_type=F32))
    u = z_ref[:, :D_GM]
    vr = z_ref[:, D_GM:]
    xc = vr - jnp.mean(vr, axis=-1, keepdims=True)
    v = xc * lax.rsqrt(jnp.mean(xc * xc, axis=-1, keepdims=True) + EPS) * lng_ref[...] + lnb_ref[...]
    vb = v.astype(BF16)
    chunks = []
    for c in range(tm // GM_CHUNK):
        vc = vb[c * GM_CHUNK:(c + 1) * GM_CHUNK]
        parts = [jnp.dot(wmix_ref[g], vc[:, g * GM_GROUP_W:(g + 1) * GM_GROUP_W], preferred_element_type=F32)
                 for g in range(GM_GROUPS)]
        chunks.append(jnp.concatenate(parts, axis=1) + bexp_ref[...])
    s = jnp.concatenate(chunks, axis=0) if len(chunks) > 1 else chunks[0]
    y = jnp.dot((u * s).astype(BF16), wout_ref[...], preferred_element_type=F32)
    xo_ref[...] = x3 + mod_ref[2] * y.reshape(r, tl, d)

    @pl.when(pl.program_id(1) == pl.num_programs(1) - 1)
    def _():
        v_ref[...] = v.reshape(r, tl, D_GM)[:, tl - v_tail:, :]


def _gm_layer(x, mod, layer, row0, r, tl, g, w_in, ln_g, ln_b, w_s, b_s, w_out):
    nb, l, d = x.shape
    q = min(l, GM_CHUNK)
    assert l % q == 0 and GM_CHUNK % q == 0 and (r * tl) % GM_CHUNK == 0 and tl % q == 0
    rep = GM_CHUNK // q
    tri = jnp.tril(jnp.ones((q, q), F32))
    wq = w_s[:, :q, :q] * tri
    wmix = jnp.einsum("ab,gts->gatbs", jnp.eye(rep, dtype=F32), wq).reshape(GM_GROUPS, GM_CHUNK, GM_CHUNK)
    bq = jnp.tile(b_s[:, :q].T, (rep, 1))
    bexp = jnp.repeat(bq, GM_GROUP_W, axis=1)
    v_tail = l - ((l - 1) // GM_CHUNK) * GM_CHUNK
    assert v_tail <= tl
    kern = functools.partial(_gm_kernel, v_tail=v_tail)
    xo, v = pl.pallas_call(
        kern,
        grid=(nb // r, l // tl),
        in_specs=[pl.BlockSpec((r, tl, d), lambda i, j: (i, j, 0)),
                  _mod_spec(layer, 0, r, row0),
                  _const_spec((1, d)),
                  _const_spec((d, 2 * D_GM)),
                  _const_spec((1, D_GM)),
                  _const_spec((1, D_GM)),
                  _const_spec((GM_GROUPS, GM_CHUNK, GM_CHUNK)),
                  _const_spec((GM_CHUNK, D_GM)),
                  _const_spec((D_GM, d))],
        out_specs=[pl.BlockSpec((r, tl, d), lambda i, j: (i, j, 0)),
                   pl.BlockSpec((r, v_tail, D_GM), lambda i, j: (i, 0, 0))],
        out_shape=[jax.ShapeDtypeStruct((nb, l, d), F32),
                   jax.ShapeDtypeStruct((nb, v_tail, D_GM), F32)],
        scratch_shapes=[pltpu.VMEM((r * tl, 2 * D_GM), F32)],
        compiler_params=_params("parallel", "arbitrary"),
        name="gmlp_mixer",
    )(x, mod, g.reshape(1, d), w_in.astype(BF16), ln_g.reshape(1, D_GM), ln_b.reshape(1, D_GM),
      wmix.astype(BF16), bexp, w_out.astype(BF16))
    return xo, v


def _mlp_kernel(x_ref, mod_ref, g_ref, w1_ref, w2_ref, gf_ref, o_ref, *, final):
    r, tl, d = x_ref.shape
    x3 = x_ref[...]
    h = _norm_modulate(x3, g_ref[...], mod_ref[0], mod_ref[1])
    acc = jnp.zeros((r * tl, d), F32)
    for k in range(D_FF // FF_CHUNK):
        a = jnp.dot(h, w1_ref[:, k * FF_CHUNK:(k + 1) * FF_CHUNK], preferred_element_type=F32)
        a = jnp.square(jnp.maximum(a, 0.0)).astype(BF16)
        acc = acc + jnp.dot(a, w2_ref[k * FF_CHUNK:(k + 1) * FF_CHUNK, :], preferred_element_type=F32)
    xo = x3 + mod_ref[2] * acc.reshape(r, tl, d)
    if final:
        xo = _rms(xo.reshape(r * tl, d), gf_ref[...]).reshape(r, tl, d)
    o_ref[...] = xo


def _layer_spec(layer, shape):
    return pl.BlockSpec((None,) + shape, lambda *_: (layer,) + (0,) * len(shape), pipeline_mode=pl.Buffered(1))


def _mlp_layer(x, mod, layer, row0, r, tl, g, w1_all, w2_all, final_g, final):
    nb, l, d = x.shape
    kern = functools.partial(_mlp_kernel, final=final)
    return pl.pallas_call(
        kern,
        grid=(nb // r, l // tl),
        in_specs=[pl.BlockSpec((r, tl, d), lambda i, j: (i, j, 0)),
                  _mod_spec(layer, 1, r, row0),
                  _const_spec((1, d)),
                  _layer_spec(layer, (d, D_FF)),
                  _layer_spec(layer, (D_FF, d)),
                  _const_spec((1, d))],
        out_specs=pl.BlockSpec((r, tl, d), lambda i, j: (i, j, 0)),
        out_shape=jax.ShapeDtypeStruct((nb, l, d), F32),
        compiler_params=_params("parallel", "parallel"),
        name="relu2_mlp",
    )(x, mod, g.reshape(1, d), w1_all, w2_all, final_g.reshape(1, d))


def _ssm_in_kernel(x_ref, mod_ref, g_ref, w_ref, cw_ref, cb_ref, dtb_ref, cs_ref,
                   z_ref, xbc_ref, dt_ref, cnew_ref, xp_ref):
    r, tl, d = x_ref.shape

    @pl.when(pl.program_id(1) == 0)
    def _():
        xp_ref[:, 0:CONV_PAD, :] = jnp.zeros((r, CONV_PAD, CONV_DIM), F32)
        xp_ref[:, CONV_PAD - (CONV_W - 1):CONV_PAD, :] = cs_ref[...]

    h = _norm_modulate(x_ref[...], g_ref[...], mod_ref[0], mod_ref[1])
    z_ref[...] = jnp.dot(h, w_ref[:, :D_INNER], preferred_element_type=F32).reshape(r, tl, D_INNER)
    xbc = jnp.dot(h, w_ref[:, D_INNER:D_ZX], preferred_element_type=F32)
    dt_raw = jnp.dot(h, w_ref[:, D_ZX:], preferred_element_type=F32)
    dt_ref[...] = _softplus(dt_raw + dtb_ref[...]).reshape(r, tl, SSM_HEADS)
    xp_ref[:, CONV_PAD:CONV_PAD + tl, :] = xbc.reshape(r, tl, CONV_DIM)
    regs = r * tl // SUBLANES
    cur = xp_ref[:, CONV_PAD:CONV_PAD + tl, :].reshape(regs, SUBLANES, CONV_DIM)
    prv = xp_ref[:, 0:tl, :].reshape(regs, SUBLANES, CONV_DIM)
    sub = lax.broadcasted_iota(jnp.int32, (1, SUBLANES, 1), 1)
    conv = cb_ref[...].reshape(1, 1, CONV_DIM) + cur * cw_ref[CONV_W - 1:CONV_W, :].reshape(1, 1, CONV_DIM)
    for s in range(1, CONV_W):
        shifted = pltpu.roll(jnp.where(sub < SUBLANES - s, cur, prv), s, axis=1)
        conv = conv + shifted * cw_ref[CONV_W - 1 - s:CONV_W - s, :].reshape(1, 1, CONV_DIM)
    xbc_ref[...] = _silu(conv).reshape(r, tl, CONV_DIM)
    cnew_ref[...] = xp_ref[:, tl + CONV_PAD - (CONV_W - 1):tl + CONV_PAD, :]
    xp_ref[:, 0:CONV_PAD, :] = xp_ref[:, tl:tl + CONV_PAD, :]


def _ssm_in(x, mod, layer, row0, r, tl, g, w_in, conv_w, conv_b, dt_bias, conv_state8):
    nb, l, d = x.shape
    return pl.pallas_call(
        _ssm_in_kernel,
        grid=(nb // r, l // tl),
        in_specs=[pl.BlockSpec((r, tl, d), lambda i, j: (i, j, 0)),
                  _mod_spec(layer, 0, r, row0),
                  _const_spec((1, d)),
                  _const_spec((d, D_ZX + SSM_HEADS)),
                  _const_spec((CONV_W, CONV_DIM)),
                  _const_spec((1, CONV_DIM)),
                  _const_spec((1, SSM_HEADS)),
                  pl.BlockSpec((r, CONV_W - 1, CONV_DIM), lambda i, j: (i, 0, 0))],
        out_specs=[pl.BlockSpec((r, tl, D_INNER), lambda i, j: (i, j, 0)),
                   pl.BlockSpec((r, tl, CONV_DIM), lambda i, j: (i, j, 0)),
                   pl.BlockSpec((r, tl, SSM_HEADS), lambda i, j: (i, j, 0)),
                   pl.BlockSpec((r, CONV_W - 1, CONV_DIM), lambda i, j: (i, 0, 0))],
        out_shape=[jax.ShapeDtypeStruct((nb, l, D_INNER), F32),
                   jax.ShapeDtypeStruct((nb, l, CONV_DIM), F32),
                   jax.ShapeDtypeStruct((nb, l, SSM_HEADS), F32),
                   jax.ShapeDtypeStruct((nb, CONV_W - 1, CONV_DIM), F32)],
        scratch_shapes=[pltpu.VMEM((r, CONV_PAD + tl, CONV_DIM), F32)],
        compiler_params=_params("parallel", "arbitrary"),
        name="ssm_in_conv",
    )(x, mod, g.reshape(1, d), w_in.astype(BF16), conv_w, conv_b.reshape(1, CONV_DIM),
      dt_bias.reshape(1, SSM_HEADS), conv_state8)


NT_DIMS = (((1,), (1,)), ((), ()))
TN_DIMS = (((0,), (0,)), ((), ()))


def _expand_heads(vals, e_bf):
    tm = vals[0].shape[0]
    v = jnp.concatenate(vals, axis=0)
    hi = v.astype(BF16)
    lo = (v - hi.astype(F32)).astype(BF16)
    x = jnp.dot(hi, e_bf, preferred_element_type=F32) + jnp.dot(lo, e_bf, preferred_element_type=F32)
    return [x[k * tm:(k + 1) * tm] for k in range(len(vals))]


def _head_masks():
    lane = lax.broadcasted_iota(jnp.int32, (1, SSM_GROUP_W), 1) // SSM_HEAD_DIM
    return [jnp.where(lane == hr, 1.0, 0.0).astype(BF16) for hr in range(HEADS_PER_GROUP)]


def _ssd_intra(g, cbm, acum, acum_t, xdt_g, hmask):
    acc = None
    for hr in range(HEADS_PER_GROUP):
        hh = g * HEADS_PER_GROUP + hr
        seg = jnp.minimum(acum[:, hh:hh + 1] - acum_t[hh:hh + 1, :], 0.0)
        lmat = (cbm * jnp.exp(seg)).astype(BF16)
        part = jnp.dot(lmat, xdt_g * hmask[hr], preferred_element_type=F32)
        acc = part if acc is None else acc + part
    return acc


def _cumsum_rows(lcum3, da):
    hi = da.astype(BF16)
    rest = da - hi.astype(F32)
    mid = rest.astype(BF16)
    lo = (rest - mid.astype(F32)).astype(BF16)
    return jnp.dot(lcum3, jnp.concatenate([hi, mid, lo], axis=0), preferred_element_type=F32)


def _gate_norm(y_g, z_g, ng_g):
    return _rms(y_g * _silu(z_g), ng_g)


def _ssm_prompt_kernel(x_ref, mod_ref, g_ref, w_ref, cw_ref, cb_ref, dtb_ref, alog_ref, dexp_ref, lcum_ref, e_ref,
                       ng_ref, wout_ref, xo_ref, cnew_ref, s_ref, xp_ref, z_ref, xc_ref, yn_ref, st_ref):
    _, tl, _ = x_ref.shape
    j = pl.program_id(1)

    @pl.when(j == 0)
    def _():
        xp_ref[0:CONV_PAD, :] = jnp.zeros((CONV_PAD, CONV_DIM), F32)
        st_ref[...] = jnp.zeros_like(st_ref)

    h = _norm_modulate(x_ref[...], g_ref[...], mod_ref[0], mod_ref[1])
    z_ref[...] = jnp.dot(h, w_ref[:, :D_INNER], preferred_element_type=F32)
    xp_ref[CONV_PAD:CONV_PAD + tl, :] = jnp.dot(h, w_ref[:, D_INNER:D_ZX], preferred_element_type=F32)
    dt_all = _softplus(jnp.dot(h, w_ref[:, D_ZX:], preferred_element_type=F32) + dtb_ref[...])

    sub = lax.broadcasted_iota(jnp.int32, (1, SUBLANES, 1), 1)
    for lb in range(CONV_DIM // LANE_TILE):
        lns = slice(lb * LANE_TILE, (lb + 1) * LANE_TILE)
        taps = [cw_ref[k:k + 1, lns].reshape(1, 1, LANE_TILE) for k in range(CONV_W)]
        bias = cb_ref[:, lns].reshape(1, 1, LANE_TILE)
        for rb in range(tl // ROW_TILE):
            r0 = rb * ROW_TILE
            cur = xp_ref[CONV_PAD + r0:CONV_PAD + r0 + ROW_TILE, lns]
            prv = xp_ref[r0:r0 + ROW_TILE, lns]
            cur = cur.reshape(ROW_TILE // SUBLANES, SUBLANES, LANE_TILE)
            prv = prv.reshape(ROW_TILE // SUBLANES, SUBLANES, LANE_TILE)
            conv = bias + cur * taps[CONV_W - 1]
            for s in range(1, CONV_W):
                conv = conv + pltpu.roll(jnp.where(sub < SUBLANES - s, cur, prv), s, axis=1) * taps[CONV_W - 1 - s]
            xc_ref[r0:r0 + ROW_TILE, lns] = _silu(conv).reshape(ROW_TILE, LANE_TILE)
    cnew_ref[0] = xp_ref[tl + CONV_PAD - (CONV_W - 1):tl + CONV_PAD, :]
    xp_ref[0:CONV_PAD, :] = xp_ref[tl:tl + CONV_PAD, :]

    a_neg = -jnp.exp(alog_ref[...])
    lcum3 = lcum_ref[...]
    causal = lcum3[:, :SSM_CHUNK].astype(F32) > 0.5
    hmask = _head_masks()
    for c in range(tl // SSM_CHUNK):
        rows = slice(c * SSM_CHUNK, (c + 1) * SSM_CHUNK)
        dt = dt_all[rows]
        acum = _cumsum_rows(lcum3, dt * a_neg)
        alast = acum[SSM_CHUNK - 1:SSM_CHUNK, :]
        acum2 = acum * LOG2E
        ldt_t = jnp.log2(dt).T
        src_t = acum2.T - ldt_t
        heads = jnp.concatenate([jnp.exp(acum), dt * jnp.exp(alast - acum)], axis=0)
        heads_hi = heads.astype(BF16)
        heads_lo = (heads - heads_hi.astype(F32)).astype(BF16)
        ex = jnp.dot(jnp.concatenate([heads_hi, heads_lo], axis=1), e_ref[...], preferred_element_type=F32)
        for g in range(SSM_GROUPS):
            ch = slice(g * SSM_GROUP_W, (g + 1) * SSM_GROUP_W)
            ea_x = ex[:SSM_CHUNK, ch]
            w_x = ex[SSM_CHUNK:, ch]
            xs = xc_ref[rows, ch]
            xs_bf = xs.astype(BF16)
            if g % 2 == 0:
                b_lo = D_INNER + g * D_STATE
                c_lo = D_INNER + (SSM_GROUPS + g) * D_STATE
                bgt_pair = [xc_ref[rows, b_lo + k * D_STATE:b_lo + (k + 1) * D_STATE].T.astype(BF16) for k in (0, 1)]
                cg_pair = [xc_ref[rows, c_lo + k * D_STATE:c_lo + (k + 1) * D_STATE].astype(BF16) for k in (0, 1)]
                zero = jnp.zeros((D_STATE, SSM_CHUNK), BF16)
                bdiag = jnp.concatenate([jnp.concatenate([bgt_pair[0], zero], axis=1),
                                         jnp.concatenate([zero, bgt_pair[1]], axis=1)], axis=0)
                cb_pair = jnp.dot(jnp.concatenate(cg_pair, axis=1), bdiag, preferred_element_type=F32)
            bgt = bgt_pair[g % 2]
            cg = cg_pair[g % 2]
            cbm = jnp.where(causal, cb_pair[:, (g % 2) * SSM_CHUNK:(g % 2 + 1) * SSM_CHUNK], 0.0)
            lmats = []
            for hr in range(HEADS_PER_GROUP):
                hh = g * HEADS_PER_GROUP + hr
                seg = jnp.minimum(acum2[:, hh:hh + 1] - src_t[hh:hh + 1, :], ldt_t[hh:hh + 1, :])
                lmats.append((cbm * jnp.exp2(seg)).astype(BF16))
            x_stack = jnp.concatenate([xs_bf * hmask[hr] for hr in range(HEADS_PER_GROUP)], axis=0)
            y_g = xs * dexp_ref[:, ch] + jnp.dot(jnp.concatenate(lmats, axis=1), x_stack,
                                                 preferred_element_type=F32)
            s_old = st_ref[:, ch]
            y_g = y_g + jnp.dot(cg, s_old.astype(BF16), preferred_element_type=F32) * ea_x
            st_ref[:, ch] = (s_old * ea_x[SSM_CHUNK - 1:SSM_CHUNK, :]
                             + jnp.dot(bgt, (xs * w_x).astype(BF16), preferred_element_type=F32))
            yn_ref[rows, ch] = _gate_norm(y_g, z_ref[rows, ch], ng_ref[:, ch]).astype(BF16)

    out = jnp.dot(yn_ref[...], wout_ref[...], preferred_element_type=F32)
    xo_ref[0] = x_ref[0] + mod_ref[2, 0] * out

    @pl.when(j == pl.num_programs(1) - 1)
    def _():
        s_ref[0] = st_ref[...].T


def _ssd_sample_kernel(xbc_ref, dt_ref, z_ref, alog_ref, dexp_ref, lcum_ref, bd_ref, e_ref, ng_ref, s0_ref,
                       yn_ref, s_ref):
    r, tl, _ = xbc_ref.shape
    tm = r * tl
    xbc = xbc_ref[...].reshape(tm, CONV_DIM)
    xs = xbc[:, :D_INNER]
    dt = dt_ref[...].reshape(tm, SSM_HEADS)
    da = dt * (-jnp.exp(alog_ref[...]))
    lcum = lcum_ref[...]
    causal = lcum > 0.5
    acum = jnp.dot(lcum, da, precision=HIGHEST, preferred_element_type=F32)
    alast = jnp.dot(bd_ref[...], da, precision=HIGHEST, preferred_element_type=F32)
    acum_t = acum.T
    dec_end_t = jnp.exp(alast.T)
    dt_x, ea_x, w_x = _expand_heads([dt, jnp.exp(acum), dt * jnp.exp(alast - acum)], e_ref[...])
    xdt = (xs * dt_x).astype(BF16)
    xw = xs * w_x
    hmask = _head_masks()
    row = lax.broadcasted_iota(jnp.int32, (tm, 1), 0) // tl
    dec_cols = [jnp.broadcast_to(dec_end_t[:, b * tl:b * tl + 1], (SSM_HEADS, D_STATE)) for b in range(r)]
    for g in range(SSM_GROUPS):
        ch = slice(g * SSM_GROUP_W, (g + 1) * SSM_GROUP_W)
        bg = xbc[:, D_INNER + g * D_STATE:D_INNER + (g + 1) * D_STATE].astype(BF16)
        cg32 = xbc[:, D_INNER + (SSM_GROUPS + g) * D_STATE:D_INNER + (SSM_GROUPS + g + 1) * D_STATE]
        cbm = jnp.where(causal, lax.dot_general(cg32.astype(BF16), bg, NT_DIMS, preferred_element_type=F32), 0.0)
        y_g = xs[:, ch] * dexp_ref[:, ch] + _ssd_intra(g, cbm, acum, acum_t, xdt[:, ch], hmask)
        y_off = jnp.zeros((tm, SSM_GROUP_W), F32)
        xw_g = xw[:, ch]
        for b in range(r):
            s_old = s0_ref[b, ch, :]
            c_b = jnp.where(row == b, cg32, 0.0).astype(BF16)
            xw_b = jnp.where(row == b, xw_g, 0.0).astype(BF16)
            y_off = y_off + lax.dot_general(c_b, s_old.astype(BF16), NT_DIMS, preferred_element_type=F32)
            upd = lax.dot_general(xw_b, bg, TN_DIMS, preferred_element_type=F32)
            dec = jnp.concatenate(
                [jnp.broadcast_to(dec_cols[b][g * HEADS_PER_GROUP + hr:g * HEADS_PER_GROUP + hr + 1, :],
                                  (SSM_HEAD_DIM, D_STATE)) for hr in range(HEADS_PER_GROUP)], axis=0)
            s_ref[b, ch, :] = s_old * dec + upd
        z_g = z_ref[:, :, ch].reshape(tm, SSM_GROUP_W)
        yn_g = _gate_norm(y_g + y_off * ea_x[:, ch], z_g, ng_ref[:, ch])
        yn_ref[:, :, ch] = yn_g.reshape(r, tl, SSM_GROUP_W).astype(BF16)


def _ssd_consts(d_skip):
    e_bf = jnp.repeat(jnp.eye(SSM_HEADS, dtype=F32), SSM_HEAD_DIM, axis=1).astype(BF16)
    dexp = jnp.repeat(d_skip, SSM_HEAD_DIM).reshape(1, D_INNER)
    return e_bf, dexp


def _ssm_prompt(x, mod, layer, row0, tl, g, w_in, conv_w, conv_b, dt_bias, a_log, d_skip, norm_g, w_out):
    nb, l, d = x.shape
    assert l % tl == 0 and tl % SSM_CHUNK == 0 and tl % ROW_TILE == 0
    e_bf, dexp = _ssd_consts(d_skip)
    return pl.pallas_call(
        _ssm_prompt_kernel,
        grid=(nb, l // tl),
        in_specs=[pl.BlockSpec((1, tl, d), lambda i, j: (i, j, 0)),
                  _mod_spec(layer, 0, 1, row0),
                  _const_spec((1, d)),
                  _const_spec((d, D_ZX + SSM_HEADS)),
                  _const_spec((CONV_W, CONV_DIM)),
                  _const_spec((1, CONV_DIM)),
                  _const_spec((1, SSM_HEADS)),
                  _const_spec((1, SSM_HEADS)),
                  _const_spec((1, D_INNER)),
                  _const_spec((SSM_CHUNK, 3 * SSM_CHUNK)),
                  _const_spec((2 * SSM_HEADS, D_INNER)),
                  _const_spec((1, D_INNER)),
                  _const_spec((D_INNER, d))],
        out_specs=[pl.BlockSpec((1, tl, d), lambda i, j: (i, j, 0)),
                   pl.BlockSpec((1, CONV_W - 1, CONV_DIM), lambda i, j: (i, 0, 0)),
                   pl.BlockSpec((1, D_INNER, D_STATE), lambda i, j: (i, 0, 0))],
        out_shape=[jax.ShapeDtypeStruct((nb, l, d), F32),
                   jax.ShapeDtypeStruct((nb, CONV_W - 1, CONV_DIM), F32),
                   jax.ShapeDtypeStruct((nb, D_INNER, D_STATE), F32)],
        scratch_shapes=[pltpu.VMEM((CONV_PAD + tl, CONV_DIM), F32),
                        pltpu.VMEM((tl, D_INNER), F32),
                        pltpu.VMEM((tl, CONV_DIM), F32),
                        pltpu.VMEM((tl, D_INNER), BF16),
                        pltpu.VMEM((D_STATE, D_INNER), F32)],
        compiler_params=_params("parallel", "arbitrary"),
        name="ssm_prompt",
    )(x, mod, g.reshape(1, d), w_in.astype(BF16), conv_w, conv_b.reshape(1, CONV_DIM),
      dt_bias.reshape(1, SSM_HEADS), a_log.reshape(1, SSM_HEADS), dexp,
      jnp.tile(jnp.tril(jnp.ones((SSM_CHUNK, SSM_CHUNK), BF16)), (1, 3)), jnp.concatenate([e_bf, e_bf], axis=0),
      norm_g.reshape(1, D_INNER), w_out.astype(BF16))


def _ssd_sample(xbc, dt, z, a_log, d_skip, norm_g, state0, r):
    nb, l, _ = xbc.shape
    tm = r * l
    assert l <= SSM_CHUNK and nb % r == 0
    eye_r = jnp.eye(r, dtype=F32)
    lcum = jnp.kron(eye_r, jnp.tril(jnp.ones((l, l), F32)))
    bd = jnp.kron(eye_r, jnp.ones((l, l), F32))
    e_bf, dexp = _ssd_consts(d_skip)
    return pl.pallas_call(
        _ssd_sample_kernel,
        grid=(nb // r,),
        in_specs=[pl.BlockSpec((r, l, CONV_DIM), lambda i: (i, 0, 0)),
                  pl.BlockSpec((r, l, SSM_HEADS), lambda i: (i, 0, 0)),
                  pl.BlockSpec((r, l, D_INNER), lambda i: (i, 0, 0)),
                  _const_spec((1, SSM_HEADS)),
                  _const_spec((1, D_INNER)),
                  _const_spec((tm, tm)),
                  _const_spec((tm, tm)),
                  _const_spec((SSM_HEADS, D_INNER)),
                  _const_spec((1, D_INNER)),
                  pl.BlockSpec((r, D_INNER, D_STATE), lambda i: (i, 0, 0))],
        out_specs=[pl.BlockSpec((r, l, D_INNER), lambda i: (i, 0, 0)),
                   pl.BlockSpec((r, D_INNER, D_STATE), lambda i: (i, 0, 0))],
        out_shape=[jax.ShapeDtypeStruct((nb, l, D_INNER), BF16),
                   jax.ShapeDtypeStruct((nb, D_INNER, D_STATE), F32)],
        compiler_params=_params("parallel"),
        name="ssd_scan_sample",
    )(xbc, dt, z, a_log.reshape(1, SSM_HEADS), dexp, lcum, bd, e_bf, norm_g.reshape(1, D_INNER), state0)


def _ssm_out_kernel(x_ref, yn_ref, mod_ref, wout_ref, o_ref):
    r, tl, d = x_ref.shape
    if r == 1:
        yn = yn_ref[0]
    else:
        yn = yn_ref[...].astype(F32).reshape(r * tl, D_INNER).astype(BF16)
    out = jnp.dot(yn, wout_ref[...], preferred_element_type=F32)
    o_ref[...] = x_ref[...] + mod_ref[2] * out.reshape(r, tl, d)


def _ssm_out(x, yn, mod, layer, row0, r, tl, w_out):
    nb, l, d = x.shape
    return pl.pallas_call(
        _ssm_out_kernel,
        grid=(nb // r, l // tl),
        in_specs=[pl.BlockSpec((r, tl, d), lambda i, j: (i, j, 0)),
                  pl.BlockSpec((r, tl, D_INNER), lambda i, j: (i, j, 0)),
                  _mod_spec(layer, 0, r, row0),
                  _const_spec((D_INNER, d))],
        out_specs=pl.BlockSpec((r, tl, d), lambda i, j: (i, j, 0)),
        out_shape=jax.ShapeDtypeStruct((nb, l, d), F32),
        compiler_params=_params("parallel", "parallel"),
        name="ssm_out_proj",
    )(x, yn, mod, w_out.astype(BF16))


def _trunk(x, mod, row0, tiles, ssm_state, conv_state, p):
    nb, l, _ = x.shape
    new_v, new_ssm, new_conv = [], [], []
    for i in range(DEPTH):
        j = i // 2
        if i % 2 == 0:
            x, v = _gm_layer(x, mod, i, row0, *tiles["gm"], p["norm1_g"][i], p["gm_w_in"][j], p["gm_ln_g"][j],
                             p["gm_ln_b"][j], p["gm_w_s"][j], p["gm_b_s"][j], p["gm_w_out"][j])
            new_v.append(v)
        else:
            if ssm_state is None:
                x, cnew, s_new = _ssm_prompt(x, mod, i, row0, tiles["ssm"], p["norm1_g"][i], p["ssm_w_in"][j],
                                             p["ssm_conv_w"][j], p["ssm_conv_b"][j], p["ssm_dt_bias"][j],
                                             p["ssm_a_log"][j], p["ssm_d"][j], p["ssm_norm_g"][j],
                                             p["ssm_w_out"][j])
            else:
                z, xbc, dt, cnew = _ssm_in(x, mod, i, row0, *tiles["ssm_in"], p["norm1_g"][i], p["ssm_w_in"][j],
                                           p["ssm_conv_w"][j], p["ssm_conv_b"][j], p["ssm_dt_bias"][j],
                                           conv_state[j])
                s0 = ssm_state[j].reshape(nb, D_INNER, D_STATE)
                yn, s_new = _ssd_sample(xbc, dt, z, p["ssm_a_log"][j], p["ssm_d"][j], p["ssm_norm_g"][j], s0,
                                        tiles["ssd"])
                x = _ssm_out(x, yn, mod, i, row0, *tiles["ssm_out"], p["ssm_w_out"][j])
            new_conv.append(cnew)
            new_ssm.append(s_new.reshape(nb, SSM_HEADS, SSM_HEAD_DIM, D_STATE))
        x = _mlp_layer(x, mod, i, row0, *tiles["mlp"], p["norm2_g"][i], p["mlp_w1_bf"], p["mlp_w2_bf"],
                       p["final_g"], final=(i == DEPTH - 1))
    return x, jnp.stack(new_v), jnp.stack(new_ssm), jnp.stack(new_conv)


PROMPT_TILES = {"gm": (1, 512), "mlp": (1, 1024), "ssm": 512}
SAMPLE_TILES = {"gm": (64, 8), "mlp": (64, 8), "ssm_in": (32, 8), "ssd": 8, "ssm_out": (64, 8)}


def kernel(x_prompt, x_sample, c_prompt, c_sample, state_ssm, state_conv, ada_w, ada_b, norm1_g, norm2_g, gm_w_in, gm_ln_g, gm_ln_b, gm_w_s, gm_b_s, gm_w_out, ssm_w_in, ssm_conv_w, ssm_conv_b, ssm_dt_bias, ssm_a_log, ssm_d, ssm_norm_g, ssm_w_out, mlp_w1, mlp_w2, final_g):
    p = dict(norm1_g=norm1_g, norm2_g=norm2_g, gm_w_in=gm_w_in, gm_ln_g=gm_ln_g, gm_ln_b=gm_ln_b, gm_w_s=gm_w_s,
             gm_b_s=gm_b_s, gm_w_out=gm_w_out, ssm_w_in=ssm_w_in, ssm_conv_w=ssm_conv_w, ssm_conv_b=ssm_conv_b,
             ssm_dt_bias=ssm_dt_bias, ssm_a_log=ssm_a_log, ssm_d=ssm_d, ssm_norm_g=ssm_norm_g, ssm_w_out=ssm_w_out,
             mlp_w1_bf=mlp_w1.astype(BF16), mlp_w2_bf=mlp_w2.astype(BF16), final_g=final_g)
    n_sample = x_sample.shape[0]
    mod = _ada(jnp.concatenate([c_sample, c_prompt], axis=0), ada_w, ada_b)
    y_p, v_p, ssm_p, conv_p = _trunk(x_prompt, mod, n_sample, PROMPT_TILES, None, None, p)
    y_s, v_s, ssm_s, conv_s = _trunk(x_sample, mod, 0, SAMPLE_TILES, state_ssm, state_conv, p)
    return (y_p, y_s, v_p, v_s, ssm_p, conv_p, ssm_s, conv_s)
```

```python
import functools

import jax
import jax.numpy as jnp
from jax import lax
from jax.experimental import pallas as pl
from jax.experimental.pallas import tpu as pltpu

F32 = jnp.float32
BF16 = jnp.bfloat16

D_MODEL = 1024
DEPTH = 2
N_MOD = 6
EPS = 1e-6
LOG2E = 1.4426950408889634
GM_CHUNK = 128
D_GM = 2 * D_MODEL
GM_GROUPS = 8
GM_GROUP_W = D_GM // GM_GROUPS
GM_COL_BLOCK = 1024
D_INNER = 2 * D_MODEL
SSM_HEAD_DIM = 64
SSM_HEADS = D_INNER // SSM_HEAD_DIM
SSM_GROUPS = 8
HEADS_PER_GROUP = SSM_HEADS // SSM_GROUPS
SSM_GROUP_W = D_INNER // SSM_GROUPS
D_STATE = 128
CONV_W = 4
CONV_DIM = D_INNER + 2 * SSM_GROUPS * D_STATE
D_ZX = D_INNER + CONV_DIM
SSM_CHUNK = 128
D_FF = 4 * D_MODEL
FF_CHUNK = 1024
SUBLANES = 8
CONV_PAD = SUBLANES
ROW_TILE = 256
LANE_TILE = 512
VMEM_LIMIT = 56 * 1024 * 1024


def _silu(x):
    return x * (1.0 / (1.0 + jnp.exp2(x * (-LOG2E))))


def _softplus(x):
    return jnp.maximum(x, 0.0) + jnp.log1p(jnp.exp(-jnp.abs(x)))


def _gelu_tanh(x):
    return x * (0.5 * (1.0 + jnp.tanh(0.7978845608028654 * (x + 0.044715 * (x * x * x)))))


def _rms(x, g):
    return x * lax.rsqrt(jnp.mean(x * x, axis=-1, keepdims=True) + EPS) * g


def _norm_modulate(x3, g, shift, scale):
    r, tl, d = x3.shape
    hn = _rms(x3.reshape(r * tl, d), g)
    h3 = hn.reshape(r, tl, d) * (1.0 + scale) + shift
    return h3.reshape(r * tl, d).astype(BF16)


def _const_spec(shape):
    return pl.BlockSpec(shape, lambda *_: (0,) * len(shape), pipeline_mode=pl.Buffered(1))


def _params(*sem):
    return pltpu.CompilerParams(dimension_semantics=sem, vmem_limit_bytes=VMEM_LIMIT)


def _ada_kernel(c_ref, w_ref, b_ref, o_ref):
    sc = _silu(c_ref[...]).astype(BF16)
    res = jnp.dot(sc, w_ref[...].astype(BF16), preferred_element_type=F32) + b_ref[...]
    o_ref[...] = res.reshape(o_ref.shape)


def _ada(c_all, ada_w, ada_b):
    nb = c_all.shape[0]
    return pl.pallas_call(
        _ada_kernel,
        grid=(DEPTH, N_MOD),
        in_specs=[pl.BlockSpec((nb, D_MODEL), lambda i, k: (0, 0)),
                  pl.BlockSpec((None, D_MODEL, D_MODEL), lambda i, k: (i, 0, k)),
                  pl.BlockSpec((None, None, 1, D_MODEL), lambda i, k: (i, k, 0, 0))],
        out_specs=pl.BlockSpec((None, None, nb, 1, D_MODEL), lambda i, k: (i, k, 0, 0, 0)),
        out_shape=jax.ShapeDtypeStruct((DEPTH, N_MOD, nb, 1, D_MODEL), F32),
        compiler_params=_params("parallel", "parallel"),
        name="ada_mod",
    )(c_all, ada_w, ada_b.reshape(DEPTH, N_MOD, 1, D_MODEL))


def _mod_spec(layer, half, r, row0):
    return pl.BlockSpec((None, 3, r, 1, D_MODEL), lambda i, j: (layer, half, row0 // r + i, 0, 0))


def _gm_kernel(x_ref, mod_ref, g_ref, win_ref, lng_ref, lnb_ref, wmix_ref, bexp_ref, wout_ref,
               xo_ref, v_ref, z_ref, *, v_tail):
    r, tl, d = x_ref.shape
    tm = r * tl
    x3 = x_ref[...]
    h = _norm_modulate(x3, g_ref[...], mod_ref[0], mod_ref[1])
    for k in range(2 * D_GM // GM_COL_BLOCK):
        cols = slice(k * GM_COL_BLOCK, (k + 1) * GM_COL_BLOCK)
        z_ref[:, cols] = _gelu_tanh(jnp.dot(h, win_ref[:, cols], preferred_element_type=F32))
    u = z_ref[:, :D_GM]
    vr = z_ref[:, D_GM:]
    xc = vr - jnp.mean(vr, axis=-1, keepdims=True)
    v = xc * lax.rsqrt(jnp.mean(xc * xc, axis=-1, keepdims=True) + EPS) * lng_ref[...] + lnb_ref[...]
    vb = v.astype(BF16)
    chunks = []
    for c in range(tm // GM_CHUNK):
        vc = vb[c * GM_CHUNK:(c + 1) * GM_CHUNK]
        parts = [jnp.dot(wmix_ref[g], vc[:, g * GM_GROUP_W:(g + 1) * GM_GROUP_W], preferred_element_type=F32)
                 for g in range(GM_GROUPS)]
        chunks.append(jnp.concatenate(parts, axis=1) + bexp_ref[...])
    s = jnp.concatenate(chunks, axis=0) if len(chunks) > 1 else chunks[0]
    y = jnp.dot((u * s).astype(BF16), wout_ref[...], preferred_element_type=F32)
    xo_ref[...] = x3 + mod_ref[2] * y.reshape(r, tl, d)

    @pl.when(pl.program_id(1) == pl.num_programs(1) - 1)
    def _():
        v_ref[...] = v.reshape(r, tl, D_GM)[:, tl - v_tail:, :]


def _gm_layer(x, mod, layer, row0, r, tl, g, w_in, ln_g, ln_b, w_s, b_s, w_out):
    nb, l, d = x.shape
    q = min(l, GM_CHUNK)
    assert l % q == 0 and GM_CHUNK % q == 0 and (r * tl) % GM_CHUNK == 0 and tl % q == 0
    rep = GM_CHUNK // q
    tri = jnp.tril(jnp.ones((q, q), F32))
    wq = w_s[:, :q, :q] * tri
    wmix = jnp.einsum("ab,gts->gatbs", jnp.eye(rep, dtype=F32), wq).reshape(GM_GROUPS, GM_CHUNK, GM_CHUNK)
    bq = jnp.tile(b_s[:, :q].T, (rep, 1))
    bexp = jnp.repeat(bq, GM_GROUP_W, axis=1)
    v_tail = l - ((l - 1) // GM_CHUNK) * GM_CHUNK
    assert v_tail <= tl
    kern = functools.partial(_gm_kernel, v_tail=v_tail)
    xo, v = pl.pallas_call(
        kern,
        grid=(nb // r, l // tl),
        in_specs=[pl.BlockSpec((r, tl, d), lambda i, j: (i, j, 0)),
                  _mod_spec(layer, 0, r, row0),
                  _const_spec((1, d)),
                  _const_spec((d, 2 * D_GM)),
                  _const_spec((1, D_GM)),
                  _const_spec((1, D_GM)),
                  _const_spec((GM_GROUPS, GM_CHUNK, GM_CHUNK)),
                  _const_spec((GM_CHUNK, D_GM)),
                  _const_spec((D_GM, d))],
        out_specs=[pl.BlockSpec((r, tl, d), lambda i, j: (i, j, 0)),
                   pl.BlockSpec((r, v_tail, D_GM), lambda i, j: (i, 0, 0))],
        out_shape=[jax.ShapeDtypeStruct((nb, l, d), F32),
                   jax.ShapeDtypeStruct((nb, v_tail, D_GM), F32)],
        scratch_shapes=[pltpu.VMEM((r * tl, 2 * D_GM), F32)],
        compiler_params=_params("parallel", "arbitrary"),
        name="gmlp_mixer",
    )(x, mod, g.reshape(1, d), w_in.astype(BF16), ln_g.reshape(1, D_GM), ln_b.reshape(1, D_GM),
      wmix.astype(BF16), bexp, w_out.astype(BF16))
    return xo, v


def _mlp_kernel(x_ref, mod_ref, g_ref, w1_ref, w2_ref, gf_ref, o_ref, *, final):
    r, tl, d = x_ref.shape
    x3 = x_ref[...]
    h = _norm_modulate(x3, g_ref[...], mod_ref[0], mod_ref[1])
    acc = jnp.zeros((r * tl, d), F32)
    for k in range(D_FF // FF_CHUNK):
        a = jnp.dot(h, w1_ref[:, k * FF_CHUNK:(k + 1) * FF_CHUNK], preferred_element_type=F32)
        a = jnp.square(jnp.maximum(a, 0.0)).astype(BF16)
        acc = acc + jnp.dot(a, w2_ref[k * FF_CHUNK:(k + 1) * FF_CHUNK, :], preferred_element_type=F32)
    xo = x3 + mod_ref[2] * acc.reshape(r, tl, d)
    if final:
        xo = _rms(xo.reshape(r * tl, d), gf_ref[...]).reshape(r, tl, d)
    o_ref[...] = xo


def _layer_spec(layer, shape):
    return pl.BlockSpec((None,) + shape, lambda *_: (layer,) + (0,) * len(shape), pipeline_mode=pl.Buffered(1))


def _mlp_layer(x, mod, layer, row0, r, tl, g, w1_all, w2_all, final_g, final):
    nb, l, d = x.shape
    kern = functools.partial(_mlp_kernel, final=final)
    return pl.pallas_call(
        kern,
        grid=(nb // r, l // tl),
        in_specs=[pl.BlockSpec((r, tl, d), lambda i, j: (i, j, 0)),
                  _mod_spec(layer, 1, r, row0),
                  _const_spec((1, d)),
                  _layer_spec(layer, (d, D_FF)),
                  _layer_spec(layer, (D_FF, d)),
                  _const_spec((1, d))],
        out_specs=pl.BlockSpec((r, tl, d), lambda i, j: (i, j, 0)),
        out_shape=jax.ShapeDtypeStruct((nb, l, d), F32),
        compiler_params=_params("parallel", "parallel"),
        name="relu2_mlp",
    )(x, mod, g.reshape(1, d), w1_all, w2_all, final_g.reshape(1, d))


def _ssm_in_kernel(x_ref, mod_ref, g_ref, w_ref, cw_ref, cb_ref, dtb_ref, cs_ref,
                   z_ref, xbc_ref, dt_ref, cnew_ref, xp_ref):
    r, tl, d = x_ref.shape

    @pl.when(pl.program_id(1) == 0)
    def _():
        xp_ref[:, 0:CONV_PAD, :] = jnp.zeros((r, CONV_PAD, CONV_DIM), F32)
        xp_ref[:, CONV_PAD - (CONV_W - 1):CONV_PAD, :] = cs_ref[...]

    h = _norm_modulate(x_ref[...], g_ref[...], mod_ref[0], mod_ref[1])
    z_ref[...] = jnp.dot(h, w_ref[:, :D_INNER], preferred_element_type=F32).reshape(r, tl, D_INNER)
    xbc = jnp.dot(h, w_ref[:, D_INNER:D_ZX], preferred_element_type=F32)
    dt_raw = jnp.dot(h, w_ref[:, D_ZX:], preferred_element_type=F32)
    dt_ref[...] = _softplus(dt_raw + dtb_ref[...]).reshape(r, tl, SSM_HEADS)
    xp_ref[:, CONV_PAD:CONV_PAD + tl, :] = xbc.reshape(r, tl, CONV_DIM)
    regs = r * tl // SUBLANES
    cur = xp_ref[:, CONV_PAD:CONV_PAD + tl, :].reshape(regs, SUBLANES, CONV_DIM)
    prv = xp_ref[:, 0:tl, :].reshape(regs, SUBLANES, CONV_DIM)
    sub = lax.broadcasted_iota(jnp.int32, (1, SUBLANES, 1), 1)
    conv = cb_ref[...].reshape(1, 1, CONV_DIM) + cur * cw_ref[CONV_W - 1:CONV_W, :].reshape(1, 1, CONV_DIM)
    for s in range(1, CONV_W):
        shifted = pltpu.roll(jnp.where(sub < SUBLANES - s, cur, prv), s, axis=1)
        conv = conv + shifted * cw_ref[CONV_W - 1 - s:CONV_W - s, :].reshape(1, 1, CONV_DIM)
    xbc_ref[...] = _silu(conv).reshape(r, tl, CONV_DIM)
    cnew_ref[...] = xp_ref[:, tl + CONV_PAD - (CONV_W - 1):tl + CONV_PAD, :]
    xp_ref[:, 0:CONV_PAD, :] = xp_ref[:, tl:tl + CONV_PAD, :]


def _ssm_in(x, mod, layer, row0, r, tl, g, w_in, conv_w, conv_b, dt_bias, conv_state):
    nb, l, d = x.shape
    return pl.pallas_call(
        _ssm_in_kernel,
        grid=(nb // r, l // tl),
        in_specs=[pl.BlockSpec((r, tl, d), lambda i, j: (i, j, 0)),
                  _mod_spec(layer, 0, r, row0),
                  _const_spec((1, d)),
                  _const_spec((d, D_ZX + SSM_HEADS)),
                  _const_spec((CONV_W, CONV_DIM)),
                  _const_spec((1, CONV_DIM)),
                  _const_spec((1, SSM_HEADS)),
                  pl.BlockSpec((r, CONV_W - 1, CONV_DIM), lambda i, j: (i, 0, 0))],
        out_specs=[pl.BlockSpec((r, tl, D_INNER), lambda i, j: (i, j, 0)),
                   pl.BlockSpec((r, tl, CONV_DIM), lambda i, j: (i, j, 0)),
                   pl.BlockSpec((r, tl, SSM_HEADS), lambda i, j: (i, j, 0)),
                   pl.BlockSpec((r, CONV_W - 1, CONV_DIM), lambda i, j: (i, 0, 0))],
        out_shape=[jax.ShapeDtypeStruct((nb, l, D_INNER), F32),
                   jax.ShapeDtypeStruct((nb, l, CONV_DIM), F32),
                   jax.ShapeDtypeStruct((nb, l, SSM_HEADS), F32),
                   jax.ShapeDtypeStruct((nb, CONV_W - 1, CONV_DIM), F32)],
        scratch_shapes=[pltpu.VMEM((r, CONV_PAD + tl, CONV_DIM), F32)],
        compiler_params=_params("parallel", "arbitrary"),
        name="ssm_in_conv",
    )(x, mod, g.reshape(1, d), w_in.astype(BF16), conv_w, conv_b.reshape(1, CONV_DIM),
      dt_bias.reshape(1, SSM_HEADS), conv_state)


NT_DIMS = (((1,), (1,)), ((), ()))
TN_DIMS = (((0,), (0,)), ((), ()))


def _expand_heads(vals, e_bf):
    tm = vals[0].shape[0]
    v = jnp.concatenate(vals, axis=0)
    hi = v.astype(BF16)
    lo = (v - hi.astype(F32)).astype(BF16)
    x = jnp.dot(hi, e_bf, preferred_element_type=F32) + jnp.dot(lo, e_bf, preferred_element_type=F32)
    return [x[k * tm:(k + 1) * tm] for k in range(len(vals))]


def _head_masks():
    lane = lax.broadcasted_iota(jnp.int32, (1, SSM_GROUP_W), 1) // SSM_HEAD_DIM
    return [jnp.where(lane == hr, 1.0, 0.0).astype(BF16) for hr in range(HEADS_PER_GROUP)]


def _ssd_intra(g, cbm, acum, acum_t, xdt_g, hmask):
    acc = None
    for hr in range(HEADS_PER_GROUP):
        hh = g * HEADS_PER_GROUP + hr
        seg = jnp.minimum(acum[:, hh:hh + 1] - acum_t[hh:hh + 1, :], 0.0)
        lmat = (cbm * jnp.exp(seg)).astype(BF16)
        part = jnp.dot(lmat, xdt_g * hmask[hr], preferred_element_type=F32)
        acc = part if acc is None else acc + part
    return acc


def _cumsum_rows(lcum3, da):
    hi = da.astype(BF16)
    rest = da - hi.astype(F32)
    mid = rest.astype(BF16)
    lo = (rest - mid.astype(F32)).astype(BF16)
    return jnp.dot(lcum3, jnp.concatenate([hi, mid, lo], axis=0), preferred_element_type=F32)


def _gate_norm(y_g, z_g, ng_g):
    return _rms(y_g * _silu(z_g), ng_g)


def _ssm_prompt_kernel(x_ref, mod_ref, g_ref, w_ref, cw_ref, cb_ref, dtb_ref, alog_ref, dexp_ref, lcum_ref, e_ref,
                       ng_ref, wout_ref, xo_ref, cnew_ref, s_ref, xp_ref, z_ref, xc_ref, yn_ref, st_ref):
    _, tl, _ = x_ref.shape
    j = pl.program_id(1)

    @pl.when(j == 0)
    def _():
        xp_ref[0:CONV_PAD, :] = jnp.zeros((CONV_PAD, CONV_DIM), F32)
        st_ref[...] = jnp.zeros_like(st_ref)

    h = _norm_modulate(x_ref[...], g_ref[...], mod_ref[0], mod_ref[1])
    z_ref[...] = jnp.dot(h, w_ref[:, :D_INNER], preferred_element_type=F32)
    xp_ref[CONV_PAD:CONV_PAD + tl, :] = jnp.dot(h, w_ref[:, D_INNER:D_ZX], preferred_element_type=F32)
    dt_all = _softplus(jnp.dot(h, w_ref[:, D_ZX:], preferred_element_type=F32) + dtb_ref[...])

    sub = lax.broadcasted_iota(jnp.int32, (1, SUBLANES, 1), 1)
    for lb in range(CONV_DIM // LANE_TILE):
        lns = slice(lb * LANE_TILE, (lb + 1) * LANE_TILE)
        taps = [cw_ref[k:k + 1, lns].reshape(1, 1, LANE_TILE) for k in range(CONV_W)]
        bias = cb_ref[:, lns].reshape(1, 1, LANE_TILE)
        for rb in range(tl // ROW_TILE):
            r0 = rb * ROW_TILE
            cur = xp_ref[CONV_PAD + r0:CONV_PAD + r0 + ROW_TILE, lns]
            prv = xp_ref[r0:r0 + ROW_TILE, lns]
            cur = cur.reshape(ROW_TILE // SUBLANES, SUBLANES, LANE_TILE)
            prv = prv.reshape(ROW_TILE // SUBLANES, SUBLANES, LANE_TILE)
            conv = bias + cur * taps[CONV_W - 1]
            for s in range(1, CONV_W):
                conv = conv + pltpu.roll(jnp.where(sub < SUBLANES - s, cur, prv), s, axis=1) * taps[CONV_W - 1 - s]
            xc_ref[r0:r0 + ROW_TILE, lns] = _silu(conv).reshape(ROW_TILE, LANE_TILE)
    cnew_ref[0] = xp_ref[tl + CONV_PAD - (CONV_W - 1):tl + CONV_PAD, :]
    xp_ref[0:CONV_PAD, :] = xp_ref[tl:tl + CONV_PAD, :]

    a_neg = -jnp.exp(alog_ref[...])
    lcum3 = lcum_ref[...]
    causal = lcum3[:, :SSM_CHUNK].astype(F32) > 0.5
    hmask = _head_masks()
    for c in range(tl // SSM_CHUNK):
        rows = slice(c * SSM_CHUNK, (c + 1) * SSM_CHUNK)
        dt = dt_all[rows]
        acum = _cumsum_rows(lcum3, dt * a_neg)
        alast = acum[SSM_CHUNK - 1:SSM_CHUNK, :]
        acum2 = acum * LOG2E
        ldt_t = jnp.log2(dt).T
        src_t = acum2.T - ldt_t
        heads = jnp.concatenate([jnp.exp(acum), dt * jnp.exp(alast - acum)], axis=0)
        heads_hi = heads.astype(BF16)
        heads_lo = (heads - heads_hi.astype(F32)).astype(BF16)
        ex = jnp.dot(jnp.concatenate([heads_hi, heads_lo], axis=1), e_ref[...], preferred_element_type=F32)
        for g in range(SSM_GROUPS):
            ch = slice(g * SSM_GROUP_W, (g + 1) * SSM_GROUP_W)
            ea_x = ex[:SSM_CHUNK, ch]
            w_x = ex[SSM_CHUNK:, ch]
            xs = xc_ref[rows, ch]
            xs_bf = xs.astype(BF16)
            if g % 2 == 0:
                b_lo = D_INNER + g * D_STATE
                c_lo = D_INNER + (SSM_GROUPS + g) * D_STATE
                bgt_pair = [xc_ref[rows, b_lo + k * D_STATE:b_lo + (k + 1) * D_STATE].T.astype(BF16) for k in (0, 1)]
                cg_pair = [xc_ref[rows, c_lo + k * D_STATE:c_lo + (k + 1) * D_STATE].astype(BF16) for k in (0, 1)]
                zero = jnp.zeros((D_STATE, SSM_CHUNK), BF16)
                bdiag = jnp.concatenate([jnp.concatenate([bgt_pair[0], zero], axis=1),
                                         jnp.concatenate([zero, bgt_pair[1]], axis=1)], axis=0)
                cb_pair = jnp.dot(jnp.concatenate(cg_pair, axis=1), bdiag, preferred_element_type=F32)
            bgt = bgt_pair[g % 2]
            cg = cg_pair[g % 2]
            cbm = jnp.where(causal, cb_pair[:, (g % 2) * SSM_CHUNK:(g % 2 + 1) * SSM_CHUNK], 0.0)
            lmats = []
            for hr in range(HEADS_PER_GROUP):
                hh = g * HEADS_PER_GROUP + hr
                seg = jnp.minimum(acum2[:, hh:hh + 1] - src_t[hh:hh + 1, :], ldt_t[hh:hh + 1, :])
                lmats.append((cbm * jnp.exp2(seg)).astype(BF16))
            x_stack = jnp.concatenate([xs_bf * hmask[hr] for hr in range(HEADS_PER_GROUP)], axis=0)
            y_g = xs * dexp_ref[:, ch] + jnp.dot(jnp.concatenate(lmats, axis=1), x_stack,
                                                 preferred_element_type=F32)
            s_old = st_ref[:, ch]
            y_g = y_g + jnp.dot(cg, s_old.astype(BF16), preferred_element_type=F32) * ea_x
            st_ref[:, ch] = (s_old * ea_x[SSM_CHUNK - 1:SSM_CHUNK, :]
                             + jnp.dot(bgt, (xs * w_x).astype(BF16), preferred_element_type=F32))
            yn_ref[rows, ch] = _gate_norm(y_g, z_ref[rows, ch], ng_ref[:, ch]).astype(BF16)

    out = jnp.dot(yn_ref[...], wout_ref[...], preferred_element_type=F32)
    xo_ref[0] = x_ref[0] + mod_ref[2, 0] * out

    @pl.when(j == pl.num_programs(1) - 1)
    def _():
        s_ref[0] = st_ref[...].T


def _ssd_sample_kernel(xbc_ref, dt_ref, z_ref, alog_ref, dexp_ref, lcum_ref, bd_ref, e_ref, ng_ref, s0_ref,
                       yn_ref, s_ref):
    r, tl, _ = xbc_ref.shape
    tm = r * tl
    xbc = xbc_ref[...].reshape(tm, CONV_DIM)
    xs = xbc[:, :D_INNER]
    dt = dt_ref[...].reshape(tm, SSM_HEADS)
    da = dt * (-jnp.exp(alog_ref[...]))
    lcum3 = lcum_ref[...]
    causal = lcum3[:, :tm].astype(F32) > 0.5
    acum = _cumsum_rows(lcum3, da)
    alast = _cumsum_rows(bd_ref[...], da)
    acum_t = acum.T
    dec_end_t = jnp.exp(alast.T)
    dt_x, ea_x, w_x = _expand_heads([dt, jnp.exp(acum), dt * jnp.exp(alast - acum)], e_ref[...])
    xdt = (xs * dt_x).astype(BF16)
    xw = xs * w_x
    hmask = _head_masks()
    row = lax.broadcasted_iota(jnp.int32, (tm, 1), 0) // tl
    dec_cols = [jnp.broadcast_to(dec_end_t[:, b * tl:b * tl + 1], (SSM_HEADS, D_STATE)) for b in range(r)]
    for g in range(SSM_GROUPS):
        ch = slice(g * SSM_GROUP_W, (g + 1) * SSM_GROUP_W)
        bg = xbc[:, D_INNER + g * D_STATE:D_INNER + (g + 1) * D_STATE].astype(BF16)
        cg32 = xbc[:, D_INNER + (SSM_GROUPS + g) * D_STATE:D_INNER + (SSM_GROUPS + g + 1) * D_STATE]
        cbm = jnp.where(causal, lax.dot_general(cg32.astype(BF16), bg, NT_DIMS, preferred_element_type=F32), 0.0)
        y_g = xs[:, ch] * dexp_ref[:, ch] + _ssd_intra(g, cbm, acum, acum_t, xdt[:, ch], hmask)
        y_off = jnp.zeros((tm, SSM_GROUP_W), F32)
        xw_g = xw[:, ch]
        for b in range(r):
            s_old = s0_ref[b, ch, :]
            c_b = jnp.where(row == b, cg32, 0.0).astype(BF16)
            xw_b = jnp.where(row == b, xw_g, 0.0).astype(BF16)
            y_off = y_off + lax.dot_general(c_b, s_old.astype(BF16), NT_DIMS, preferred_element_type=F32)
            upd = lax.dot_general(xw_b, bg, TN_DIMS, preferred_element_type=F32)
            dec = jnp.concatenate(
                [jnp.broadcast_to(dec_cols[b][g * HEADS_PER_GROUP + hr:g * HEADS_PER_GROUP + hr + 1, :],
                                  (SSM_HEAD_DIM, D_STATE)) for hr in range(HEADS_PER_GROUP)], axis=0)
            s_ref[b, ch, :] = s_old * dec + upd
        z_g = z_ref[:, :, ch].reshape(tm, SSM_GROUP_W)
        yn_g = _gate_norm(y_g + y_off * ea_x[:, ch], z_g, ng_ref[:, ch])
        yn_ref[:, :, ch] = yn_g.reshape(r, tl, SSM_GROUP_W).astype(BF16)


def _ssd_consts(d_skip):
    e_bf = jnp.repeat(jnp.eye(SSM_HEADS, dtype=F32), SSM_HEAD_DIM, axis=1).astype(BF16)
    dexp = jnp.repeat(d_skip, SSM_HEAD_DIM).reshape(1, D_INNER)
    return e_bf, dexp


def _ssm_prompt(x, mod, layer, row0, tl, g, w_in, conv_w, conv_b, dt_bias, a_log, d_skip, norm_g, w_out):
    nb, l, d = x.shape
    assert l % tl == 0 and tl % SSM_CHUNK == 0 and tl % ROW_TILE == 0
    e_bf, dexp = _ssd_consts(d_skip)
    return pl.pallas_call(
        _ssm_prompt_kernel,
        grid=(nb, l // tl),
        in_specs=[pl.BlockSpec((1, tl, d), lambda i, j: (i, j, 0)),
                  _mod_spec(layer, 0, 1, row0),
                  _const_spec((1, d)),
                  _const_spec((d, D_ZX + SSM_HEADS)),
                  _const_spec((CONV_W, CONV_DIM)),
                  _const_spec((1, CONV_DIM)),
                  _const_spec((1, SSM_HEADS)),
                  _const_spec((1, SSM_HEADS)),
                  _const_spec((1, D_INNER)),
                  _const_spec((SSM_CHUNK, 3 * SSM_CHUNK)),
                  _const_spec((2 * SSM_HEADS, D_INNER)),
                  _const_spec((1, D_INNER)),
                  _const_spec((D_INNER, d))],
        out_specs=[pl.BlockSpec((1, tl, d), lambda i, j: (i, j, 0)),
                   pl.BlockSpec((1, CONV_W - 1, CONV_DIM), lambda i, j: (i, 0, 0)),
                   pl.BlockSpec((1, D_INNER, D_STATE), lambda i, j: (i, 0, 0))],
        out_shape=[jax.ShapeDtypeStruct((nb, l, d), F32),
                   jax.ShapeDtypeStruct((nb, CONV_W - 1, CONV_DIM), F32),
                   jax.ShapeDtypeStruct((nb, D_INNER, D_STATE), F32)],
        scratch_shapes=[pltpu.VMEM((CONV_PAD + tl, CONV_DIM), F32),
                        pltpu.VMEM((tl, D_INNER), F32),
                        pltpu.VMEM((tl, CONV_DIM), F32),
                        pltpu.VMEM((tl, D_INNER), BF16),
                        pltpu.VMEM((D_STATE, D_INNER), F32)],
        compiler_params=_params("parallel", "arbitrary"),
        name="ssm_prompt",
    )(x, mod, g.reshape(1, d), w_in.astype(BF16), conv_w, conv_b.reshape(1, CONV_DIM),
      dt_bias.reshape(1, SSM_HEADS), a_log.reshape(1, SSM_HEADS), dexp,
      jnp.tile(jnp.tril(jnp.ones((SSM_CHUNK, SSM_CHUNK), BF16)), (1, 3)), jnp.concatenate([e_bf, e_bf], axis=0),
      norm_g.reshape(1, D_INNER), w_out.astype(BF16))


def _ssd_sample(xbc, dt, z, a_log, d_skip, norm_g, state0, r):
    nb, l, _ = xbc.shape
    tm = r * l
    assert l <= SSM_CHUNK and nb % r == 0
    eye_r = jnp.eye(r, dtype=F32)
    lcum = jnp.tile(jnp.kron(eye_r, jnp.tril(jnp.ones((l, l), F32))).astype(BF16), (1, 3))
    bd = jnp.tile(jnp.kron(eye_r, jnp.ones((l, l), F32)).astype(BF16), (1, 3))
    e_bf, dexp = _ssd_consts(d_skip)
    return pl.pallas_call(
        _ssd_sample_kernel,
        grid=(nb // r,),
        in_specs=[pl.BlockSpec((r, l, CONV_DIM), lambda i: (i, 0, 0)),
                  pl.BlockSpec((r, l, SSM_HEADS), lambda i: (i, 0, 0)),
                  pl.BlockSpec((r, l, D_INNER), lambda i: (i, 0, 0)),
                  _const_spec((1, SSM_HEADS)),
                  _const_spec((1, D_INNER)),
                  _const_spec((tm, 3 * tm)),
                  _const_spec((tm, 3 * tm)),
                  _const_spec((SSM_HEADS, D_INNER)),
                  _const_spec((1, D_INNER)),
                  pl.BlockSpec((r, D_INNER, D_STATE), lambda i: (i, 0, 0))],
        out_specs=[pl.BlockSpec((r, l, D_INNER), lambda i: (i, 0, 0)),
                   pl.BlockSpec((r, D_INNER, D_STATE), lambda i: (i, 0, 0))],
        out_shape=[jax.ShapeDtypeStruct((nb, l, D_INNER), BF16),
                   jax.ShapeDtypeStruct((nb, D_INNER, D_STATE), F32)],
        compiler_params=_params("parallel"),
        name="ssd_scan_sample",
    )(xbc, dt, z, a_log.reshape(1, SSM_HEADS), dexp, lcum, bd, e_bf, norm_g.reshape(1, D_INNER), state0)


def _ssm_out_kernel(x_ref, yn_ref, mod_ref, wout_ref, o_ref):
    r, tl, d = x_ref.shape
    if r == 1:
        yn = yn_ref[0]
    else:
        yn = yn_ref[...].astype(F32).reshape(r * tl, D_INNER).astype(BF16)
    out = jnp.dot(yn, wout_ref[...], preferred_element_type=F32)
    o_ref[...] = x_ref[...] + mod_ref[2] * out.reshape(r, tl, d)


def _ssm_out(x, yn, mod, layer, row0, r, tl, w_out):
    nb, l, d = x.shape
    return pl.pallas_call(
        _ssm_out_kernel,
        grid=(nb // r, l // tl),
        in_specs=[pl.BlockSpec((r, tl, d), lambda i, j: (i, j, 0)),
                  pl.BlockSpec((r, tl, D_INNER), lambda i, j: (i, j, 0)),
                  _mod_spec(layer, 0, r, row0),
                  _const_spec((D_INNER, d))],
        out_specs=pl.BlockSpec((r, tl, d), lambda i, j: (i, j, 0)),
        out_shape=jax.ShapeDtypeStruct((nb, l, d), F32),
        compiler_params=_params("parallel", "parallel"),
        name="ssm_out_proj",
    )(x, yn, mod, w_out.astype(BF16))


def _trunk(x, mod, row0, tiles, ssm_state, conv_state, p):
    nb, l, _ = x.shape
    new_v, new_ssm, new_conv = [], [], []
    for i in range(DEPTH):
        j = i // 2
        if i % 2 == 0:
            x, v = _gm_layer(x, mod, i, row0, *tiles["gm"], p["norm1_g"][i], p["gm_w_in"][j], p["gm_ln_g"][j],
                             p["gm_ln_b"][j], p["gm_w_s"][j], p["gm_b_s"][j], p["gm_w_out"][j])
            new_v.append(v)
        else:
            if ssm_state is None:
                x, cnew, s_new = _ssm_prompt(x, mod, i, row0, tiles["ssm"], p["norm1_g"][i], p["ssm_w_in"][j],
                                             p["ssm_conv_w"][j], p["ssm_conv_b"][j], p["ssm_dt_bias"][j],
                                             p["ssm_a_log"][j], p["ssm_d"][j], p["ssm_norm_g"][j],
                                             p["ssm_w_out"][j])
            else:
                z, xbc, dt, cnew = _ssm_in(x, mod, i, row0, *tiles["ssm_in"], p["norm1_g"][i], p["ssm_w_in"][j],
                                           p["ssm_conv_w"][j], p["ssm_conv_b"][j], p["ssm_dt_bias"][j],
                                           conv_state[j])
                s0 = ssm_state[j].reshape(nb, D_INNER, D_STATE)
                yn, s_new = _ssd_sample(xbc, dt, z, p["ssm_a_log"][j], p["ssm_d"][j], p["ssm_norm_g"][j], s0,
                                        tiles["ssd"])
                x = _ssm_out(x, yn, mod, i, row0, *tiles["ssm_out"], p["ssm_w_out"][j])
            new_conv.append(cnew)
            new_ssm.append(s_new.reshape(nb, SSM_HEADS, SSM_HEAD_DIM, D_STATE))
        x = _mlp_layer(x, mod, i, row0, *tiles["mlp"], p["norm2_g"][i], p["mlp_w1_bf"], p["mlp_w2_bf"],
                       p["final_g"], final=(i == DEPTH - 1))
    return x, jnp.stack(new_v), jnp.stack(new_ssm), jnp.stack(new_conv)


PROMPT_TILES = {"gm": (1, 512), "mlp": (1, 1024), "ssm": 512}
SAMPLE_TILES = {"gm": (64, 8), "mlp": (64, 8), "ssm_in": (32, 8), "ssd": 8, "ssm_out": (64, 8)}


def kernel(x_prompt, x_sample, c_prompt, c_sample, state_ssm, state_conv, ada_w, ada_b, norm1_g, norm2_g, gm_w_in, gm_ln_g, gm_ln_b, gm_w_s, gm_b_s, gm_w_out, ssm_w_in, ssm_conv_w, ssm_conv_b, ssm_dt_bias, ssm_a_log, ssm_d, ssm_norm_g, ssm_w_out, mlp_w1, mlp_w2, final_g):
    p = dict(norm1_g=norm1_g, norm2_g=norm2_g, gm_w_in=gm_w_in, gm_ln_g=gm_ln_g, gm_ln_b=gm_ln_b, gm_w_s=gm_w_s,
             gm_b_s=gm_b_s, gm_w_out=gm_w_out, ssm_w_in=ssm_w_in, ssm_conv_w=ssm_conv_w, ssm_conv_b=ssm_conv_b,
             ssm_dt_bias=ssm_dt_bias, ssm_a_log=ssm_a_log, ssm_d=ssm_d, ssm_norm_g=ssm_norm_g, ssm_w_out=ssm_w_out,
             mlp_w1_bf=mlp_w1.astype(BF16), mlp_w2_bf=mlp_w2.astype(BF16), final_g=final_g)
    n_sample = x_sample.shape[0]
    mod = _ada(jnp.concatenate([c_sample, c_prompt], axis=0), ada_w, ada_b)
    y_p, v_p, ssm_p, conv_p = _trunk(x_prompt, mod, n_sample, PROMPT_TILES, None, None, p)
    y_s, v_s, ssm_s, conv_s = _trunk(x_sample, mod, 0, SAMPLE_TILES, state_ssm, state_conv, p)
    return (y_p, y_s, v_p, v_s, ssm_p, conv_p, ssm_s, conv_s)
```

```python
import functools

import jax
import jax.numpy as jnp
from jax import lax
from jax.experimental import pallas as pl
from jax.experimental.pallas import tpu as pltpu

F32 = jnp.float32
BF16 = jnp.bfloat16

D_MODEL = 1024
DEPTH = 2
N_MOD = 6
EPS = 1e-6
LOG2E = 1.4426950408889634
GM_CHUNK = 128
D_GM = 2 * D_MODEL
GM_GROUPS = 8
GM_GROUP_W = D_GM // GM_GROUPS
GM_COL_BLOCK = 1024
D_INNER = 2 * D_MODEL
SSM_HEAD_DIM = 64
SSM_HEADS = D_INNER // SSM_HEAD_DIM
SSM_GROUPS = 8
HEADS_PER_GROUP = SSM_HEADS // SSM_GROUPS
SSM_GROUP_W = D_INNER // SSM_GROUPS
D_STATE = 128
CONV_W = 4
CONV_DIM = D_INNER + 2 * SSM_GROUPS * D_STATE
D_ZX = D_INNER + CONV_DIM
SSM_CHUNK = 128
D_FF = 4 * D_MODEL
FF_CHUNK = 1024
SUBLANES = 8
CONV_PAD = SUBLANES
ROW_TILE = 256
LANE_TILE = 512
VMEM_LIMIT = 56 * 1024 * 1024


def _silu(x):
    return x * (1.0 / (1.0 + jnp.exp2(x * (-LOG2E))))


def _softplus(x):
    return jnp.maximum(x, 0.0) + jnp.log1p(jnp.exp(-jnp.abs(x)))


def _gelu_tanh(x):
    return x * (0.5 * (1.0 + jnp.tanh(0.7978845608028654 * (x + 0.044715 * (x * x * x)))))


def _rms(x, g):
    return x * lax.rsqrt(jnp.mean(x * x, axis=-1, keepdims=True) + EPS) * g


def _norm_modulate(x3, g, shift, scale):
    r, tl, d = x3.shape
    hn = _rms(x3.reshape(r * tl, d), g)
    h3 = hn.reshape(r, tl, d) * (1.0 + scale) + shift
    return h3.reshape(r * tl, d).astype(BF16)


def _const_spec(shape):
    return pl.BlockSpec(shape, lambda *_: (0,) * len(shape), pipeline_mode=pl.Buffered(1))


def _params(*sem):
    return pltpu.CompilerParams(dimension_semantics=sem, vmem_limit_bytes=VMEM_LIMIT)


def _ada_kernel(c_ref, w_ref, b_ref, o_ref):
    sc = _silu(c_ref[...]).astype(BF16)
    res = jnp.dot(sc, w_ref[...].astype(BF16), preferred_element_type=F32) + b_ref[...]
    o_ref[...] = res.reshape(o_ref.shape)


def _ada(c_all, ada_w, ada_b):
    nb = c_all.shape[0]
    return pl.pallas_call(
        _ada_kernel,
        grid=(DEPTH, N_MOD),
        in_specs=[pl.BlockSpec((nb, D_MODEL), lambda i, k: (0, 0)),
                  pl.BlockSpec((None, D_MODEL, D_MODEL), lambda i, k: (i, 0, k)),
                  pl.BlockSpec((None, None, 1, D_MODEL), lambda i, k: (i, k, 0, 0))],
        out_specs=pl.BlockSpec((None, None, nb, 1, D_MODEL), lambda i, k: (i, k, 0, 0, 0)),
        out_shape=jax.ShapeDtypeStruct((DEPTH, N_MOD, nb, 1, D_MODEL), F32),
        compiler_params=_params("parallel", "parallel"),
        name="ada_mod",
    )(c_all, ada_w, ada_b.reshape(DEPTH, N_MOD, 1, D_MODEL))


def _mod_spec(layer, half, r, row0):
    return pl.BlockSpec((None, 3, r, 1, D_MODEL), lambda i, j: (layer, half, row0 // r + i, 0, 0))


def _gm_kernel(x_ref, mod_ref, g_ref, win_ref, lng_ref, lnb_ref, wmix_ref, bexp_ref, wout_ref,
               xo_ref, v_ref, z_ref, *, v_tail):
    r, tl, d = x_ref.shape
    tm = r * tl
    x3 = x_ref[...]
    h = _norm_modulate(x3, g_ref[...], mod_ref[0], mod_ref[1])

    def project(k):
        cols = slice(k * GM_COL_BLOCK, (k + 1) * GM_COL_BLOCK)
        z_ref[:, cols] = _gelu_tanh(jnp.dot(h, win_ref[:, cols], preferred_element_type=F32))

    n_blocks = 2 * D_GM // GM_COL_BLOCK
    for k in range(n_blocks // 2, n_blocks):
        project(k)
    vr = z_ref[:, D_GM:]
    xc = vr - jnp.mean(vr, axis=-1, keepdims=True)
    v = xc * lax.rsqrt(jnp.mean(xc * xc, axis=-1, keepdims=True) + EPS) * lng_ref[...] + lnb_ref[...]
    vb = v.astype(BF16)
    for k in range(n_blocks // 2):
        project(k)
    u = z_ref[:, :D_GM]
    chunks = []
    for c in range(tm // GM_CHUNK):
        vc = vb[c * GM_CHUNK:(c + 1) * GM_CHUNK]
        parts = [jnp.dot(wmix_ref[g], vc[:, g * GM_GROUP_W:(g + 1) * GM_GROUP_W], preferred_element_type=F32)
                 for g in range(GM_GROUPS)]
        chunks.append(jnp.concatenate(parts, axis=1) + bexp_ref[...])
    s = jnp.concatenate(chunks, axis=0) if len(chunks) > 1 else chunks[0]
    y = jnp.dot((u * s).astype(BF16), wout_ref[...], preferred_element_type=F32)
    xo_ref[...] = x3 + mod_ref[2] * y.reshape(r, tl, d)

    @pl.when(pl.program_id(1) == pl.num_programs(1) - 1)
    def _():
        v_ref[...] = v.reshape(r, tl, D_GM)[:, tl - v_tail:, :]


def _gm_layer(x, mod, layer, row0, r, tl, g, w_in, ln_g, ln_b, w_s, b_s, w_out):
    nb, l, d = x.shape
    q = min(l, GM_CHUNK)
    assert l % q == 0 and GM_CHUNK % q == 0 and (r * tl) % GM_CHUNK == 0 and tl % q == 0
    rep = GM_CHUNK // q
    tri = jnp.tril(jnp.ones((q, q), F32))
    wq = w_s[:, :q, :q] * tri
    wmix = jnp.einsum("ab,gts->gatbs", jnp.eye(rep, dtype=F32), wq).reshape(GM_GROUPS, GM_CHUNK, GM_CHUNK)
    bq = jnp.tile(b_s[:, :q].T, (rep, 1))
    bexp = jnp.repeat(bq, GM_GROUP_W, axis=1)
    v_tail = l - ((l - 1) // GM_CHUNK) * GM_CHUNK
    assert v_tail <= tl
    kern = functools.partial(_gm_kernel, v_tail=v_tail)
    xo, v = pl.pallas_call(
        kern,
        grid=(nb // r, l // tl),
        in_specs=[pl.BlockSpec((r, tl, d), lambda i, j: (i, j, 0)),
                  _mod_spec(layer, 0, r, row0),
                  _const_spec((1, d)),
                  _const_spec((d, 2 * D_GM)),
                  _const_spec((1, D_GM)),
                  _const_spec((1, D_GM)),
                  _const_spec((GM_GROUPS, GM_CHUNK, GM_CHUNK)),
                  _const_spec((GM_CHUNK, D_GM)),
                  _const_spec((D_GM, d))],
        out_specs=[pl.BlockSpec((r, tl, d), lambda i, j: (i, j, 0)),
                   pl.BlockSpec((r, v_tail, D_GM), lambda i, j: (i, 0, 0))],
        out_shape=[jax.ShapeDtypeStruct((nb, l, d), F32),
                   jax.ShapeDtypeStruct((nb, v_tail, D_GM), F32)],
        scratch_shapes=[pltpu.VMEM((r * tl, 2 * D_GM), F32)],
        compiler_params=_params("parallel", "arbitrary"),
        name="gmlp_mixer",
    )(x, mod, g.reshape(1, d), w_in.astype(BF16), ln_g.reshape(1, D_GM), ln_b.reshape(1, D_GM),
      wmix.astype(BF16), bexp, w_out.astype(BF16))
    return xo, v


def _mlp_kernel(x_ref, mod_ref, g_ref, w1_ref, w2_ref, gf_ref, o_ref, *, final):
    r, tl, d = x_ref.shape
    x3 = x_ref[...]
    h = _norm_modulate(x3, g_ref[...], mod_ref[0], mod_ref[1])
    acc = jnp.zeros((r * tl, d), F32)
    for k in range(D_FF // FF_CHUNK):
        a = jnp.dot(h, w1_ref[:, k * FF_CHUNK:(k + 1) * FF_CHUNK], preferred_element_type=F32)
        a = jnp.square(jnp.maximum(a, 0.0)).astype(BF16)
        acc = acc + jnp.dot(a, w2_ref[k * FF_CHUNK:(k + 1) * FF_CHUNK, :], preferred_element_type=F32)
    xo = x3 + mod_ref[2] * acc.reshape(r, tl, d)
    if final:
        xo = _rms(xo.reshape(r * tl, d), gf_ref[...]).reshape(r, tl, d)
    o_ref[...] = xo


def _layer_spec(layer, shape):
    return pl.BlockSpec((None,) + shape, lambda *_: (layer,) + (0,) * len(shape), pipeline_mode=pl.Buffered(1))


def _mlp_layer(x, mod, layer, row0, r, tl, g, w1_all, w2_all, final_g, final):
    nb, l, d = x.shape
    kern = functools.partial(_mlp_kernel, final=final)
    return pl.pallas_call(
        kern,
        grid=(nb // r, l // tl),
        in_specs=[pl.BlockSpec((r, tl, d), lambda i, j: (i, j, 0)),
                  _mod_spec(layer, 1, r, row0),
                  _const_spec((1, d)),
                  _layer_spec(layer, (d, D_FF)),
                  _layer_spec(layer, (D_FF, d)),
                  _const_spec((1, d))],
        out_specs=pl.BlockSpec((r, tl, d), lambda i, j: (i, j, 0)),
        out_shape=jax.ShapeDtypeStruct((nb, l, d), F32),
        compiler_params=_params("parallel", "parallel"),
        name="relu2_mlp",
    )(x, mod, g.reshape(1, d), w1_all, w2_all, final_g.reshape(1, d))


def _ssm_in_kernel(x_ref, mod_ref, g_ref, w_ref, cw_ref, cb_ref, dtb_ref, cs_ref,
                   z_ref, xbc_ref, dt_ref, cnew_ref, xp_ref):
    r, tl, d = x_ref.shape

    @pl.when(pl.program_id(1) == 0)
    def _():
        xp_ref[:, 0:CONV_PAD, :] = jnp.zeros((r, CONV_PAD, CONV_DIM), F32)
        xp_ref[:, CONV_PAD - (CONV_W - 1):CONV_PAD, :] = cs_ref[...]

    h = _norm_modulate(x_ref[...], g_ref[...], mod_ref[0], mod_ref[1])
    z_ref[...] = jnp.dot(h, w_ref[:, :D_INNER], preferred_element_type=F32).reshape(r, tl, D_INNER)
    xbc = jnp.dot(h, w_ref[:, D_INNER:D_ZX], preferred_element_type=F32)
    dt_raw = jnp.dot(h, w_ref[:, D_ZX:], preferred_element_type=F32)
    dt_ref[...] = _softplus(dt_raw + dtb_ref[...]).reshape(r, tl, SSM_HEADS)
    xp_ref[:, CONV_PAD:CONV_PAD + tl, :] = xbc.reshape(r, tl, CONV_DIM)
    regs = r * tl // SUBLANES
    cur = xp_ref[:, CONV_PAD:CONV_PAD + tl, :].reshape(regs, SUBLANES, CONV_DIM)
    prv = xp_ref[:, 0:tl, :].reshape(regs, SUBLANES, CONV_DIM)
    sub = lax.broadcasted_iota(jnp.int32, (1, SUBLANES, 1), 1)
    conv = cb_ref[...].reshape(1, 1, CONV_DIM) + cur * cw_ref[CONV_W - 1:CONV_W, :].reshape(1, 1, CONV_DIM)
    for s in range(1, CONV_W):
        shifted = pltpu.roll(jnp.where(sub < SUBLANES - s, cur, prv), s, axis=1)
        conv = conv + shifted * cw_ref[CONV_W - 1 - s:CONV_W - s, :].reshape(1, 1, CONV_DIM)
    xbc_ref[...] = _silu(conv).reshape(r, tl, CONV_DIM)
    cnew_ref[...] = xp_ref[:, tl + CONV_PAD - (CONV_W - 1):tl + CONV_PAD, :]
    xp_ref[:, 0:CONV_PAD, :] = xp_ref[:, tl:tl + CONV_PAD, :]


def _ssm_in(x, mod, layer, row0, r, tl, g, w_in, conv_w, conv_b, dt_bias, conv_state):
    nb, l, d = x.shape
    return pl.pallas_call(
        _ssm_in_kernel,
        grid=(nb // r, l // tl),
        in_specs=[pl.BlockSpec((r, tl, d), lambda i, j: (i, j, 0)),
                  _mod_spec(layer, 0, r, row0),
                  _const_spec((1, d)),
                  _const_spec((d, D_ZX + SSM_HEADS)),
                  _const_spec((CONV_W, CONV_DIM)),
                  _const_spec((1, CONV_DIM)),
                  _const_spec((1, SSM_HEADS)),
                  pl.BlockSpec((r, CONV_W - 1, CONV_DIM), lambda i, j: (i, 0, 0))],
        out_specs=[pl.BlockSpec((r, tl, D_INNER), lambda i, j: (i, j, 0)),
                   pl.BlockSpec((r, tl, CONV_DIM), lambda i, j: (i, j, 0)),
                   pl.BlockSpec((r, tl, SSM_HEADS), lambda i, j: (i, j, 0)),
                   pl.BlockSpec((r, CONV_W - 1, CONV_DIM), lambda i, j: (i, 0, 0))],
        out_shape=[jax.ShapeDtypeStruct((nb, l, D_INNER), F32),
                   jax.ShapeDtypeStruct((nb, l, CONV_DIM), F32),
                   jax.ShapeDtypeStruct((nb, l, SSM_HEADS), F32),
                   jax.ShapeDtypeStruct((nb, CONV_W - 1, CONV_DIM), F32)],
        scratch_shapes=[pltpu.VMEM((r, CONV_PAD + tl, CONV_DIM), F32)],
        compiler_params=_params("parallel", "arbitrary"),
        name="ssm_in_conv",
    )(x, mod, g.reshape(1, d), w_in.astype(BF16), conv_w, conv_b.reshape(1, CONV_DIM),
      dt_bias.reshape(1, SSM_HEADS), conv_state)


NT_DIMS = (((1,), (1,)), ((), ()))
TN_DIMS = (((0,), (0,)), ((), ()))


def _expand_heads(vals, e_bf):
    tm = vals[0].shape[0]
    v = jnp.concatenate(vals, axis=0)
    hi = v.astype(BF16)
    lo = (v - hi.astype(F32)).astype(BF16)
    x = jnp.dot(hi, e_bf, preferred_element_type=F32) + jnp.dot(lo, e_bf, preferred_element_type=F32)
    return [x[k * tm:(k + 1) * tm] for k in range(len(vals))]


def _head_masks():
    lane = lax.broadcasted_iota(jnp.int32, (1, SSM_GROUP_W), 1) // SSM_HEAD_DIM
    return [jnp.where(lane == hr, 1.0, 0.0).astype(BF16) for hr in range(HEADS_PER_GROUP)]


def _ssd_intra(g, cbm, acum, acum_t, xdt_g, hmask):
    acc = None
    for hr in range(HEADS_PER_GROUP):
        hh = g * HEADS_PER_GROUP + hr
        seg = jnp.minimum(acum[:, hh:hh + 1] - acum_t[hh:hh + 1, :], 0.0)
        lmat = (cbm * jnp.exp(seg)).astype(BF16)
        part = jnp.dot(lmat, xdt_g * hmask[hr], preferred_element_type=F32)
        acc = part if acc is None else acc + part
    return acc


def _cumsum_rows(lcum3, da):
    hi = da.astype(BF16)
    rest = da - hi.astype(F32)
    mid = rest.astype(BF16)
    lo = (rest - mid.astype(F32)).astype(BF16)
    return jnp.dot(lcum3, jnp.concatenate([hi, mid, lo], axis=0), preferred_element_type=F32)


def _gate_norm(y_g, z_g, ng_g):
    return _rms(y_g * _silu(z_g), ng_g)


def _ssm_prompt_kernel(x_ref, mod_ref, g_ref, w_ref, cw_ref, cb_ref, dtb_ref, alog_ref, dexp_ref, lcum_ref, e_ref,
                       ng_ref, wout_ref, xo_ref, cnew_ref, s_ref, xp_ref, z_ref, xc_ref, yn_ref, st_ref):
    _, tl, _ = x_ref.shape
    j = pl.program_id(1)

    @pl.when(j == 0)
    def _():
        xp_ref[0:CONV_PAD, :] = jnp.zeros((CONV_PAD, CONV_DIM), F32)
        st_ref[...] = jnp.zeros_like(st_ref)

    h = _norm_modulate(x_ref[...], g_ref[...], mod_ref[0], mod_ref[1])
    xp_ref[CONV_PAD:CONV_PAD + tl, :] = jnp.dot(h, w_ref[:, D_INNER:D_ZX], preferred_element_type=F32)
    dt_all = _softplus(jnp.dot(h, w_ref[:, D_ZX:], preferred_element_type=F32) + dtb_ref[...])
    z_ref[...] = jnp.dot(h, w_ref[:, :D_INNER], preferred_element_type=F32)

    sub = lax.broadcasted_iota(jnp.int32, (1, SUBLANES, 1), 1)
    for lb in range(CONV_DIM // LANE_TILE):
        lns = slice(lb * LANE_TILE, (lb + 1) * LANE_TILE)
        taps = [cw_ref[k:k + 1, lns].reshape(1, 1, LANE_TILE) for k in range(CONV_W)]
        bias = cb_ref[:, lns].reshape(1, 1, LANE_TILE)
        for rb in range(tl // ROW_TILE):
            r0 = rb * ROW_TILE
            cur = xp_ref[CONV_PAD + r0:CONV_PAD + r0 + ROW_TILE, lns]
            prv = xp_ref[r0:r0 + ROW_TILE, lns]
            cur = cur.reshape(ROW_TILE // SUBLANES, SUBLANES, LANE_TILE)
            prv = prv.reshape(ROW_TILE // SUBLANES, SUBLANES, LANE_TILE)
            conv = bias + cur * taps[CONV_W - 1]
            for s in range(1, CONV_W):
                conv = conv + pltpu.roll(jnp.where(sub < SUBLANES - s, cur, prv), s, axis=1) * taps[CONV_W - 1 - s]
            xc_ref[r0:r0 + ROW_TILE, lns] = _silu(conv).reshape(ROW_TILE, LANE_TILE)
    cnew_ref[0] = xp_ref[tl + CONV_PAD - (CONV_W - 1):tl + CONV_PAD, :]
    xp_ref[0:CONV_PAD, :] = xp_ref[tl:tl + CONV_PAD, :]

    a_neg = -jnp.exp(alog_ref[...])
    lcum3 = lcum_ref[...]
    causal = lcum3[:, :SSM_CHUNK].astype(F32) > 0.5
    hmask = _head_masks()
    for c in range(tl // SSM_CHUNK):
        rows = slice(c * SSM_CHUNK, (c + 1) * SSM_CHUNK)
        dt = dt_all[rows]
        acum = _cumsum_rows(lcum3, dt * a_neg)
        alast = acum[SSM_CHUNK - 1:SSM_CHUNK, :]
        acum2 = acum * LOG2E
        ldt_t = jnp.log2(dt).T
        src_t = acum2.T - ldt_t
        heads = jnp.concatenate([jnp.exp(acum), dt * jnp.exp(alast - acum)], axis=0)
        heads_hi = heads.astype(BF16)
        heads_lo = (heads - heads_hi.astype(F32)).astype(BF16)
        ex = jnp.dot(jnp.concatenate([heads_hi, heads_lo], axis=1), e_ref[...], preferred_element_type=F32)
        for g in range(SSM_GROUPS):
            ch = slice(g * SSM_GROUP_W, (g + 1) * SSM_GROUP_W)
            ea_x = ex[:SSM_CHUNK, ch]
            w_x = ex[SSM_CHUNK:, ch]
            xs = xc_ref[rows, ch]
            xs_bf = xs.astype(BF16)
            if g % 2 == 0:
                b_lo = D_INNER + g * D_STATE
                c_lo = D_INNER + (SSM_GROUPS + g) * D_STATE
                bgt_pair = [xc_ref[rows, b_lo + k * D_STATE:b_lo + (k + 1) * D_STATE].T.astype(BF16) for k in (0, 1)]
                cg_pair = [xc_ref[rows, c_lo + k * D_STATE:c_lo + (k + 1) * D_STATE].astype(BF16) for k in (0, 1)]
                zero = jnp.zeros((D_STATE, SSM_CHUNK), BF16)
                bdiag = jnp.concatenate([jnp.concatenate([bgt_pair[0], zero], axis=1),
                                         jnp.concatenate([zero, bgt_pair[1]], axis=1)], axis=0)
                cb_pair = jnp.dot(jnp.concatenate(cg_pair, axis=1), bdiag, preferred_element_type=F32)
            bgt = bgt_pair[g % 2]
            cg = cg_pair[g % 2]
            cbm = jnp.where(causal, cb_pair[:, (g % 2) * SSM_CHUNK:(g % 2 + 1) * SSM_CHUNK], 0.0)
            lmats = []
            for hr in range(HEADS_PER_GROUP):
                hh = g * HEADS_PER_GROUP + hr
                seg = jnp.minimum(acum2[:, hh:hh + 1] - src_t[hh:hh + 1, :], ldt_t[hh:hh + 1, :])
                lmats.append((cbm * jnp.exp2(seg)).astype(BF16))
            x_stack = jnp.concatenate([xs_bf * hmask[hr] for hr in range(HEADS_PER_GROUP)], axis=0)
            y_g = xs * dexp_ref[:, ch] + jnp.dot(jnp.concatenate(lmats, axis=1), x_stack,
                                                 preferred_element_type=F32)
            s_old = st_ref[:, ch]
            y_g = y_g + jnp.dot(cg, s_old.astype(BF16), preferred_element_type=F32) * ea_x
            st_ref[:, ch] = (s_old * ea_x[SSM_CHUNK - 1:SSM_CHUNK, :]
                             + jnp.dot(bgt, (xs * w_x).astype(BF16), preferred_element_type=F32))
            yn_ref[rows, ch] = _gate_norm(y_g, z_ref[rows, ch], ng_ref[:, ch]).astype(BF16)

    out = jnp.dot(yn_ref[...], wout_ref[...], preferred_element_type=F32)
    xo_ref[0] = x_ref[0] + mod_ref[2, 0] * out

    @pl.when(j == pl.num_programs(1) - 1)
    def _():
        s_ref[0] = st_ref[...].T


def _ssd_sample_kernel(xbc_ref, dt_ref, z_ref, alog_ref, dexp_ref, lcum_ref, bd_ref, e_ref, ng_ref, s0_ref,
                       yn_ref, s_ref):
    r, tl, _ = xbc_ref.shape
    tm = r * tl
    xbc = xbc_ref[...].reshape(tm, CONV_DIM)
    xs = xbc[:, :D_INNER]
    dt = dt_ref[...].reshape(tm, SSM_HEADS)
    da = dt * (-jnp.exp(alog_ref[...]))
    lcum3 = lcum_ref[...]
    causal = lcum3[:, :tm].astype(F32) > 0.5
    acum = _cumsum_rows(lcum3, da)
    alast = _cumsum_rows(bd_ref[...], da)
    acum_t = acum.T
    dec_end_t = jnp.exp(alast.T)
    dt_x, ea_x, w_x = _expand_heads([dt, jnp.exp(acum), dt * jnp.exp(alast - acum)], e_ref[...])
    xdt = (xs * dt_x).astype(BF16)
    xw = xs * w_x
    hmask = _head_masks()
    row = lax.broadcasted_iota(jnp.int32, (tm, 1), 0) // tl
    dec_cols = [jnp.broadcast_to(dec_end_t[:, b * tl:b * tl + 1], (SSM_HEADS, D_STATE)) for b in range(r)]
    for g in range(SSM_GROUPS):
        ch = slice(g * SSM_GROUP_W, (g + 1) * SSM_GROUP_W)
        bg = xbc[:, D_INNER + g * D_STATE:D_INNER + (g + 1) * D_STATE].astype(BF16)
        cg32 = xbc[:, D_INNER + (SSM_GROUPS + g) * D_STATE:D_INNER + (SSM_GROUPS + g + 1) * D_STATE]
        cbm = jnp.where(causal, lax.dot_general(cg32.astype(BF16), bg, NT_DIMS, preferred_element_type=F32), 0.0)
        y_g = xs[:, ch] * dexp_ref[:, ch] + _ssd_intra(g, cbm, acum, acum_t, xdt[:, ch], hmask)
        y_off = jnp.zeros((tm, SSM_GROUP_W), F32)
        xw_g = xw[:, ch]
        for b in range(r):
            s_old = s0_ref[b, ch, :]
            c_b = jnp.where(row == b, cg32, 0.0).astype(BF16)
            xw_b = jnp.where(row == b, xw_g, 0.0).astype(BF16)
            y_off = y_off + lax.dot_general(c_b, s_old.astype(BF16), NT_DIMS, preferred_element_type=F32)
            upd = lax.dot_general(xw_b, bg, TN_DIMS, preferred_element_type=F32)
            dec = jnp.concatenate(
                [jnp.broadcast_to(dec_cols[b][g * HEADS_PER_GROUP + hr:g * HEADS_PER_GROUP + hr + 1, :],
                                  (SSM_HEAD_DIM, D_STATE)) for hr in range(HEADS_PER_GROUP)], axis=0)
            s_ref[b, ch, :] = s_old * dec + upd
        z_g = z_ref[:, :, ch].reshape(tm, SSM_GROUP_W)
        yn_g = _gate_norm(y_g + y_off * ea_x[:, ch], z_g, ng_ref[:, ch])
        yn_ref[:, :, ch] = yn_g.reshape(r, tl, SSM_GROUP_W).astype(BF16)


def _ssd_consts(d_skip):
    e_bf = jnp.repeat(jnp.eye(SSM_HEADS, dtype=F32), SSM_HEAD_DIM, axis=1).astype(BF16)
    dexp = jnp.repeat(d_skip, SSM_HEAD_DIM).reshape(1, D_INNER)
    return e_bf, dexp


def _ssm_prompt(x, mod, layer, row0, tl, g, w_in, conv_w, conv_b, dt_bias, a_log, d_skip, norm_g, w_out):
    nb, l, d = x.shape
    assert l % tl == 0 and tl % SSM_CHUNK == 0 and tl % ROW_TILE == 0
    e_bf, dexp = _ssd_consts(d_skip)
    return pl.pallas_call(
        _ssm_prompt_kernel,
        grid=(nb, l // tl),
        in_specs=[pl.BlockSpec((1, tl, d), lambda i, j: (i, j, 0)),
                  _mod_spec(layer, 0, 1, row0),
                  _const_spec((1, d)),
                  _const_spec((d, D_ZX + SSM_HEADS)),
                  _const_spec((CONV_W, CONV_DIM)),
                  _const_spec((1, CONV_DIM)),
                  _const_spec((1, SSM_HEADS)),
                  _const_spec((1, SSM_HEADS)),
                  _const_spec((1, D_INNER)),
                  _const_spec((SSM_CHUNK, 3 * SSM_CHUNK)),
                  _const_spec((2 * SSM_HEADS, D_INNER)),
                  _const_spec((1, D_INNER)),
                  _const_spec((D_INNER, d))],
        out_specs=[pl.BlockSpec((1, tl, d), lambda i, j: (i, j, 0)),
                   pl.BlockSpec((1, CONV_W - 1, CONV_DIM), lambda i, j: (i, 0, 0)),
                   pl.BlockSpec((1, D_INNER, D_STATE), lambda i, j: (i, 0, 0))],
        out_shape=[jax.ShapeDtypeStruct((nb, l, d), F32),
                   jax.ShapeDtypeStruct((nb, CONV_W - 1, CONV_DIM), F32),
                   jax.ShapeDtypeStruct((nb, D_INNER, D_STATE), F32)],
        scratch_shapes=[pltpu.VMEM((CONV_PAD + tl, CONV_DIM), F32),
                        pltpu.VMEM((tl, D_INNER), F32),
                        pltpu.VMEM((tl, CONV_DIM), F32),
                        pltpu.VMEM((tl, D_INNER), BF16),
                        pltpu.VMEM((D_STATE, D_INNER), F32)],
        compiler_params=_params("parallel", "arbitrary"),
        name="ssm_prompt",
    )(x, mod, g.reshape(1, d), w_in.astype(BF16), conv_w, conv_b.reshape(1, CONV_DIM),
      dt_bias.reshape(1, SSM_HEADS), a_log.reshape(1, SSM_HEADS), dexp,
      jnp.tile(jnp.tril(jnp.ones((SSM_CHUNK, SSM_CHUNK), BF16)), (1, 3)), jnp.concatenate([e_bf, e_bf], axis=0),
      norm_g.reshape(1, D_INNER), w_out.astype(BF16))


def _ssd_sample(xbc, dt, z, a_log, d_skip, norm_g, state0, r):
    nb, l, _ = xbc.shape
    tm = r * l
    assert l <= SSM_CHUNK and nb % r == 0
    eye_r = jnp.eye(r, dtype=F32)
    lcum = jnp.tile(jnp.kron(eye_r, jnp.tril(jnp.ones((l, l), F32))).astype(BF16), (1, 3))
    bd = jnp.tile(jnp.kron(eye_r, jnp.ones((l, l), F32)).astype(BF16), (1, 3))
    e_bf, dexp = _ssd_consts(d_skip)
    return pl.pallas_call(
        _ssd_sample_kernel,
        grid=(nb // r,),
        in_specs=[pl.BlockSpec((r, l, CONV_DIM), lambda i: (i, 0, 0)),
                  pl.BlockSpec((r, l, SSM_HEADS), lambda i: (i, 0, 0)),
                  pl.BlockSpec((r, l, D_INNER), lambda i: (i, 0, 0)),
                  _const_spec((1, SSM_HEADS)),
                  _const_spec((1, D_INNER)),
                  _const_spec((tm, 3 * tm)),
                  _const_spec((tm, 3 * tm)),
                  _const_spec((SSM_HEADS, D_INNER)),
                  _const_spec((1, D_INNER)),
                  pl.BlockSpec((r, D_INNER, D_STATE), lambda i: (i, 0, 0))],
        out_specs=[pl.BlockSpec((r, l, D_INNER), lambda i: (i, 0, 0)),
                   pl.BlockSpec((r, D_INNER, D_STATE), lambda i: (i, 0, 0))],
        out_shape=[jax.ShapeDtypeStruct((nb, l, D_INNER), BF16),
                   jax.ShapeDtypeStruct((nb, D_INNER, D_STATE), F32)],
        compiler_params=_params("parallel"),
        name="ssd_scan_sample",
    )(xbc, dt, z, a_log.reshape(1, SSM_HEADS), dexp, lcum, bd, e_bf, norm_g.reshape(1, D_INNER), state0)


def _ssm_out_kernel(x_ref, yn_ref, mod_ref, wout_ref, o_ref):
    r, tl, d = x_ref.shape
    if r == 1:
        yn = yn_ref[0]
    else:
        yn = yn_ref[...].astype(F32).reshape(r * tl, D_INNER).astype(BF16)
    out = jnp.dot(yn, wout_ref[...], preferred_element_type=F32)
    o_ref[...] = x_ref[...] + mod_ref[2] * out.reshape(r, tl, d)


def _ssm_out(x, yn, mod, layer, row0, r, tl, w_out):
    nb, l, d = x.shape
    return pl.pallas_call(
        _ssm_out_kernel,
        grid=(nb // r, l // tl),
        in_specs=[pl.BlockSpec((r, tl, d), lambda i, j: (i, j, 0)),
                  pl.BlockSpec((r, tl, D_INNER), lambda i, j: (i, j, 0)),
                  _mod_spec(layer, 0, r, row0),
                  _const_spec((D_INNER, d))],
        out_specs=pl.BlockSpec((r, tl, d), lambda i, j: (i, j, 0)),
        out_shape=jax.ShapeDtypeStruct((nb, l, d), F32),
        compiler_params=_params("parallel", "parallel"),
        name="ssm_out_proj",
    )(x, yn, mod, w_out.astype(BF16))


def _trunk(x, mod, row0, tiles, ssm_state, conv_state, p):
    nb, l, _ = x.shape
    new_v, new_ssm, new_conv = [], [], []
    for i in range(DEPTH):
        j = i // 2
        if i % 2 == 0:
            x, v = _gm_layer(x, mod, i, row0, *tiles["gm"], p["norm1_g"][i], p["gm_w_in"][j], p["gm_ln_g"][j],
                             p["gm_ln_b"][j], p["gm_w_s"][j], p["gm_b_s"][j], p["gm_w_out"][j])
            new_v.append(v)
        else:
            if ssm_state is None:
                x, cnew, s_new = _ssm_prompt(x, mod, i, row0, tiles["ssm"], p["norm1_g"][i], p["ssm_w_in"][j],
                                             p["ssm_conv_w"][j], p["ssm_conv_b"][j], p["ssm_dt_bias"][j],
                                             p["ssm_a_log"][j], p["ssm_d"][j], p["ssm_norm_g"][j],
                                             p["ssm_w_out"][j])
            else:
                z, xbc, dt, cnew = _ssm_in(x, mod, i, row0, *tiles["ssm_in"], p["norm1_g"][i], p["ssm_w_in"][j],
                                           p["ssm_conv_w"][j], p["ssm_conv_b"][j], p["ssm_dt_bias"][j],
                                           conv_state[j])
                s0 = ssm_state[j].reshape(nb, D_INNER, D_STATE)
                yn, s_new = _ssd_sample(xbc, dt, z, p["ssm_a_log"][j], p["ssm_d"][j], p["ssm_norm_g"][j], s0,
                                        tiles["ssd"])
                x = _ssm_out(x, yn, mod, i, row0, *tiles["ssm_out"], p["ssm_w_out"][j])
            new_conv.append(cnew)
            new_ssm.append(s_new.reshape(nb, SSM_HEADS, SSM_HEAD_DIM, D_STATE))
        x = _mlp_layer(x, mod, i, row0, *tiles["mlp"], p["norm2_g"][i], p["mlp_w1_bf"], p["mlp_w2_bf"],
                       p["final_g"], final=(i == DEPTH - 1))
    return x, jnp.stack(new_v), jnp.stack(new_ssm), jnp.stack(new_conv)


PROMPT_TILES = {"gm": (1, 512), "mlp": (1, 1024), "ssm": 512}
SAMPLE_TILES = {"gm": (64, 8), "mlp": (64, 8), "ssm_in": (32, 8), "ssd": 8, "ssm_out": (64, 8)}


def kernel(x_prompt, x_sample, c_prompt, c_sample, state_ssm, state_conv, ada_w, ada_b, norm1_g, norm2_g, gm_w_in, gm_ln_g, gm_ln_b, gm_w_s, gm_b_s, gm_w_out, ssm_w_in, ssm_conv_w, ssm_conv_b, ssm_dt_bias, ssm_a_log, ssm_d, ssm_norm_g, ssm_w_out, mlp_w1, mlp_w2, final_g):
    p = dict(norm1_g=norm1_g, norm2_g=norm2_g, gm_w_in=gm_w_in, gm_ln_g=gm_ln_g, gm_ln_b=gm_ln_b, gm_w_s=gm_w_s,
             gm_b_s=gm_b_s, gm_w_out=gm_w_out, ssm_w_in=ssm_w_in, ssm_conv_w=ssm_conv_w, ssm_conv_b=ssm_conv_b,
             ssm_dt_bias=ssm_dt_bias, ssm_a_log=ssm_a_log, ssm_d=ssm_d, ssm_norm_g=ssm_norm_g, ssm_w_out=ssm_w_out,
             mlp_w1_bf=mlp_w1.astype(BF16), mlp_w2_bf=mlp_w2.astype(BF16), final_g=final_g)
    n_sample = x_sample.shape[0]
    mod = _ada(jnp.concatenate([c_sample, c_prompt], axis=0), ada_w, ada_b)
    y_p, v_p, ssm_p, conv_p = _trunk(x_prompt, mod, n_sample, PROMPT_TILES, None, None, p)
    y_s, v_s, ssm_s, conv_s = _trunk(x_sample, mod, 0, SAMPLE_TILES, state_ssm, state_conv, p)
    return (y_p, y_s, v_p, v_s, ssm_p, conv_p, ssm_s, conv_s)
```

```python
import functools

import jax
import jax.numpy as jnp
from jax import lax
from jax.experimental import pallas as pl
from jax.experimental.pallas import tpu as pltpu

F32 = jnp.float32
BF16 = jnp.bfloat16

D_MODEL = 1024
DEPTH = 2
N_MOD = 6
EPS = 1e-6
LOG2E = 1.4426950408889634
GM_CHUNK = 128
D_GM = 2 * D_MODEL
GM_GROUPS = 8
GM_GROUP_W = D_GM // GM_GROUPS
GM_COL_BLOCK = 1024
D_INNER = 2 * D_MODEL
SSM_HEAD_DIM = 64
SSM_HEADS = D_INNER // SSM_HEAD_DIM
SSM_GROUPS = 8
HEADS_PER_GROUP = SSM_HEADS // SSM_GROUPS
SSM_GROUP_W = D_INNER // SSM_GROUPS
D_STATE = 128
CONV_W = 4
CONV_DIM = D_INNER + 2 * SSM_GROUPS * D_STATE
D_ZX = D_INNER + CONV_DIM
SSM_CHUNK = 128
D_FF = 4 * D_MODEL
FF_CHUNK = 1024
SUBLANES = 8
CONV_PAD = SUBLANES
ROW_TILE = 256
LANE_TILE = 512
VMEM_LIMIT = 56 * 1024 * 1024


def _silu(x):
    return x * (1.0 / (1.0 + jnp.exp2(x * (-LOG2E))))


def _softplus(x):
    return jnp.maximum(x, 0.0) + jnp.log1p(jnp.exp(-jnp.abs(x)))


def _gelu_tanh(x):
    return x * (0.5 * (1.0 + jnp.tanh(0.7978845608028654 * (x + 0.044715 * (x * x * x)))))


def _rms(x, g):
    return x * lax.rsqrt(jnp.mean(x * x, axis=-1, keepdims=True) + EPS) * g


def _norm_modulate(x3, g, shift, scale):
    r, tl, d = x3.shape
    hn = _rms(x3.reshape(r * tl, d), g)
    h3 = hn.reshape(r, tl, d) * (1.0 + scale) + shift
    return h3.reshape(r * tl, d).astype(BF16)


def _const_spec(shape):
    return pl.BlockSpec(shape, lambda *_: (0,) * len(shape), pipeline_mode=pl.Buffered(1))


def _params(*sem):
    return pltpu.CompilerParams(dimension_semantics=sem, vmem_limit_bytes=VMEM_LIMIT)


def _ada_kernel(c_ref, w_ref, b_ref, o_ref):
    sc = _silu(c_ref[...]).astype(BF16)
    res = jnp.dot(sc, w_ref[...].astype(BF16), preferred_element_type=F32) + b_ref[...]
    o_ref[...] = res.reshape(o_ref.shape)


def _ada(c_all, ada_w, ada_b):
    nb = c_all.shape[0]
    return pl.pallas_call(
        _ada_kernel,
        grid=(DEPTH, N_MOD),
        in_specs=[pl.BlockSpec((nb, D_MODEL), lambda i, k: (0, 0)),
                  pl.BlockSpec((None, D_MODEL, D_MODEL), lambda i, k: (i, 0, k)),
                  pl.BlockSpec((None, None, 1, D_MODEL), lambda i, k: (i, k, 0, 0))],
        out_specs=pl.BlockSpec((None, None, nb, 1, D_MODEL), lambda i, k: (i, k, 0, 0, 0)),
        out_shape=jax.ShapeDtypeStruct((DEPTH, N_MOD, nb, 1, D_MODEL), F32),
        compiler_params=_params("parallel", "parallel"),
        name="ada_mod",
    )(c_all, ada_w, ada_b.reshape(DEPTH, N_MOD, 1, D_MODEL))


def _mod_spec(layer, half, r, row0):
    return pl.BlockSpec((None, 3, r, 1, D_MODEL), lambda i, j: (layer, half, row0 // r + i, 0, 0))


def _gm_kernel(x_ref, mod_ref, g_ref, win_ref, lng_ref, lnb_ref, wmix_ref, bexp_ref, wout_ref,
               xo_ref, v_ref, z_ref, *, v_tail):
    r, tl, d = x_ref.shape
    tm = r * tl
    x3 = x_ref[...]
    h = _norm_modulate(x3, g_ref[...], mod_ref[0], mod_ref[1])

    def project(k):
        cols = slice(k * GM_COL_BLOCK, (k + 1) * GM_COL_BLOCK)
        z_ref[:, cols] = _gelu_tanh(jnp.dot(h, win_ref[:, cols], preferred_element_type=F32))

    n_blocks = 2 * D_GM // GM_COL_BLOCK
    for k in range(n_blocks // 2, n_blocks):
        project(k)
    vr = z_ref[:, D_GM:]
    xc = vr - jnp.mean(vr, axis=-1, keepdims=True)
    v = xc * lax.rsqrt(jnp.mean(xc * xc, axis=-1, keepdims=True) + EPS) * lng_ref[...] + lnb_ref[...]
    vb = v.astype(BF16)
    for k in range(n_blocks // 2):
        project(k)
    u = z_ref[:, :D_GM]
    chunks = []
    for c in range(tm // GM_CHUNK):
        vc = vb[c * GM_CHUNK:(c + 1) * GM_CHUNK]
        parts = [jnp.dot(wmix_ref[g], vc[:, g * GM_GROUP_W:(g + 1) * GM_GROUP_W], preferred_element_type=F32)
                 for g in range(GM_GROUPS)]
        chunks.append(jnp.concatenate(parts, axis=1) + bexp_ref[...])
    s = jnp.concatenate(chunks, axis=0) if len(chunks) > 1 else chunks[0]
    y = jnp.dot((u * s).astype(BF16), wout_ref[...], preferred_element_type=F32)
    xo_ref[...] = x3 + mod_ref[2] * y.reshape(r, tl, d)

    @pl.when(pl.program_id(1) == pl.num_programs(1) - 1)
    def _():
        v_ref[...] = v.reshape(r, tl, D_GM)[:, tl - v_tail:, :]


def _gm_layer(x, mod, layer, row0, r, tl, g, w_in, ln_g, ln_b, w_s, b_s, w_out):
    nb, l, d = x.shape
    q = min(l, GM_CHUNK)
    assert l % q == 0 and GM_CHUNK % q == 0 and (r * tl) % GM_CHUNK == 0 and tl % q == 0
    rep = GM_CHUNK // q
    tri = jnp.tril(jnp.ones((q, q), F32))
    wq = w_s[:, :q, :q] * tri
    wmix = jnp.einsum("ab,gts->gatbs", jnp.eye(rep, dtype=F32), wq).reshape(GM_GROUPS, GM_CHUNK, GM_CHUNK)
    bq = jnp.tile(b_s[:, :q].T, (rep, 1))
    bexp = jnp.repeat(bq, GM_GROUP_W, axis=1)
    v_tail = l - ((l - 1) // GM_CHUNK) * GM_CHUNK
    assert v_tail <= tl
    kern = functools.partial(_gm_kernel, v_tail=v_tail)
    xo, v = pl.pallas_call(
        kern,
        grid=(nb // r, l // tl),
        in_specs=[pl.BlockSpec((r, tl, d), lambda i, j: (i, j, 0)),
                  _mod_spec(layer, 0, r, row0),
                  _const_spec((1, d)),
                  _const_spec((d, 2 * D_GM)),
                  _const_spec((1, D_GM)),
                  _const_spec((1, D_GM)),
                  _const_spec((GM_GROUPS, GM_CHUNK, GM_CHUNK)),
                  _const_spec((GM_CHUNK, D_GM)),
                  _const_spec((D_GM, d))],
        out_specs=[pl.BlockSpec((r, tl, d), lambda i, j: (i, j, 0)),
                   pl.BlockSpec((r, v_tail, D_GM), lambda i, j: (i, 0, 0))],
        out_shape=[jax.ShapeDtypeStruct((nb, l, d), F32),
                   jax.ShapeDtypeStruct((nb, v_tail, D_GM), F32)],
        scratch_shapes=[pltpu.VMEM((r * tl, 2 * D_GM), F32)],
        compiler_params=_params("parallel", "arbitrary"),
        name="gmlp_mixer",
    )(x, mod, g.reshape(1, d), w_in.astype(BF16), ln_g.reshape(1, D_GM), ln_b.reshape(1, D_GM),
      wmix.astype(BF16), bexp, w_out.astype(BF16))
    return xo, v


def _mlp_kernel(x_ref, mod_ref, g_ref, w1_ref, w2_ref, gf_ref, o_ref, *, final):
    r, tl, d = x_ref.shape
    x3 = x_ref[...]
    h = _norm_modulate(x3, g_ref[...], mod_ref[0], mod_ref[1])
    acc = jnp.zeros((r * tl, d), F32)
    for k in range(D_FF // FF_CHUNK):
        a = jnp.dot(h, w1_ref[:, k * FF_CHUNK:(k + 1) * FF_CHUNK], preferred_element_type=F32)
        a = jnp.square(jnp.maximum(a, 0.0)).astype(BF16)
        acc = acc + jnp.dot(a, w2_ref[k * FF_CHUNK:(k + 1) * FF_CHUNK, :], preferred_element_type=F32)
    xo = x3 + mod_ref[2] * acc.reshape(r, tl, d)
    if final:
        xo = _rms(xo.reshape(r * tl, d), gf_ref[...]).reshape(r, tl, d)
    o_ref[...] = xo


def _layer_spec(layer, shape):
    return pl.BlockSpec((None,) + shape, lambda *_: (layer,) + (0,) * len(shape), pipeline_mode=pl.Buffered(1))


def _mlp_layer(x, mod, layer, row0, r, tl, g, w1_all, w2_all, final_g, final):
    nb, l, d = x.shape
    kern = functools.partial(_mlp_kernel, final=final)
    return pl.pallas_call(
        kern,
        grid=(nb // r, l // tl),
        in_specs=[pl.BlockSpec((r, tl, d), lambda i, j: (i, j, 0)),
                  _mod_spec(layer, 1, r, row0),
                  _const_spec((1, d)),
                  _layer_spec(layer, (d, D_FF)),
                  _layer_spec(layer, (D_FF, d)),
                  _const_spec((1, d))],
        out_specs=pl.BlockSpec((r, tl, d), lambda i, j: (i, j, 0)),
        out_shape=jax.ShapeDtypeStruct((nb, l, d), F32),
        compiler_params=_params("parallel", "parallel"),
        name="relu2_mlp",
    )(x, mod, g.reshape(1, d), w1_all, w2_all, final_g.reshape(1, d))


def _ssm_in_kernel(x_ref, mod_ref, g_ref, w_ref, cw_ref, cb_ref, dtb_ref, cs_ref,
                   z_ref, xbc_ref, dt_ref, cnew_ref, xp_ref):
    r, tl, d = x_ref.shape

    @pl.when(pl.program_id(1) == 0)
    def _():
        xp_ref[:, 0:CONV_PAD, :] = jnp.zeros((r, CONV_PAD, CONV_DIM), F32)
        xp_ref[:, CONV_PAD - (CONV_W - 1):CONV_PAD, :] = cs_ref[...]

    h = _norm_modulate(x_ref[...], g_ref[...], mod_ref[0], mod_ref[1])
    z_ref[...] = jnp.dot(h, w_ref[:, :D_INNER], preferred_element_type=F32).reshape(r, tl, D_INNER)
    xbc = jnp.dot(h, w_ref[:, D_INNER:D_ZX], preferred_element_type=F32)
    dt_raw = jnp.dot(h, w_ref[:, D_ZX:], preferred_element_type=F32)
    dt_ref[...] = _softplus(dt_raw + dtb_ref[...]).reshape(r, tl, SSM_HEADS)
    xp_ref[:, CONV_PAD:CONV_PAD + tl, :] = xbc.reshape(r, tl, CONV_DIM)
    regs = r * tl // SUBLANES
    cur = xp_ref[:, CONV_PAD:CONV_PAD + tl, :].reshape(regs, SUBLANES, CONV_DIM)
    prv = xp_ref[:, 0:tl, :].reshape(regs, SUBLANES, CONV_DIM)
    sub = lax.broadcasted_iota(jnp.int32, (1, SUBLANES, 1), 1)
    conv = cb_ref[...].reshape(1, 1, CONV_DIM) + cur * cw_ref[CONV_W - 1:CONV_W, :].reshape(1, 1, CONV_DIM)
    for s in range(1, CONV_W):
        shifted = pltpu.roll(jnp.where(sub < SUBLANES - s, cur, prv), s, axis=1)
        conv = conv + shifted * cw_ref[CONV_W - 1 - s:CONV_W - s, :].reshape(1, 1, CONV_DIM)
    xbc_ref[...] = _silu(conv).reshape(r, tl, CONV_DIM)
    cnew_ref[...] = xp_ref[:, tl + CONV_PAD - (CONV_W - 1):tl + CONV_PAD, :]
    xp_ref[:, 0:CONV_PAD, :] = xp_ref[:, tl:tl + CONV_PAD, :]


def _ssm_in(x, mod, layer, row0, r, tl, g, w_in, conv_w, conv_b, dt_bias, conv_state):
    nb, l, d = x.shape
    return pl.pallas_call(
        _ssm_in_kernel,
        grid=(nb // r, l // tl),
        in_specs=[pl.BlockSpec((r, tl, d), lambda i, j: (i, j, 0)),
                  _mod_spec(layer, 0, r, row0),
                  _const_spec((1, d)),
                  _const_spec((d, D_ZX + SSM_HEADS)),
                  _const_spec((CONV_W, CONV_DIM)),
                  _const_spec((1, CONV_DIM)),
                  _const_spec((1, SSM_HEADS)),
                  pl.BlockSpec((r, CONV_W - 1, CONV_DIM), lambda i, j: (i, 0, 0))],
        out_specs=[pl.BlockSpec((r, tl, D_INNER), lambda i, j: (i, j, 0)),
                   pl.BlockSpec((r, tl, CONV_DIM), lambda i, j: (i, j, 0)),
                   pl.BlockSpec((r, tl, SSM_HEADS), lambda i, j: (i, j, 0)),
                   pl.BlockSpec((r, CONV_W - 1, CONV_DIM), lambda i, j: (i, 0, 0))],
        out_shape=[jax.ShapeDtypeStruct((nb, l, D_INNER), F32),
                   jax.ShapeDtypeStruct((nb, l, CONV_DIM), F32),
                   jax.ShapeDtypeStruct((nb, l, SSM_HEADS), F32),
                   jax.ShapeDtypeStruct((nb, CONV_W - 1, CONV_DIM), F32)],
        scratch_shapes=[pltpu.VMEM((r, CONV_PAD + tl, CONV_DIM), F32)],
        compiler_params=_params("parallel", "arbitrary"),
        name="ssm_in_conv",
    )(x, mod, g.reshape(1, d), w_in.astype(BF16), conv_w, conv_b.reshape(1, CONV_DIM),
      dt_bias.reshape(1, SSM_HEADS), conv_state)


NT_DIMS = (((1,), (1,)), ((), ()))
TN_DIMS = (((0,), (0,)), ((), ()))


def _expand_heads(vals, e_bf):
    tm = vals[0].shape[0]
    v = jnp.concatenate(vals, axis=0)
    hi = v.astype(BF16)
    lo = (v - hi.astype(F32)).astype(BF16)
    x = jnp.dot(hi, e_bf, preferred_element_type=F32) + jnp.dot(lo, e_bf, preferred_element_type=F32)
    return [x[k * tm:(k + 1) * tm] for k in range(len(vals))]


def _head_masks():
    lane = lax.broadcasted_iota(jnp.int32, (1, SSM_GROUP_W), 1) // SSM_HEAD_DIM
    return [jnp.where(lane == hr, 1.0, 0.0).astype(BF16) for hr in range(HEADS_PER_GROUP)]


def _ssd_intra(g, cbm, acum, acum_t, xdt_g, hmask):
    acc = None
    for hr in range(HEADS_PER_GROUP):
        hh = g * HEADS_PER_GROUP + hr
        seg = jnp.minimum(acum[:, hh:hh + 1] - acum_t[hh:hh + 1, :], 0.0)
        lmat = (cbm * jnp.exp(seg)).astype(BF16)
        part = jnp.dot(lmat, xdt_g * hmask[hr], preferred_element_type=F32)
        acc = part if acc is None else acc + part
    return acc


def _cumsum_rows(lcum3, da):
    hi = da.astype(BF16)
    rest = da - hi.astype(F32)
    mid = rest.astype(BF16)
    lo = (rest - mid.astype(F32)).astype(BF16)
    return jnp.dot(lcum3, jnp.concatenate([hi, mid, lo], axis=0), preferred_element_type=F32)


def _gate_norm(y_g, z_g, ng_g):
    return _rms(y_g * _silu(z_g), ng_g)


def _ssm_prompt_kernel(x_ref, mod_ref, g_ref, w_ref, cw_ref, cb_ref, dtb_ref, alog_ref, dexp_ref, lcum_ref, e_ref,
                       ng_ref, wout_ref, xo_ref, cnew_ref, s_ref, xp_ref, z_ref, xc_ref, yn_ref, st_ref):
    _, tl, _ = x_ref.shape
    j = pl.program_id(1)

    @pl.when(j == 0)
    def _():
        xp_ref[0:CONV_PAD, :] = jnp.zeros((CONV_PAD, CONV_DIM), F32)
        st_ref[...] = jnp.zeros_like(st_ref)

    h = _norm_modulate(x_ref[...], g_ref[...], mod_ref[0], mod_ref[1])
    dt_all = _softplus(jnp.dot(h, w_ref[:, D_ZX:], preferred_element_type=F32) + dtb_ref[...])

    sub = lax.broadcasted_iota(jnp.int32, (1, SUBLANES, 1), 1)
    n_lane_blocks = CONV_DIM // LANE_TILE
    for lb in range(n_lane_blocks):
        if lb % (n_lane_blocks // 2) == 0:
            xp_ref[CONV_PAD:CONV_PAD + tl, lb * LANE_TILE:lb * LANE_TILE + CONV_DIM // 2] = jnp.dot(
                h, w_ref[:, D_INNER + lb * LANE_TILE:D_INNER + lb * LANE_TILE + CONV_DIM // 2],
                preferred_element_type=F32)
        lns = slice(lb * LANE_TILE, (lb + 1) * LANE_TILE)
        taps = [cw_ref[k:k + 1, lns].reshape(1, 1, LANE_TILE) for k in range(CONV_W)]
        bias = cb_ref[:, lns].reshape(1, 1, LANE_TILE)
        for rb in range(tl // ROW_TILE):
            r0 = rb * ROW_TILE
            cur = xp_ref[CONV_PAD + r0:CONV_PAD + r0 + ROW_TILE, lns]
            prv = xp_ref[r0:r0 + ROW_TILE, lns]
            cur = cur.reshape(ROW_TILE // SUBLANES, SUBLANES, LANE_TILE)
            prv = prv.reshape(ROW_TILE // SUBLANES, SUBLANES, LANE_TILE)
            conv = bias + cur * taps[CONV_W - 1]
            for s in range(1, CONV_W):
                conv = conv + pltpu.roll(jnp.where(sub < SUBLANES - s, cur, prv), s, axis=1) * taps[CONV_W - 1 - s]
            xc_ref[r0:r0 + ROW_TILE, lns] = _silu(conv).reshape(ROW_TILE, LANE_TILE)
    cnew_ref[0] = xp_ref[tl + CONV_PAD - (CONV_W - 1):tl + CONV_PAD, :]
    xp_ref[0:CONV_PAD, :] = xp_ref[tl:tl + CONV_PAD, :]
    z_ref[...] = jnp.dot(h, w_ref[:, :D_INNER], preferred_element_type=F32)

    a_neg = -jnp.exp(alog_ref[...])
    lcum3 = lcum_ref[...]
    causal = lcum3[:, :SSM_CHUNK].astype(F32) > 0.5
    hmask = _head_masks()
    for c in range(tl // SSM_CHUNK):
        rows = slice(c * SSM_CHUNK, (c + 1) * SSM_CHUNK)
        dt = dt_all[rows]
        acum = _cumsum_rows(lcum3, dt * a_neg)
        alast = acum[SSM_CHUNK - 1:SSM_CHUNK, :]
        acum2 = acum * LOG2E
        ldt_t = jnp.log2(dt).T
        src_t = acum2.T - ldt_t
        heads = jnp.concatenate([jnp.exp(acum), dt * jnp.exp(alast - acum)], axis=0)
        heads_hi = heads.astype(BF16)
        heads_lo = (heads - heads_hi.astype(F32)).astype(BF16)
        ex = jnp.dot(jnp.concatenate([heads_hi, heads_lo], axis=1), e_ref[...], preferred_element_type=F32)
        for g in range(SSM_GROUPS):
            ch = slice(g * SSM_GROUP_W, (g + 1) * SSM_GROUP_W)
            ea_x = ex[:SSM_CHUNK, ch]
            w_x = ex[SSM_CHUNK:, ch]
            xs = xc_ref[rows, ch]
            xs_bf = xs.astype(BF16)
            if g % 2 == 0:
                b_lo = D_INNER + g * D_STATE
                c_lo = D_INNER + (SSM_GROUPS + g) * D_STATE
                bgt_pair = [xc_ref[rows, b_lo + k * D_STATE:b_lo + (k + 1) * D_STATE].T.astype(BF16) for k in (0, 1)]
                cg_pair = [xc_ref[rows, c_lo + k * D_STATE:c_lo + (k + 1) * D_STATE].astype(BF16) for k in (0, 1)]
                zero = jnp.zeros((D_STATE, SSM_CHUNK), BF16)
                bdiag = jnp.concatenate([jnp.concatenate([bgt_pair[0], zero], axis=1),
                                         jnp.concatenate([zero, bgt_pair[1]], axis=1)], axis=0)
                cb_pair = jnp.dot(jnp.concatenate(cg_pair, axis=1), bdiag, preferred_element_type=F32)
            bgt = bgt_pair[g % 2]
            cg = cg_pair[g % 2]
            cbm = jnp.where(causal, cb_pair[:, (g % 2) * SSM_CHUNK:(g % 2 + 1) * SSM_CHUNK], 0.0)
            lmats = []
            for hr in range(HEADS_PER_GROUP):
                hh = g * HEADS_PER_GROUP + hr
                seg = jnp.minimum(acum2[:, hh:hh + 1] - src_t[hh:hh + 1, :], ldt_t[hh:hh + 1, :])
                lmats.append((cbm * jnp.exp2(seg)).astype(BF16))
            x_stack = jnp.concatenate([xs_bf * hmask[hr] for hr in range(HEADS_PER_GROUP)], axis=0)
            y_g = xs * dexp_ref[:, ch] + jnp.dot(jnp.concatenate(lmats, axis=1), x_stack,
                                                 preferred_element_type=F32)
            s_old = st_ref[:, ch]
            y_g = y_g + jnp.dot(cg, s_old.astype(BF16), preferred_element_type=F32) * ea_x
            st_ref[:, ch] = (s_old * ea_x[SSM_CHUNK - 1:SSM_CHUNK, :]
                             + jnp.dot(bgt, (xs * w_x).astype(BF16), preferred_element_type=F32))
            yn_ref[rows, ch] = _gate_norm(y_g, z_ref[rows, ch], ng_ref[:, ch]).astype(BF16)

    out = jnp.dot(yn_ref[...], wout_ref[...], preferred_element_type=F32)
    xo_ref[0] = x_ref[0] + mod_ref[2, 0] * out

    @pl.when(j == pl.num_programs(1) - 1)
    def _():
        s_ref[0] = st_ref[...].T


def _ssd_sample_kernel(xbc_ref, dt_ref, z_ref, alog_ref, dexp_ref, lcum_ref, bd_ref, e_ref, ng_ref, s0_ref,
                       yn_ref, s_ref):
    r, tl, _ = xbc_ref.shape
    tm = r * tl
    xbc = xbc_ref[...].reshape(tm, CONV_DIM)
    xs = xbc[:, :D_INNER]
    dt = dt_ref[...].reshape(tm, SSM_HEADS)
    da = dt * (-jnp.exp(alog_ref[...]))
    lcum3 = lcum_ref[...]
    causal = lcum3[:, :tm].astype(F32) > 0.5
    acum = _cumsum_rows(lcum3, da)
    alast = _cumsum_rows(bd_ref[...], da)
    acum_t = acum.T
    dec_end_t = jnp.exp(alast.T)
    dt_x, ea_x, w_x = _expand_heads([dt, jnp.exp(acum), dt * jnp.exp(alast - acum)], e_ref[...])
    xdt = (xs * dt_x).astype(BF16)
    xw = xs * w_x
    hmask = _head_masks()
    row = lax.broadcasted_iota(jnp.int32, (tm, 1), 0) // tl
    dec_cols = [jnp.broadcast_to(dec_end_t[:, b * tl:b * tl + 1], (SSM_HEADS, D_STATE)) for b in range(r)]
    for g in range(SSM_GROUPS):
        ch = slice(g * SSM_GROUP_W, (g + 1) * SSM_GROUP_W)
        bg = xbc[:, D_INNER + g * D_STATE:D_INNER + (g + 1) * D_STATE].astype(BF16)
        cg32 = xbc[:, D_INNER + (SSM_GROUPS + g) * D_STATE:D_INNER + (SSM_GROUPS + g + 1) * D_STATE]
        cbm = jnp.where(causal, lax.dot_general(cg32.astype(BF16), bg, NT_DIMS, preferred_element_type=F32), 0.0)
        y_g = xs[:, ch] * dexp_ref[:, ch] + _ssd_intra(g, cbm, acum, acum_t, xdt[:, ch], hmask)
        y_off = jnp.zeros((tm, SSM_GROUP_W), F32)
        xw_g = xw[:, ch]
        for b in range(r):
            s_old = s0_ref[b, ch, :]
            c_b = jnp.where(row == b, cg32, 0.0).astype(BF16)
            xw_b = jnp.where(row == b, xw_g, 0.0).astype(BF16)
            y_off = y_off + lax.dot_general(c_b, s_old.astype(BF16), NT_DIMS, preferred_element_type=F32)
            upd = lax.dot_general(xw_b, bg, TN_DIMS, preferred_element_type=F32)
            dec = jnp.concatenate(
                [jnp.broadcast_to(dec_cols[b][g * HEADS_PER_GROUP + hr:g * HEADS_PER_GROUP + hr + 1, :],
                                  (SSM_HEAD_DIM, D_STATE)) for hr in range(HEADS_PER_GROUP)], axis=0)
            s_ref[b, ch, :] = s_old * dec + upd
        z_g = z_ref[:, :, ch].reshape(tm, SSM_GROUP_W)
        yn_g = _gate_norm(y_g + y_off * ea_x[:, ch], z_g, ng_ref[:, ch])
        yn_ref[:, :, ch] = yn_g.reshape(r, tl, SSM_GROUP_W).astype(BF16)


def _ssd_consts(d_skip):
    e_bf = jnp.repeat(jnp.eye(SSM_HEADS, dtype=F32), SSM_HEAD_DIM, axis=1).astype(BF16)
    dexp = jnp.repeat(d_skip, SSM_HEAD_DIM).reshape(1, D_INNER)
    return e_bf, dexp


def _ssm_prompt(x, mod, layer, row0, tl, g, w_in, conv_w, conv_b, dt_bias, a_log, d_skip, norm_g, w_out):
    nb, l, d = x.shape
    assert l % tl == 0 and tl % SSM_CHUNK == 0 and tl % ROW_TILE == 0
    e_bf, dexp = _ssd_consts(d_skip)
    return pl.pallas_call(
        _ssm_prompt_kernel,
        grid=(nb, l // tl),
        in_specs=[pl.BlockSpec((1, tl, d), lambda i, j: (i, j, 0)),
                  _mod_spec(layer, 0, 1, row0),
                  _const_spec((1, d)),
                  _const_spec((d, D_ZX + SSM_HEADS)),
                  _const_spec((CONV_W, CONV_DIM)),
                  _const_spec((1, CONV_DIM)),
                  _const_spec((1, SSM_HEADS)),
                  _const_spec((1, SSM_HEADS)),
                  _const_spec((1, D_INNER)),
                  _const_spec((SSM_CHUNK, 3 * SSM_CHUNK)),
                  _const_spec((2 * SSM_HEADS, D_INNER)),
                  _const_spec((1, D_INNER)),
                  _const_spec((D_INNER, d))],
        out_specs=[pl.BlockSpec((1, tl, d), lambda i, j: (i, j, 0)),
                   pl.BlockSpec((1, CONV_W - 1, CONV_DIM), lambda i, j: (i, 0, 0)),
                   pl.BlockSpec((1, D_INNER, D_STATE), lambda i, j: (i, 0, 0))],
        out_shape=[jax.ShapeDtypeStruct((nb, l, d), F32),
                   jax.ShapeDtypeStruct((nb, CONV_W - 1, CONV_DIM), F32),
                   jax.ShapeDtypeStruct((nb, D_INNER, D_STATE), F32)],
        scratch_shapes=[pltpu.VMEM((CONV_PAD + tl, CONV_DIM), F32),
                        pltpu.VMEM((tl, D_INNER), F32),
                        pltpu.VMEM((tl, CONV_DIM), F32),
                        pltpu.VMEM((tl, D_INNER), BF16),
                        pltpu.VMEM((D_STATE, D_INNER), F32)],
        compiler_params=_params("parallel", "arbitrary"),
        name="ssm_prompt",
    )(x, mod, g.reshape(1, d), w_in.astype(BF16), conv_w, conv_b.reshape(1, CONV_DIM),
      dt_bias.reshape(1, SSM_HEADS), a_log.reshape(1, SSM_HEADS), dexp,
      jnp.tile(jnp.tril(jnp.ones((SSM_CHUNK, SSM_CHUNK), BF16)), (1, 3)), jnp.concatenate([e_bf, e_bf], axis=0),
      norm_g.reshape(1, D_INNER), w_out.astype(BF16))


def _ssd_sample(xbc, dt, z, a_log, d_skip, norm_g, state0, r):
    nb, l, _ = xbc.shape
    tm = r * l
    assert l <= SSM_CHUNK and nb % r == 0
    eye_r = jnp.eye(r, dtype=F32)
    lcum = jnp.tile(jnp.kron(eye_r, jnp.tril(jnp.ones((l, l), F32))).astype(BF16), (1, 3))
    bd = jnp.tile(jnp.kron(eye_r, jnp.ones((l, l), F32)).astype(BF16), (1, 3))
    e_bf, dexp = _ssd_consts(d_skip)
    return pl.pallas_call(
        _ssd_sample_kernel,
        grid=(nb // r,),
        in_specs=[pl.BlockSpec((r, l, CONV_DIM), lambda i: (i, 0, 0)),
                  pl.BlockSpec((r, l, SSM_HEADS), lambda i: (i, 0, 0)),
                  pl.BlockSpec((r, l, D_INNER), lambda i: (i, 0, 0)),
                  _const_spec((1, SSM_HEADS)),
                  _const_spec((1, D_INNER)),
                  _const_spec((tm, 3 * tm)),
                  _const_spec((tm, 3 * tm)),
                  _const_spec((SSM_HEADS, D_INNER)),
                  _const_spec((1, D_INNER)),
                  pl.BlockSpec((r, D_INNER, D_STATE), lambda i: (i, 0, 0))],
        out_specs=[pl.BlockSpec((r, l, D_INNER), lambda i: (i, 0, 0)),
                   pl.BlockSpec((r, D_INNER, D_STATE), lambda i: (i, 0, 0))],
        out_shape=[jax.ShapeDtypeStruct((nb, l, D_INNER), BF16),
                   jax.ShapeDtypeStruct((nb, D_INNER, D_STATE), F32)],
        compiler_params=_params("parallel"),
        name="ssd_scan_sample",
    )(xbc, dt, z, a_log.reshape(1, SSM_HEADS), dexp, lcum, bd, e_bf, norm_g.reshape(1, D_INNER), state0)


def _ssm_out_kernel(x_ref, yn_ref, mod_ref, wout_ref, o_ref):
    r, tl, d = x_ref.shape
    if r == 1:
        yn = yn_ref[0]
    else:
        yn = yn_ref[...].astype(F32).reshape(r * tl, D_INNER).astype(BF16)
    out = jnp.dot(yn, wout_ref[...], preferred_element_type=F32)
    o_ref[...] = x_ref[...] + mod_ref[2] * out.reshape(r, tl, d)


def _ssm_out(x, yn, mod, layer, row0, r, tl, w_out):
    nb, l, d = x.shape
    return pl.pallas_call(
        _ssm_out_kernel,
        grid=(nb // r, l // tl),
        in_specs=[pl.BlockSpec((r, tl, d), lambda i, j: (i, j, 0)),
                  pl.BlockSpec((r, tl, D_INNER), lambda i, j: (i, j, 0)),
                  _mod_spec(layer, 0, r, row0),
                  _const_spec((D_INNER, d))],
        out_specs=pl.BlockSpec((r, tl, d), lambda i, j: (i, j, 0)),
        out_shape=jax.ShapeDtypeStruct((nb, l, d), F32),
        compiler_params=_params("parallel", "parallel"),
        name="ssm_out_proj",
    )(x, yn, mod, w_out.astype(BF16))


def _trunk(x, mod, row0, tiles, ssm_state, conv_state, p):
    nb, l, _ = x.shape
    new_v, new_ssm, new_conv = [], [], []
    for i in range(DEPTH):
        j = i // 2
        if i % 2 == 0:
            x, v = _gm_layer(x, mod, i, row0, *tiles["gm"], p["norm1_g"][i], p["gm_w_in"][j], p["gm_ln_g"][j],
                             p["gm_ln_b"][j], p["gm_w_s"][j], p["gm_b_s"][j], p["gm_w_out"][j])
            new_v.append(v)
        else:
            if ssm_state is None:
                x, cnew, s_new = _ssm_prompt(x, mod, i, row0, tiles["ssm"], p["norm1_g"][i], p["ssm_w_in"][j],
                                             p["ssm_conv_w"][j], p["ssm_conv_b"][j], p["ssm_dt_bias"][j],
                                             p["ssm_a_log"][j], p["ssm_d"][j], p["ssm_norm_g"][j],
                                             p["ssm_w_out"][j])
            else:
                z, xbc, dt, cnew = _ssm_in(x, mod, i, row0, *tiles["ssm_in"], p["norm1_g"][i], p["ssm_w_in"][j],
                                           p["ssm_conv_w"][j], p["ssm_conv_b"][j], p["ssm_dt_bias"][j],
                                           conv_state[j])
                s0 = ssm_state[j].reshape(nb, D_INNER, D_STATE)
                yn, s_new = _ssd_sample(xbc, dt, z, p["ssm_a_log"][j], p["ssm_d"][j], p["ssm_norm_g"][j], s0,
                                        tiles["ssd"])
                x = _ssm_out(x, yn, mod, i, row0, *tiles["ssm_out"], p["ssm_w_out"][j])
            new_conv.append(cnew)
            new_ssm.append(s_new.reshape(nb, SSM_HEADS, SSM_HEAD_DIM, D_STATE))
        x = _mlp_layer(x, mod, i, row0, *tiles["mlp"], p["norm2_g"][i], p["mlp_w1_bf"], p["mlp_w2_bf"],
                       p["final_g"], final=(i == DEPTH - 1))
    return x, jnp.stack(new_v), jnp.stack(new_ssm), jnp.stack(new_conv)


PROMPT_TILES = {"gm": (1, 512), "mlp": (1, 1024), "ssm": 512}
SAMPLE_TILES = {"gm": (64, 8), "mlp": (64, 8), "ssm_in": (32, 8), "ssd": 8, "ssm_out": (64, 8)}


def kernel(x_prompt, x_sample, c_prompt, c_sample, state_ssm, state_conv, ada_w, ada_b, norm1_g, norm2_g, gm_w_in, gm_ln_g, gm_ln_b, gm_w_s, gm_b_s, gm_w_out, ssm_w_in, ssm_conv_w, ssm_conv_b, ssm_dt_bias, ssm_a_log, ssm_d, ssm_norm_g, ssm_w_out, mlp_w1, mlp_w2, final_g):
    p = dict(norm1_g=norm1_g, norm2_g=norm2_g, gm_w_in=gm_w_in, gm_ln_g=gm_ln_g, gm_ln_b=gm_ln_b, gm_w_s=gm_w_s,
             gm_b_s=gm_b_s, gm_w_out=gm_w_out, ssm_w_in=ssm_w_in, ssm_conv_w=ssm_conv_w, ssm_conv_b=ssm_conv_b,
             ssm_dt_bias=ssm_dt_bias, ssm_a_log=ssm_a_log, ssm_d=ssm_d, ssm_norm_g=ssm_norm_g, ssm_w_out=ssm_w_out,
             mlp_w1_bf=mlp_w1.astype(BF16), mlp_w2_bf=mlp_w2.astype(BF16), final_g=final_g)
    n_sample = x_sample.shape[0]
    mod = _ada(jnp.concatenate([c_sample, c_prompt], axis=0), ada_w, ada_b)
    y_p, v_p, ssm_p, conv_p = _trunk(x_prompt, mod, n_sample, PROMPT_TILES, None, None, p)
    y_s, v_s, ssm_s, conv_s = _trunk(x_sample, mod, 0, SAMPLE_TILES, state_ssm, state_conv, p)
    return (y_p, y_s, v_p, v_s, ssm_p, conv_p, ssm_s, conv_s)
```
